```python
import math
import jax, jax.numpy as jnp
from jax import lax
import numpy as np

D_MODEL = 2048
BATCH = 16
SEQ = 2048
DEPTH = 2

HEAD_DIM = 64
HEADS_PER_GROUP = 8
DILATED_GROUPS = ((128, 1), (512, 4), (2048, 16))
N_GROUPS = len(DILATED_GROUPS)
N_ATTN_HEADS = N_GROUPS * HEADS_PER_GROUP
ATTN_WIDTH = N_ATTN_HEADS * HEAD_DIM
ATTN_OUT_WIDTH = HEADS_PER_GROUP * HEAD_DIM
ROPE_THETA = 10000.0
HYENA_WIDTH = D_MODEL // 2
HYENA_ORDER = 2
HYENA_EMB_DIM = 33
HYENA_FILTER_HIDDEN = 64
HYENA_DECAY_TARGET = 1e-2
HYENA_FAST_DECAY = 0.3
HYENA_SLOW_DECAY = 1.5
SHORT_CONV = 3
D_FF = 5632
IN_WIDTH = 3 * ATTN_WIDTH + (HYENA_ORDER + 1) * HYENA_WIDTH
RMS_EPS = 1e-6
MASK_VALUE = -1e30

kernel_name = "hybrid_dilated_attn_hyena_convffn_encoder"


def rmsnorm(x, g):
    xf = x.astype(jnp.float32)
    y = xf * lax.rsqrt(jnp.mean(xf * xf, axis=-1, keepdims=True) + RMS_EPS)
    return (y * g.astype(jnp.float32)).astype(x.dtype)


def dwconv3(x, w, b):
    xp = jnp.pad(x, ((0, 0), (1, 1), (0, 0)))
    return xp[:, :-2] * w[0] + xp[:, 1:-1] * w[1] + xp[:, 2:] * w[2] + b


def rope_tables(S):
    pos = jnp.arange(S, dtype=jnp.float32)
    inv = 1.0 / (ROPE_THETA ** (jnp.arange(0, HEAD_DIM, 2, dtype=jnp.float32) / HEAD_DIM))
    ang = pos[:, None] * inv[None, :]
    ang = jnp.concatenate([ang, ang], axis=-1)
    return jnp.cos(ang), jnp.sin(ang)


def apply_rope(x, cos, sin):
    half = HEAD_DIM // 2
    rot = jnp.concatenate([-x[..., half:], x[..., :half]], axis=-1)
    return x * cos[None, :, None, :] + rot * sin[None, :, None, :]


def dilated_window_attention(q, k, v, window, dilation):
    B, S, H, Dh = q.shape
    r = dilation
    nside = window // (2 * r)
    blk = nside
    T = S // r
    nb = -(-T // blk)
    Tp = nb * blk

    def to_sub(a):
        return a.reshape(B, T, r, H, Dh).transpose(0, 2, 1, 3, 4)

    qs = jnp.pad(to_sub(q), ((0, 0), (0, 0), (0, Tp - T), (0, 0), (0, 0)))
    qb = qs.reshape(B, r, nb, blk, H, Dh)

    def band(a):
        ap = jnp.pad(to_sub(a), ((0, 0), (0, 0), (blk, blk + Tp - T), (0, 0), (0, 0)))
        ab = ap.reshape(B, r, nb + 2, blk, H, Dh)
        return jnp.concatenate([ab[:, :, :-2], ab[:, :, 1:-1], ab[:, :, 2:]], axis=3)

    kw = band(k)
    vw = band(v)
    u = jnp.arange(blk)[:, None]
    s = jnp.arange(3 * blk)[None, :]
    b = jnp.arange(nb)[:, None, None]
    tk = (b - 1) * blk + s
    delta = s - blk - u
    valid = (jnp.abs(delta) <= nside)[None] & (tk >= 0) & (tk < T)
    scores = jnp.einsum('brnqhd,brnkhd->brnhqk', qb * (1.0 / math.sqrt(Dh)), kw)
    scores = jnp.where(valid[None, None, :, None], scores, MASK_VALUE)
    lse = jax.nn.logsumexp(scores, axis=-1)
    p = jnp.exp(scores - lse[..., None])
    o = jnp.einsum('brnhqk,brnkhd->brnqhd', p, vw)
    o = o.reshape(B, r, Tp, H, Dh)[:, :, :T].transpose(0, 2, 1, 3, 4).reshape(B, S, H, Dh)
    lse = lse.transpose(0, 1, 2, 4, 3).reshape(B, r, Tp, H)[:, :, :T]
    lse = lse.transpose(0, 2, 1, 3).reshape(B, S, H)
    return o, lse


def hyena_position_features(L):
    t = jnp.linspace(0.0, 1.0, L, dtype=jnp.float32)[:, None]
    bands = (HYENA_EMB_DIM - 1) // 2
    w = 2.0 * math.pi * jnp.arange(L, dtype=jnp.float32)[:, None] / L
    f = jnp.linspace(1e-4, bands - 1, bands, dtype=jnp.float32)[None, :]
    feats = jnp.concatenate([t, jnp.cos(f * w), -jnp.sin(f * w)], axis=-1)
    return feats, t


def hyena_filter_spectrum(feats, t, f_w1, f_b1, f_freq, f_w2, f_b2, f_w3):
    L = feats.shape[0]
    f32 = jnp.float32
    h = jnp.sin(f_freq[0].astype(f32) * (feats @ f_w1.astype(f32) + f_b1.astype(f32)))
    h = jnp.sin(f_freq[1].astype(f32) * (h @ f_w2.astype(f32) + f_b2.astype(f32)))
    h = (h @ f_w3.astype(f32)).reshape(L, HYENA_ORDER, 2, HYENA_WIDTH)
    max_decay = math.log(HYENA_DECAY_TARGET) / HYENA_FAST_DECAY
    min_decay = math.log(HYENA_DECAY_TARGET) / HYENA_SLOW_DECAY
    deltas = jnp.linspace(min_decay, max_decay, HYENA_WIDTH, dtype=f32)
    decay = jnp.exp(-t * jnp.abs(deltas)[None, :])
    h = h * decay[:, None, None, :]
    fwd, bwd = h[:, :, 0], h[:, :, 1]
    k = jnp.concatenate([fwd, jnp.zeros_like(fwd[:1]), bwd[:0:-1]], axis=0)
    k = k / jnp.sum(jnp.abs(k), axis=0, keepdims=True)
    return jnp.fft.rfft(k, axis=0)


def bidirectional_long_conv(z, kf, skip):
    L = z.shape[1]
    zf = jnp.fft.rfft(z, n=2 * L, axis=1)
    y = jnp.fft.irfft(zf * kf[None], n=2 * L, axis=1)[:, :L]
    return y + z * skip.astype(jnp.float32)


def setup_inputs(seed: int = 0) -> dict:
    key = jax.random.key(seed)
    ks = jax.random.split(key, 24)
    n = lambda k, shape, scale: jax.random.normal(k, shape, jnp.float32) * scale
    L_ = DEPTH
    return {
        "x": n(ks[0], (BATCH, SEQ, D_MODEL), 1.0),
        "attn_norm": 1.0 + n(ks[1], (L_, D_MODEL), 0.02),
        "w_in": n(ks[2], (L_, D_MODEL, IN_WIDTH), D_MODEL ** -0.5),
        "hy_conv_w": n(ks[3], (L_, SHORT_CONV, (HYENA_ORDER + 1) * HYENA_WIDTH), SHORT_CONV ** -0.5),
        "hy_conv_b": n(ks[4], (L_, (HYENA_ORDER + 1) * HYENA_WIDTH), 0.02),
        "f_w1": n(ks[5], (L_, HYENA_EMB_DIM, HYENA_FILTER_HIDDEN), HYENA_EMB_DIM ** -0.5),
        "f_b1": n(ks[6], (L_, HYENA_FILTER_HIDDEN), 0.1),
        "f_freq": 1.0 + n(ks[7], (L_, 2, HYENA_FILTER_HIDDEN), 0.1),
        "f_w2": n(ks[8], (L_, HYENA_FILTER_HIDDEN, HYENA_FILTER_HIDDEN), HYENA_FILTER_HIDDEN ** -0.5),
        "f_b2": n(ks[9], (L_, HYENA_FILTER_HIDDEN), 0.1),
        "f_w3": n(ks[10], (L_, HYENA_FILTER_HIDDEN, HYENA_ORDER * 2 * HYENA_WIDTH), HYENA_FILTER_HIDDEN ** -0.5),
        "hy_skip": n(ks[11], (L_, HYENA_ORDER, HYENA_WIDTH), 0.3),
        "w_proj_attn": n(ks[12], (L_, ATTN_OUT_WIDTH, D_MODEL), ATTN_OUT_WIDTH ** -0.5),
        "w_proj_hyena": n(ks[13], (L_, HYENA_WIDTH, D_MODEL), HYENA_WIDTH ** -0.5),
        "w_gate": n(ks[14], (L_, D_MODEL, 2 * D_MODEL), D_MODEL ** -0.5),
        "b_gate": n(ks[15], (L_, 2 * D_MODEL), 0.02),
        "w_out": n(ks[16], (L_, D_MODEL, D_MODEL), D_MODEL ** -0.5),
        "ffn_norm": 1.0 + n(ks[17], (L_, D_MODEL), 0.02),
        "w_up": n(ks[18], (L_, D_MODEL, 2 * D_FF), D_MODEL ** -0.5),
        "ffn_conv_w": n(ks[19], (L_, SHORT_CONV, D_FF), SHORT_CONV ** -0.5),
        "ffn_conv_b": n(ks[20], (L_, D_FF), 0.02),
        "w_down": n(ks[21], (L_, D_FF, D_MODEL), D_FF ** -0.5),
        "final_norm": 1.0 + n(ks[22], (D_MODEL,), 0.02),
    }


def reference(x, attn_norm, w_in, hy_conv_w, hy_conv_b, f_w1, f_b1, f_freq, f_w2, f_b2, f_w3, hy_skip,
              w_proj_attn, w_proj_hyena, w_gate, b_gate, w_out, ffn_norm, w_up, ffn_conv_w, ffn_conv_b,
              w_down, final_norm):
    B, S, _ = x.shape
    f32 = jnp.float32
    cos, sin = rope_tables(S)
    feats, t = hyena_position_features(S)
    for l in range(DEPTH):
        h = rmsnorm(x, attn_norm[l])
        proj = h @ w_in[l]
        q = proj[..., :ATTN_WIDTH]
        k = proj[..., ATTN_WIDTH:2 * ATTN_WIDTH]
        v = proj[..., 2 * ATTN_WIDTH:3 * ATTN_WIDTH]
        u = proj[..., 3 * ATTN_WIDTH:]

        q = apply_rope(q.reshape(B, S, N_ATTN_HEADS, HEAD_DIM).astype(f32), cos, sin)
        k = apply_rope(k.reshape(B, S, N_ATTN_HEADS, HEAD_DIM).astype(f32), cos, sin)
        v = v.reshape(B, S, N_ATTN_HEADS, HEAD_DIM).astype(f32)
        outs, lses = [], []
        for g, (win, dil) in enumerate(DILATED_GROUPS):
            sl = slice(g * HEADS_PER_GROUP, (g + 1) * HEADS_PER_GROUP)
            o_g, lse_g = dilated_window_attention(q[:, :, sl], k[:, :, sl], v[:, :, sl], win, dil)
            outs.append(o_g)
            lses.append(lse_g)
        alpha = jax.nn.softmax(jnp.stack(lses, axis=0), axis=0)
        o_attn = jnp.sum(alpha[..., None] * jnp.stack(outs, axis=0), axis=0)
        o_attn = o_attn.reshape(B, S, ATTN_OUT_WIDTH).astype(x.dtype)

        uc = dwconv3(u, hy_conv_w[l], hy_conv_b[l])
        hv = uc[..., :HYENA_WIDTH]
        gates_h = (uc[..., HYENA_WIDTH:2 * HYENA_WIDTH], uc[..., 2 * HYENA_WIDTH:])
        kf = hyena_filter_spectrum(feats, t, f_w1[l], f_b1[l], f_freq[l], f_w2[l], f_b2[l], f_w3[l])
        z = hv.astype(f32)
        for o in range(HYENA_ORDER):
            z = gates_h[o].astype(f32) * bidirectional_long_conv(z, kf[:, o], hy_skip[l, o])
        o_hy = z.astype(x.dtype)

        gate = jax.nn.sigmoid(h @ w_gate[l] + b_gate[l])
        mixed = gate[..., :D_MODEL] * (o_attn @ w_proj_attn[l]) + gate[..., D_MODEL:] * (o_hy @ w_proj_hyena[l])
        x = x + mixed @ w_out[l]

        h = rmsnorm(x, ffn_norm[l])
        up = h @ w_up[l]
        a = dwconv3(up[..., :D_FF], ffn_conv_w[l], ffn_conv_b[l])
        x = x + (jax.nn.gelu(a) * up[..., D_FF:]) @ w_down[l]
    return rmsnorm(x, final_norm)
```

```python
import functools
import math

import jax
import jax.numpy as jnp
from jax import lax
from jax.experimental import pallas as pl
from jax.experimental.pallas import tpu as pltpu

HEAD_DIM = 64
HEADS_PER_GROUP = 8
DILATED_GROUPS = ((128, 1), (512, 4), (2048, 16))
N_GROUPS = len(DILATED_GROUPS)
GROUP_WIDTH = HEADS_PER_GROUP * HEAD_DIM
ATTN_WIDTH = N_GROUPS * GROUP_WIDTH
ROPE_THETA = 10000.0
HYENA_ORDER = 2
HYENA_EMB_DIM = 33
HYENA_DECAY_TARGET = 1e-2
HYENA_FAST_DECAY = 0.3
HYENA_SLOW_DECAY = 1.5
RMS_EPS = 1e-6
MASK_VALUE = -1e30

V7X_LANES = 128
V7X_VMEM_LIMIT_BYTES = 60 * 1024 * 1024

F32 = jnp.float32
BF16 = jnp.bfloat16
HIGHEST = lax.Precision.HIGHEST


def _params(*semantics):
    return pltpu.CompilerParams(dimension_semantics=semantics, vmem_limit_bytes=V7X_VMEM_LIMIT_BYTES)


def _rmsnorm(x, g):
    return x * lax.rsqrt(jnp.mean(x * x, axis=-1, keepdims=True) + RMS_EPS) * g


def _norm_proj_kernel(x_ref, g_ref, w_ref, cos_ref, sin_lo_ref, sin_hi_ref, o_ref, h_ref, *, n_q_blocks, n_rope_blocks):
    j = pl.program_id(1)

    @pl.when(j == 0)
    def _():
        h_ref[...] = _rmsnorm(x_ref[...], g_ref[...]).astype(BF16)

    acc = jnp.dot(h_ref[...], w_ref[...], preferred_element_type=F32)

    @pl.when(j < n_rope_blocks)
    def _():
        scale = jnp.where(j < n_q_blocks, 1.0 / math.sqrt(HEAD_DIM), 1.0)
        cos, sin_lo, sin_hi = cos_ref[...], sin_lo_ref[...], sin_hi_ref[...]
        for s in range(acc.shape[1] // V7X_LANES):
            a = acc[:, s * V7X_LANES:(s + 1) * V7X_LANES]
            r = a * cos + pltpu.roll(a, V7X_LANES - HEAD_DIM // 2, 1) * sin_lo + pltpu.roll(a, HEAD_DIM // 2, 1) * sin_hi
            o_ref[:, s * V7X_LANES:(s + 1) * V7X_LANES] = (r * scale).astype(o_ref.dtype)

    @pl.when(j >= n_rope_blocks)
    def _():
        o_ref[...] = acc.astype(o_ref.dtype)


def _norm_proj(x2d, gain, w, cos, sin_lo, sin_hi, *, seq, tm=512, tn=GROUP_WIDTH):
    rows, d = x2d.shape
    n = w.shape[1]
    return pl.pallas_call(
        functools.partial(_norm_proj_kernel, n_q_blocks=ATTN_WIDTH // tn, n_rope_blocks=2 * ATTN_WIDTH // tn),
        grid=(rows // tm, n // tn),
        in_specs=[
            pl.BlockSpec((tm, d), lambda i, j: (i, 0)),
            pl.BlockSpec((1, d), lambda i, j: (0, 0)),
            pl.BlockSpec((d, tn), lambda i, j: (0, j)),
            pl.BlockSpec((tm, V7X_LANES), lambda i, j: (i % (seq // tm), 0)),
            pl.BlockSpec((tm, V7X_LANES), lambda i, j: (i % (seq // tm), 0)),
            pl.BlockSpec((tm, V7X_LANES), lambda i, j: (i % (seq // tm), 0)),
        ],
        out_specs=pl.BlockSpec((tm, tn), lambda i, j: (i, j)),
        out_shape=jax.ShapeDtypeStruct((rows, n), BF16),
        scratch_shapes=[pltpu.VMEM((tm, d), BF16)],
        compiler_params=_params("parallel", "arbitrary"),
        name="norm_proj",
    )(x2d, gain.reshape(1, d), w, cos, sin_lo, sin_hi)


def _attn_group(q_ref, k_ref, v_ref, acc_ref, m_ref, l_ref, g, r, t_len, nside):
    qb = 2 * nside
    kw = min(t_len, qb + 2 * nside)
    nb = t_len // qb
    lane_lo = lax.broadcasted_iota(jnp.int32, (1, V7X_LANES), 1) < HEAD_DIM

    def body(idx, carry):
        c = idx // nb
        q0 = pl.multiple_of((idx % nb) * qb, qb)
        ks = pl.multiple_of(jnp.clip(q0 - nside, 0, t_len - kw), nside)
        q = q_ref[0, c, pl.ds(q0, qb), :]
        k = k_ref[0, c, pl.ds(ks, kw), :]
        v = v_ref[0, c, pl.ds(ks, kw), :]
        zero = jnp.zeros_like(q)
        qs = jnp.concatenate([jnp.where(lane_lo, q, zero), jnp.where(lane_lo, zero, q)], axis=0)
        s = lax.dot_general(qs, k, (((1,), (1,)), ((), ())), preferred_element_type=F32)
        tq = lax.broadcasted_iota(jnp.int32, s.shape, 0) & (qb - 1)
        tk = lax.broadcasted_iota(jnp.int32, s.shape, 1)
        valid = jnp.abs(tk - tq + (ks - q0)) <= nside
        s = jnp.where(valid, s, MASK_VALUE)
        m = jnp.max(s, axis=-1, keepdims=True)
        p = jnp.exp(s - m)
        l = jnp.sum(p, axis=-1, keepdims=True)
        pv = jnp.dot(p.astype(BF16), v, preferred_element_type=F32)
        rows = pl.ds(q0 * r + c, qb, stride=r) if r > 1 else pl.ds(q0, qb)
        acc_ref[g, rows, :] = jnp.where(lane_lo, pv[:qb], pv[qb:])
        m_ref[g, rows, :] = jnp.where(lane_lo, m[:qb], m[qb:])
        l_ref[g, rows, :] = jnp.where(lane_lo, l[:qb], l[qb:])
        return carry

    lax.fori_loop(0, r * nb, body, 0)


def _attention_kernel(*refs, seq, chunk):
    qkv = refs[:3 * N_GROUPS]
    o_ref, acc_ref, m_ref, l_ref = refs[3 * N_GROUPS:]
    for g, (window, r) in enumerate(DILATED_GROUPS):
        _attn_group(qkv[3 * g], qkv[3 * g + 1], qkv[3 * g + 2], acc_ref, m_ref, l_ref, g, r, seq // r, window // (2 * r))

    def combine(i, carry):
        rows = pl.ds(pl.multiple_of(i * chunk, chunk), chunk)
        ms = [m_ref[g, rows, :] for g in range(N_GROUPS)]
        top = functools.reduce(jnp.maximum, ms)
        ws = [jnp.exp(m - top) for m in ms]
        num = sum(w * acc_ref[g, rows, :] for g, w in enumerate(ws))
        den = sum(w * l_ref[g, rows, :] for g, w in enumerate(ws))
        o_ref[0, rows, :] = (num / den).astype(o_ref.dtype)
        return carry

    lax.fori_loop(0, seq // chunk, combine, 0)


def _attention(proj3d):
    b, seq, width = proj3d.shape
    lanes_per_group = GROUP_WIDTH // V7X_LANES
    args, specs = [], []
    for g, (_, r) in enumerate(DILATED_GROUPS):
        for part in range(3):
            col = part * ATTN_WIDTH + g * GROUP_WIDTH
            if r == 1:
                args.append(proj3d.reshape(b, 1, seq, width))
                specs.append(pl.BlockSpec((1, 1, seq, V7X_LANES),
                                          lambda i, hp, cb=col // V7X_LANES: (i, 0, 0, cb + hp)))
            else:
                sub = proj3d[:, :, col:col + GROUP_WIDTH].reshape(b, seq // r, r, GROUP_WIDTH).transpose(0, 2, 1, 3)
                args.append(sub)
                specs.append(pl.BlockSpec((1, r, seq // r, V7X_LANES), lambda i, hp: (i, 0, 0, hp)))
    return pl.pallas_call(
        functools.partial(_attention_kernel, seq=seq, chunk=256),
        grid=(b, lanes_per_group),
        in_specs=specs,
        out_specs=pl.BlockSpec((1, seq, V7X_LANES), lambda i, hp: (i, 0, hp)),
        out_shape=jax.ShapeDtypeStruct((b, seq, GROUP_WIDTH), BF16),
        scratch_shapes=[pltpu.VMEM((N_GROUPS, seq, V7X_LANES), F32)] * 3,
        compiler_params=_params("parallel", "parallel"),
        name="attention",
    )(*args)


def _filter_kernel(feats_ref, t_ref, w1_ref, b1_ref, freq_ref, w2_ref, b2_ref, w3f_ref, w3b_ref, delta_ref, sgn_ref,
                   cosm_ref, sinm_ref, kr_ref, ki_ref, sum_ref, dif_ref):
    fb = pl.program_id(2)

    @pl.when(fb == 0)
    def _():
        dot = functools.partial(jnp.dot, precision=HIGHEST, preferred_element_type=F32)
        h = jnp.sin(freq_ref[0:1, :] * (dot(feats_ref[...], w1_ref[...]) + b1_ref[...]))
        h = jnp.sin(freq_ref[1:2, :] * (dot(h, w2_ref[...]) + b2_ref[...]))
        decay = jnp.exp(-t_ref[...] * jnp.abs(delta_ref[...]))
        fwd = dot(h, w3f_ref[...]) * decay
        bwd = dot(h, w3b_ref[...]) * decay
        row = lax.broadcasted_iota(jnp.int32, (fwd.shape[0], 1), 0)
        bwd = jnp.where(row == 0, 0.0, bwd)
        norm = jnp.sum(jnp.abs(fwd), axis=0, keepdims=True) + jnp.sum(jnp.abs(bwd), axis=0, keepdims=True)
        fwd = fwd / norm
        bwd = bwd / norm
        sum_ref[...] = fwd + bwd
        dif_ref[...] = fwd - bwd

    kr = jnp.dot(cosm_ref[...], sum_ref[...], precision=HIGHEST, preferred_element_type=F32)
    ki = jnp.dot(sinm_ref[...], dif_ref[...], precision=HIGHEST, preferred_element_type=F32)
    nyq = jnp.sum(sgn_ref[...] * sum_ref[...], axis=0, keepdims=True)
    row0 = jnp.logical_and(lax.broadcasted_iota(jnp.int32, (kr.shape[0], 1), 0) == 0, fb == 0)
    kr_ref[0] = jnp.where(row0, 0.5 * kr, kr)
    ki_ref[0] = jnp.where(row0, nyq, ki)


def _filter_spectrum(feats, t, w1, b1, freq, w2, b2, w3, deltas, sgn, cosm, sinm, *, tc=256, tf=512):
    length = feats.shape[0]
    hid = w2.shape[0]
    chans = deltas.shape[1]
    ncb = chans // tc
    const = lambda o, cb, fb: (0, 0)
    out = jax.ShapeDtypeStruct((HYENA_ORDER, length, chans), F32)
    return pl.pallas_call(
        _filter_kernel,
        grid=(HYENA_ORDER, ncb, length // tf),
        in_specs=[
            pl.BlockSpec(feats.shape, const),
            pl.BlockSpec((length, 1), const),
            pl.BlockSpec(w1.shape, const),
            pl.BlockSpec((1, hid), const),
            pl.BlockSpec((2, hid), const),
            pl.BlockSpec((hid, hid), const),
            pl.BlockSpec((1, hid), const),
            pl.BlockSpec((hid, tc), lambda o, cb, fb: (0, (2 * o) * ncb + cb)),
            pl.BlockSpec((hid, tc), lambda o, cb, fb: (0, (2 * o + 1) * ncb + cb)),
            pl.BlockSpec((1, tc), lambda o, cb, fb: (0, cb)),
            pl.BlockSpec((length, 1), const),
            pl.BlockSpec((tf, length), lambda o, cb, fb: (fb, 0)),
            pl.BlockSpec((tf, length), lambda o, cb, fb: (fb, 0)),
        ],
        out_specs=[pl.BlockSpec((1, tf, tc), lambda o, cb, fb: (o, fb, cb))] * 2,
        out_shape=[out, out],
        scratch_shapes=[pltpu.VMEM((length, tc), F32)] * 2,
        compiler_params=_params("parallel", "parallel", "arbitrary"),
        name="hyena_filter",
    )(feats, t, w1, b1.reshape(1, hid), freq, w2, b2.reshape(1, hid), w3, w3, deltas, sgn, cosm, sinm)


def _dwconv3_full(a, w_ref, b_ref):
    n = a.shape[0]
    row = lax.broadcasted_iota(jnp.int32, (n, 1), 0)
    prev = jnp.where(row == 0, 0.0, pltpu.roll(a, 1, 0))
    nxt = jnp.where(row == n - 1, 0.0, pltpu.roll(a, n - 1, 0))
    return prev * w_ref[0:1, :] + a * w_ref[1:2, :] + nxt * w_ref[2:3, :] + b_ref[...]


def _long_conv_kernel(*refs, z_from_u):
    if z_from_u:
        uz_ref, cwz_ref, cbz_ref, ug_ref, cwg_ref, cbg_ref, kr_ref, ki_ref, skip_ref, sgn_ref, cosm_ref, sinm_ref, o_ref = refs
        z = _dwconv3_full(uz_ref[0].astype(F32), cwz_ref, cbz_ref)
    else:
        z_ref, ug_ref, cwg_ref, cbg_ref, kr_ref, ki_ref, skip_ref, sgn_ref, cosm_ref, sinm_ref, o_ref = refs
        z = z_ref[0]
    gate = _dwconv3_full(ug_ref[0].astype(F32), cwg_ref, cbg_ref)
    n = z.shape[0]
    zb = z.astype(BF16)
    cosm, sinm, sgn = cosm_ref[...], sinm_ref[...], sgn_ref[...]
    kr, ki = kr_ref[0], ki_ref[0]
    xr = jnp.dot(cosm, zb, preferred_element_type=F32)
    xi = jnp.dot(sinm, zb, preferred_element_type=F32)
    xnyq = jnp.sum(sgn * zb.astype(F32), axis=0, keepdims=True)
    yr = (xr * kr - xi * ki).astype(BF16)
    yi = (xr * ki + xi * kr).astype(BF16)
    y = jnp.dot(cosm, yr, preferred_element_type=F32) + jnp.dot(sinm, yi, preferred_element_type=F32)
    y = (y + 0.5 * sgn * (xnyq * ki[0:1, :])) * (1.0 / n)
    o_ref[0] = (gate * (y + z * skip_ref[0])).astype(o_ref.dtype)


def _long_conv(z_src, proj3d, conv_w, conv_b, kr, ki, skip, sgn, cosm, sinm, *, order, z_part, gate_part, u_col0, out_dtype, tc=256):
    b, seq, _ = proj3d.shape
    chans = kr.shape[2]
    ncb = chans // tc
    u_cb0 = u_col0 // tc

    def u_specs(part):
        return [
            pl.BlockSpec((1, seq, tc), lambda cb, i: (i, 0, u_cb0 + part * ncb + cb)),
            pl.BlockSpec((3, tc), lambda cb, i: (0, part * ncb + cb)),
            pl.BlockSpec((1, tc), lambda cb, i: (0, part * ncb + cb)),
        ]

    conv_b2 = conv_b.reshape(1, -1)
    if z_src is None:
        args, specs = [proj3d, conv_w, conv_b2], u_specs(z_part)
    else:
        args, specs = [z_src], [pl.BlockSpec((1, seq, tc), lambda cb, i: (i, 0, cb))]
    args += [proj3d, conv_w, conv_b2, kr, ki, skip.reshape(HYENA_ORDER, 1, chans), sgn, cosm, sinm]
    specs += u_specs(gate_part) + [
        pl.BlockSpec((1, seq, tc), lambda cb, i: (order, 0, cb)),
        pl.BlockSpec((1, seq, tc), lambda cb, i: (order, 0, cb)),
        pl.BlockSpec((1, 1, tc), lambda cb, i: (order, 0, cb)),
        pl.BlockSpec((seq, 1), lambda cb, i: (0, 0)),
        pl.BlockSpec((seq, seq), lambda cb, i: (0, 0), pipeline_mode=pl.Buffered(1)),
        pl.BlockSpec((seq, seq), lambda cb, i: (0, 0), pipeline_mode=pl.Buffered(1)),
    ]
    return pl.pallas_call(
        functools.partial(_long_conv_kernel, z_from_u=z_src is None),
        grid=(ncb, b),
        in_specs=specs,
        out_specs=pl.BlockSpec((1, seq, tc), lambda cb, i: (i, 0, cb)),
        out_shape=jax.ShapeDtypeStruct((b, seq, chans), out_dtype),
        compiler_params=_params("parallel", "parallel"),
        name=f"long_conv{order}",
    )(*args)


def _merge_kernel(x_ref, g_ref, oa_ref, oh_ref, wga_ref, wgh_ref, bga_ref, bgh_ref, wpa_ref, wph_ref, wo_ref, o_ref, h_ref):
    c = pl.program_id(1)

    @pl.when(c == 0)
    def _():
        x = x_ref[...]
        h_ref[...] = _rmsnorm(x, g_ref[...]).astype(BF16)
        o_ref[...] = x

    h = h_ref[...]
    gate_a = 1.0 / (1.0 + jnp.exp(-(jnp.dot(h, wga_ref[...], preferred_element_type=F32) + bga_ref[...])))
    gate_h = 1.0 / (1.0 + jnp.exp(-(jnp.dot(h, wgh_ref[...], preferred_element_type=F32) + bgh_ref[...])))
    pa = jnp.dot(oa_ref[...], wpa_ref[...], preferred_element_type=F32)
    ph = jnp.dot(oh_ref[...], wph_ref[...], preferred_element_type=F32)
    mixed = (gate_a * pa + gate_h * ph).astype(BF16)
    o_ref[...] += jnp.dot(mixed, wo_ref[...], preferred_element_type=F32)


def _merge(x2d, gain, o_attn, o_hy, w_gate, b_gate, w_pa, w_ph, w_out, *, tm=512, tc=512):
    rows, d = x2d.shape
    ncb = d // tc
    b_gate2 = b_gate.reshape(1, -1)
    return pl.pallas_call(
        _merge_kernel,
        grid=(rows // tm, ncb),
        in_specs=[
            pl.BlockSpec((tm, d), lambda i, c: (i, 0)),
            pl.BlockSpec((1, d), lambda i, c: (0, 0)),
            pl.BlockSpec((tm, o_attn.shape[1]), lambda i, c: (i, 0)),
            pl.BlockSpec((tm, o_hy.shape[1]), lambda i, c: (i, 0)),
            pl.BlockSpec((d, tc), lambda i, c: (0, c)),
            pl.BlockSpec((d, tc), lambda i, c: (0, ncb + c)),
            pl.BlockSpec((1, tc), lambda i, c: (0, c)),
            pl.BlockSpec((1, tc), lambda i, c: (0, ncb + c)),
            pl.BlockSpec((w_pa.shape[0], tc), lambda i, c: (0, c)),
            pl.BlockSpec((w_ph.shape[0], tc), lambda i, c: (0, c)),
            pl.BlockSpec((tc, d), lambda i, c: (c, 0)),
        ],
        out_specs=pl.BlockSpec((tm, d), lambda i, c: (i, 0)),
        out_shape=jax.ShapeDtypeStruct((rows, d), F32),
        scratch_shapes=[pltpu.VMEM((tm, d), BF16)],
        compiler_params=_params("parallel", "arbitrary"),
        name="merge",
    )(x2d, gain.reshape(1, d), o_attn, o_hy, w_gate, w_gate, b_gate2, b_gate2, w_pa, w_ph, w_out)


HALO = 16


def _conv_ffn_kernel(*refs, blocks_per_seq, final_norm):
    if final_norm:
        x_ref, xp_ref, xn_ref, g_ref, wa_ref, wb_ref, cw_ref, cb_ref, wd_ref, gf_ref, o_ref, h_ref = refs
    else:
        x_ref, xp_ref, xn_ref, g_ref, wa_ref, wb_ref, cw_ref, cb_ref, wd_ref, o_ref, h_ref = refs
    i, f = pl.program_id(0), pl.program_id(1)
    tm = x_ref.shape[0]

    @pl.when(f == 0)
    def _():
        x = x_ref[...]
        g = g_ref[...]
        h_ref[0:HALO, :] = _rmsnorm(xp_ref[...], g).astype(BF16)
        h_ref[HALO:HALO + tm, :] = _rmsnorm(x, g).astype(BF16)
        h_ref[HALO + tm:, :] = _rmsnorm(xn_ref[...], g).astype(BF16)
        o_ref[...] = x

    up_a = jnp.dot(h_ref[...], wa_ref[...], preferred_element_type=F32)
    up_b = jnp.dot(h_ref[HALO:HALO + tm, :], wb_ref[...], preferred_element_type=F32)
    n_ext = up_a.shape[0]
    row = lax.broadcasted_iota(jnp.int32, (tm, 1), 0)
    seq_pos = i % blocks_per_seq
    at_start = jnp.logical_and(row == 0, seq_pos == 0)
    at_end = jnp.logical_and(row == tm - 1, seq_pos == blocks_per_seq - 1)
    a_prev = jnp.where(at_start, 0.0, pltpu.roll(up_a, 1, 0)[HALO:HALO + tm])
    a_next = jnp.where(at_end, 0.0, pltpu.roll(up_a, n_ext - 1, 0)[HALO:HALO + tm])
    a = a_prev * cw_ref[0:1, :] + up_a[HALO:HALO + tm] * cw_ref[1:2, :] + a_next * cw_ref[2:3, :] + cb_ref[...]
    gelu = 0.5 * a * (1.0 + jnp.tanh(math.sqrt(2.0 / math.pi) * (a + 0.044715 * (a * a * a))))
    o_ref[...] += jnp.dot((gelu * up_b).astype(BF16), wd_ref[...], preferred_element_type=F32)

    if final_norm:
        @pl.when(f == pl.num_programs(1) - 1)
        def _():
            o_ref[...] = _rmsnorm(o_ref[...], gf_ref[...])


def _conv_ffn(x2d, gain, w_up, conv_w, conv_b, w_down, final_gain, *, seq, tm=512, tf=512):
    rows, d = x2d.shape
    d_ff = w_down.shape[0]
    nfb = d_ff // tf
    halo_per_block = tm // HALO
    n_halo_blocks = rows // HALO
    args = [x2d, x2d, x2d, gain.reshape(1, d), w_up, w_up, conv_w, conv_b.reshape(1, d_ff), w_down]
    specs = [
        pl.BlockSpec((tm, d), lambda i, f: (i, 0)),
        pl.BlockSpec((HALO, d), lambda i, f: (jnp.maximum(i * halo_per_block - 1, 0), 0)),
        pl.BlockSpec((HALO, d), lambda i, f: (jnp.minimum((i + 1) * halo_per_block, n_halo_blocks - 1), 0)),
        pl.BlockSpec((1, d), lambda i, f: (0, 0)),
        pl.BlockSpec((d, tf), lambda i, f: (0, f)),
        pl.BlockSpec((d, tf), lambda i, f: (0, nfb + f)),
        pl.BlockSpec((3, tf), lambda i, f: (0, f)),
        pl.BlockSpec((1, tf), lambda i, f: (0, f)),
        pl.BlockSpec((tf, d), lambda i, f: (f, 0)),
    ]
    if final_gain is not None:
        args.append(final_gain.reshape(1, d))
        specs.append(pl.BlockSpec((1, d), lambda i, f: (0, 0)))
    return pl.pallas_call(
        functools.partial(_conv_ffn_kernel, blocks_per_seq=seq // tm, final_norm=final_gain is not None),
        grid=(rows // tm, nfb),
        in_specs=specs,
        out_specs=pl.BlockSpec((tm, d), lambda i, f: (i, 0)),
        out_shape=jax.ShapeDtypeStruct((rows, d), F32),
        scratch_shapes=[pltpu.VMEM((tm + 2 * HALO, d), BF16)],
        compiler_params=_params("parallel", "arbitrary"),
        name="conv_ffn",
    )(*args)


def _rope_tables(seq):
    pos = jnp.arange(seq, dtype=F32)
    inv = 1.0 / (ROPE_THETA ** (jnp.arange(0, HEAD_DIM, 2, dtype=F32) / HEAD_DIM))
    ang = pos[:, None] * inv[None, :]
    ang = jnp.concatenate([ang, ang] * (V7X_LANES // HEAD_DIM), axis=-1)
    cos, sin = jnp.cos(ang), jnp.sin(ang)
    first_half = (jnp.arange(V7X_LANES) % HEAD_DIM) < HEAD_DIM // 2
    return cos, jnp.where(first_half, -sin, 0.0), jnp.where(first_half, 0.0, sin)


def _hyena_tables(length, chans):
    t = jnp.linspace(0.0, 1.0, length, dtype=F32)[:, None]
    bands = (HYENA_EMB_DIM - 1) // 2
    w = 2.0 * math.pi * jnp.arange(length, dtype=F32)[:, None] / length
    f = jnp.linspace(1e-4, bands - 1, bands, dtype=F32)[None, :]
    feats = jnp.concatenate([t, jnp.cos(f * w), -jnp.sin(f * w)], axis=-1)
    feats = jnp.pad(feats, ((0, 0), (0, V7X_LANES - HYENA_EMB_DIM)))
    max_decay = math.log(HYENA_DECAY_TARGET) / HYENA_FAST_DECAY
    min_decay = math.log(HYENA_DECAY_TARGET) / HYENA_SLOW_DECAY
    deltas = jnp.linspace(min_decay, max_decay, chans, dtype=F32)[None, :]
    idx = jnp.arange(length, dtype=jnp.int32)
    ang = ((idx[:, None] * idx[None, :]) % (2 * length)).astype(F32) * (math.pi / length)
    sgn = (1 - 2 * (idx % 2)).astype(F32)[:, None]
    return feats, t, deltas, sgn, jnp.cos(ang), -jnp.sin(ang)


def kernel(x, attn_norm, w_in, hy_conv_w, hy_conv_b, f_w1, f_b1, f_freq, f_w2, f_b2, f_w3, hy_skip, w_proj_attn, w_proj_hyena, w_gate, b_gate, w_out, ffn_norm, w_up, ffn_conv_w, ffn_conv_b, w_down, final_norm):
    b, seq, d = x.shape
    depth = w_in.shape[0]
    chans = hy_skip.shape[2]
    u_col0 = 3 * ATTN_WIDTH
    cos, sin_lo, sin_hi = _rope_tables(seq)
    feats, t, deltas, sgn, cosm, sinm = _hyena_tables(seq, chans)
    cosm16, sinm16 = cosm.astype(BF16), sinm.astype(BF16)
    w1_pad = jnp.pad(f_w1, ((0, 0), (0, V7X_LANES - HYENA_EMB_DIM), (0, 0)))
    w_in16, w_gate16, w_pa16, w_ph16, w_out16, w_up16, w_down16 = (
        w.astype(BF16) for w in (w_in, w_gate, w_proj_attn, w_proj_hyena, w_out, w_up, w_down))

    x2d = x.reshape(b * seq, d)
    for l in range(depth):
        proj = _norm_proj(x2d, attn_norm[l], w_in16[l], cos, sin_lo, sin_hi, seq=seq)
        proj3d = proj.reshape(b, seq, -1)
        o_attn = _attention(proj3d)
        kr, ki = _filter_spectrum(feats, t, w1_pad[l], f_b1[l], f_freq[l], f_w2[l], f_b2[l], f_w3[l], deltas, sgn, cosm, sinm)
        conv = functools.partial(_long_conv, proj3d=proj3d, conv_w=hy_conv_w[l], conv_b=hy_conv_b[l], kr=kr, ki=ki,
                                 skip=hy_skip[l], sgn=sgn, cosm=cosm16, sinm=sinm16, u_col0=u_col0)
        z1 = conv(None, order=0, z_part=0, gate_part=1, out_dtype=F32)
        o_hy = conv(z1, order=1, z_part=None, gate_part=2, out_dtype=BF16)
        x2d = _merge(x2d, attn_norm[l], o_attn.reshape(b * seq, -1), o_hy.reshape(b * seq, -1),
                     w_gate16[l], b_gate[l], w_pa16[l], w_ph16[l], w_out16[l])
        x2d = _conv_ffn(x2d, ffn_norm[l], w_up16[l], ffn_conv_w[l], ffn_conv_b[l], w_down16[l],
                        final_norm if l == depth - 1 else None, seq=seq)
    return x2d.reshape(b, seq, d)
```

```python
import functools
import math

import jax
import jax.numpy as jnp
from jax import lax
from jax.experimental import pallas as pl
from jax.experimental.pallas import tpu as pltpu

HEAD_DIM = 64
HEADS_PER_GROUP = 8
DILATED_GROUPS = ((128, 1), (512, 4), (2048, 16))
N_GROUPS = len(DILATED_GROUPS)
GROUP_WIDTH = HEADS_PER_GROUP * HEAD_DIM
ATTN_WIDTH = N_GROUPS * GROUP_WIDTH
ROPE_THETA = 10000.0
HYENA_ORDER = 2
HYENA_EMB_DIM = 33
HYENA_DECAY_TARGET = 1e-2
HYENA_FAST_DECAY = 0.3
HYENA_SLOW_DECAY = 1.5
RMS_EPS = 1e-6
MASK_VALUE = -1e30

V7X_LANES = 128
V7X_VMEM_LIMIT_BYTES = 60 * 1024 * 1024

F32 = jnp.float32
BF16 = jnp.bfloat16
HIGHEST = lax.Precision.HIGHEST


def _params(*semantics):
    return pltpu.CompilerParams(dimension_semantics=semantics, vmem_limit_bytes=V7X_VMEM_LIMIT_BYTES)


def _rmsnorm(x, g):
    return x * lax.rsqrt(jnp.mean(x * x, axis=-1, keepdims=True) + RMS_EPS) * g


def _norm_proj_kernel(*refs, rope, n_q_blocks):
    if rope:
        x_ref, g_ref, w_ref, cos_ref, sin_lo_ref, sin_hi_ref, o_ref, h_ref = refs
    else:
        x_ref, g_ref, w_ref, o_ref, h_ref = refs
    j = pl.program_id(1)

    @pl.when(j == 0)
    def _():
        h_ref[...] = _rmsnorm(x_ref[...], g_ref[...]).astype(BF16)

    acc = jnp.dot(h_ref[...], w_ref[...], preferred_element_type=F32)
    if rope:
        scale = jnp.where(j < n_q_blocks, 1.0 / math.sqrt(HEAD_DIM), 1.0)
        cos, sin_lo, sin_hi = cos_ref[...] * scale, sin_lo_ref[...] * scale, sin_hi_ref[...] * scale
        for s in range(acc.shape[1] // V7X_LANES):
            a = acc[:, s * V7X_LANES:(s + 1) * V7X_LANES]
            r = a * cos + pltpu.roll(a, V7X_LANES - HEAD_DIM // 2, 1) * sin_lo + pltpu.roll(a, HEAD_DIM // 2, 1) * sin_hi
            o_ref[:, s * V7X_LANES:(s + 1) * V7X_LANES] = r.astype(o_ref.dtype)
    else:
        o_ref[...] = acc.astype(o_ref.dtype)


def _norm_proj(x2d, gain, w, col0, n, rope_tables, *, seq, tm=1024, tn=GROUP_WIDTH):
    rows, d = x2d.shape
    cb0 = col0 // tn
    rope = rope_tables is not None
    table_spec = pl.BlockSpec((tm, V7X_LANES), lambda i, j: (i % (seq // tm), 0))
    return pl.pallas_call(
        functools.partial(_norm_proj_kernel, rope=rope, n_q_blocks=ATTN_WIDTH // tn),
        grid=(rows // tm, n // tn),
        in_specs=[
            pl.BlockSpec((tm, d), lambda i, j: (i, 0)),
            pl.BlockSpec((1, d), lambda i, j: (0, 0)),
            pl.BlockSpec((d, tn), lambda i, j: (0, cb0 + j)),
        ] + ([table_spec] * 3 if rope else []),
        out_specs=pl.BlockSpec((tm, tn), lambda i, j: (i, j)),
        out_shape=jax.ShapeDtypeStruct((rows, n), BF16),
        scratch_shapes=[pltpu.VMEM((tm, d), BF16)],
        compiler_params=_params("parallel", "arbitrary"),
        name="norm_proj_rope" if rope else "norm_proj",
    )(x2d, gain.reshape(1, d), w, *(rope_tables if rope else ()))


def _attn_group(q_ref, k_ref, v_ref, acc_ref, m_ref, l_ref, g, r, t_len, nside):
    qb = 2 * nside
    kw = min(t_len, qb + 2 * nside)
    nb = t_len // qb
    lane_lo = lax.broadcasted_iota(jnp.int32, (1, V7X_LANES), 1) < HEAD_DIM

    def body(idx, carry):
        c = idx // nb
        q0 = pl.multiple_of((idx % nb) * qb, qb)
        ks = pl.multiple_of(jnp.clip(q0 - nside, 0, t_len - kw), nside)
        q = q_ref[0, c, pl.ds(q0, qb), :]
        k = k_ref[0, c, pl.ds(ks, kw), :]
        v = v_ref[0, c, pl.ds(ks, kw), :]
        zero = jnp.zeros_like(q)
        qs = jnp.concatenate([jnp.where(lane_lo, q, zero), jnp.where(lane_lo, zero, q)], axis=0)
        s = lax.dot_general(qs, k, (((1,), (1,)), ((), ())), preferred_element_type=F32)
        tq = lax.broadcasted_iota(jnp.int32, s.shape, 0) & (qb - 1)
        tk = lax.broadcasted_iota(jnp.int32, s.shape, 1)
        valid = jnp.abs(tk - tq + (ks - q0)) <= nside
        s = jnp.where(valid, s, MASK_VALUE)
        m = jnp.max(s, axis=-1, keepdims=True)
        p = jnp.exp(s - m)
        l = jnp.sum(p, axis=-1, keepdims=True)
        pv = jnp.dot(p.astype(BF16), v, preferred_element_type=F32)
        rows = pl.ds(q0 * r + c, qb, stride=r) if r > 1 else pl.ds(q0, qb)
        acc_ref[g, rows, :] = jnp.where(lane_lo, pv[:qb], pv[qb:])
        m_ref[g, rows, :] = jnp.where(lane_lo, m[:qb], m[qb:])
        l_ref[g, rows, :] = jnp.where(lane_lo, l[:qb], l[qb:])
        return carry

    lax.fori_loop(0, r * nb, body, 0)


def _attention_kernel(*refs, seq, chunk):
    qkv = refs[:3 * N_GROUPS]
    o_ref, acc_ref, m_ref, l_ref = refs[3 * N_GROUPS:]
    for g, (window, r) in enumerate(DILATED_GROUPS):
        _attn_group(qkv[3 * g], qkv[3 * g + 1], qkv[3 * g + 2], acc_ref, m_ref, l_ref, g, r, seq // r, window // (2 * r))

    def combine(i, carry):
        rows = pl.ds(pl.multiple_of(i * chunk, chunk), chunk)
        ms = [m_ref[g, rows, :] for g in range(N_GROUPS)]
        top = functools.reduce(jnp.maximum, ms)
        ws = [jnp.exp(m - top) for m in ms]
        num = sum(w * acc_ref[g, rows, :] for g, w in enumerate(ws))
        den = sum(w * l_ref[g, rows, :] for g, w in enumerate(ws))
        o_ref[0, rows, :] = (num / den).astype(o_ref.dtype)
        return carry

    lax.fori_loop(0, seq // chunk, combine, 0)


def _attention(qk3d, vu3d):
    b, seq, _ = qk3d.shape
    lanes_per_group = GROUP_WIDTH // V7X_LANES
    args, specs = [], []
    for g, (_, r) in enumerate(DILATED_GROUPS):
        for src, col0 in ((qk3d, 0), (qk3d, ATTN_WIDTH), (vu3d, 0)):
            col = col0 + g * GROUP_WIDTH
            if r == 1:
                args.append(src.reshape(b, 1, seq, src.shape[2]))
                specs.append(pl.BlockSpec((1, 1, seq, V7X_LANES),
                                          lambda i, hp, cb=col // V7X_LANES: (i, 0, 0, cb + hp)))
            else:
                sub = src[:, :, col:col + GROUP_WIDTH].reshape(b, seq // r, r, GROUP_WIDTH).transpose(0, 2, 1, 3)
                args.append(sub)
                specs.append(pl.BlockSpec((1, r, seq // r, V7X_LANES), lambda i, hp: (i, 0, 0, hp)))
    return pl.pallas_call(
        functools.partial(_attention_kernel, seq=seq, chunk=256),
        grid=(b, lanes_per_group),
        in_specs=specs,
        out_specs=pl.BlockSpec((1, seq, V7X_LANES), lambda i, hp: (i, 0, hp)),
        out_shape=jax.ShapeDtypeStruct((b, seq, GROUP_WIDTH), BF16),
        scratch_shapes=[pltpu.VMEM((N_GROUPS, seq, V7X_LANES), F32)] * 3,
        compiler_params=_params("parallel", "parallel"),
        name="attention",
    )(*args)


def _filter_kernel(feats_ref, t_ref, w1_ref, b1_ref, freq_ref, w2_ref, b2_ref, w3f_ref, w3b_ref, delta_ref, sgn_ref,
                   cosm_ref, sinm_ref, kr_ref, ki_ref, sum_ref, dif_ref):
    fb = pl.program_id(2)

    @pl.when(fb == 0)
    def _():
        dot = functools.partial(jnp.dot, precision=HIGHEST, preferred_element_type=F32)
        h = jnp.sin(freq_ref[0:1, :] * (dot(feats_ref[...], w1_ref[...]) + b1_ref[...]))
        h = jnp.sin(freq_ref[1:2, :] * (dot(h, w2_ref[...]) + b2_ref[...]))
        decay = jnp.exp(-t_ref[...] * jnp.abs(delta_ref[...]))
        fwd = dot(h, w3f_ref[...]) * decay
        bwd = dot(h, w3b_ref[...]) * decay
        row = lax.broadcasted_iota(jnp.int32, (fwd.shape[0], 1), 0)
        bwd = jnp.where(row == 0, 0.0, bwd)
        norm = jnp.sum(jnp.abs(fwd), axis=0, keepdims=True) + jnp.sum(jnp.abs(bwd), axis=0, keepdims=True)
        fwd = fwd / norm
        bwd = bwd / norm
        sum_ref[...] = fwd + bwd
        dif_ref[...] = fwd - bwd

    kr = jnp.dot(cosm_ref[...], sum_ref[...], precision=HIGHEST, preferred_element_type=F32)
    ki = jnp.dot(sinm_ref[...], dif_ref[...], precision=HIGHEST, preferred_element_type=F32)
    nyq = jnp.sum(sgn_ref[...] * sum_ref[...], axis=0, keepdims=True)
    row0 = jnp.logical_and(lax.broadcasted_iota(jnp.int32, (kr.shape[0], 1), 0) == 0, fb == 0)
    kr_ref[0] = jnp.where(row0, 0.5 * kr, kr)
    ki_ref[0] = jnp.where(row0, nyq, ki)


def _filter_spectrum(feats, t, w1, b1, freq, w2, b2, w3, deltas, sgn, cosm, sinm, *, tc=256, tf=512):
    length = feats.shape[0]
    hid = w2.shape[0]
    chans = deltas.shape[1]
    ncb = chans // tc
    const = lambda o, cb, fb: (0, 0)
    out = jax.ShapeDtypeStruct((HYENA_ORDER, length, chans), F32)
    return pl.pallas_call(
        _filter_kernel,
        grid=(HYENA_ORDER, ncb, length // tf),
        in_specs=[
            pl.BlockSpec(feats.shape, const),
            pl.BlockSpec((length, 1), const),
            pl.BlockSpec(w1.shape, const),
            pl.BlockSpec((1, hid), const),
            pl.BlockSpec((2, hid), const),
            pl.BlockSpec((hid, hid), const),
            pl.BlockSpec((1, hid), const),
            pl.BlockSpec((hid, tc), lambda o, cb, fb: (0, (2 * o) * ncb + cb)),
            pl.BlockSpec((hid, tc), lambda o, cb, fb: (0, (2 * o + 1) * ncb + cb)),
            pl.BlockSpec((1, tc), lambda o, cb, fb: (0, cb)),
            pl.BlockSpec((length, 1), const),
            pl.BlockSpec((tf, length), lambda o, cb, fb: (fb, 0)),
            pl.BlockSpec((tf, length), lambda o, cb, fb: (fb, 0)),
        ],
        out_specs=[pl.BlockSpec((1, tf, tc), lambda o, cb, fb: (o, fb, cb))] * 2,
        out_shape=[out, out],
        scratch_shapes=[pltpu.VMEM((length, tc), F32)] * 2,
        compiler_params=_params("parallel", "parallel", "arbitrary"),
        name="hyena_filter",
    )(feats, t, w1, b1.reshape(1, hid), freq, w2, b2.reshape(1, hid), w3, w3, deltas, sgn, cosm, sinm)


def _dwconv3_full(a, w_ref, b_ref):
    n = a.shape[0]
    row = lax.broadcasted_iota(jnp.int32, (n, 1), 0)
    prev = jnp.where(row == 0, 0.0, pltpu.roll(a, 1, 0))
    nxt = jnp.where(row == n - 1, 0.0, pltpu.roll(a, n - 1, 0))
    return prev * w_ref[0:1, :] + a * w_ref[1:2, :] + nxt * w_ref[2:3, :] + b_ref[...]


DFT_ROWS = 512


def _long_conv_kernel(*refs, z_from_u):
    if z_from_u:
        (uz_ref, cwz_ref, cbz_ref, ug_ref, cwg_ref, cbg_ref, kr_ref, ki_ref, skip_ref, sgn_ref, cosm_ref, sinm_ref,
         o_ref, yr_ref, yi_ref) = refs
        z = _dwconv3_full(uz_ref[0].astype(F32), cwz_ref, cbz_ref)
    else:
        z_ref, ug_ref, cwg_ref, cbg_ref, kr_ref, ki_ref, skip_ref, sgn_ref, cosm_ref, sinm_ref, o_ref, yr_ref, yi_ref = refs
        z = z_ref[0]
    gate = _dwconv3_full(ug_ref[0].astype(F32), cwg_ref, cbg_ref)
    n = z.shape[0]
    zb = z.astype(BF16)
    sgn = sgn_ref[...]
    xnyq = jnp.sum(sgn * zb.astype(F32), axis=0, keepdims=True)
    nyq = 0.5 * (xnyq * ki_ref[0, 0:1, :])
    skip = skip_ref[0]
    for c in range(n // DFT_ROWS):
        rows = slice(c * DFT_ROWS, (c + 1) * DFT_ROWS)
        xr = jnp.dot(cosm_ref[rows, :], zb, preferred_element_type=F32)
        xi = jnp.dot(sinm_ref[rows, :], zb, preferred_element_type=F32)
        kr, ki = kr_ref[0, rows, :], ki_ref[0, rows, :]
        yr_ref[rows, :] = (xr * kr - xi * ki).astype(BF16)
        yi_ref[rows, :] = (xr * ki + xi * kr).astype(BF16)
    for c in range(n // DFT_ROWS):
        rows = slice(c * DFT_ROWS, (c + 1) * DFT_ROWS)
        y = (jnp.dot(cosm_ref[rows, :], yr_ref[...], preferred_element_type=F32)
             + jnp.dot(sinm_ref[rows, :], yi_ref[...], preferred_element_type=F32))
        y = (y + sgn[rows, :] * nyq) * (1.0 / n)
        o_ref[0, rows, :] = (gate[rows, :] * (y + z[rows, :] * skip)).astype(o_ref.dtype)


def _long_conv(z_src, u3d, conv_w, conv_b, kr, ki, skip, sgn, cosm, sinm, *, order, z_part, gate_part, u_col0, out_dtype, tc=256):
    b, seq, _ = u3d.shape
    chans = kr.shape[2]
    ncb = chans // tc
    u_cb0 = u_col0 // tc

    def u_specs(part):
        return [
            pl.BlockSpec((1, seq, tc), lambda cb, i: (i, 0, u_cb0 + part * ncb + cb)),
            pl.BlockSpec((3, tc), lambda cb, i: (0, part * ncb + cb)),
            pl.BlockSpec((1, tc), lambda cb, i: (0, part * ncb + cb)),
        ]

    conv_b2 = conv_b.reshape(1, -1)
    if z_src is None:
        args, specs = [u3d, conv_w, conv_b2], u_specs(z_part)
    else:
        args, specs = [z_src], [pl.BlockSpec((1, seq, tc), lambda cb, i: (i, 0, cb))]
    args += [u3d, conv_w, conv_b2, kr, ki, skip.reshape(HYENA_ORDER, 1, chans), sgn, cosm, sinm]
    specs += u_specs(gate_part) + [
        pl.BlockSpec((1, seq, tc), lambda cb, i: (order, 0, cb)),
        pl.BlockSpec((1, seq, tc), lambda cb, i: (order, 0, cb)),
        pl.BlockSpec((1, 1, tc), lambda cb, i: (order, 0, cb)),
        pl.BlockSpec((seq, 1), lambda cb, i: (0, 0)),
        pl.BlockSpec((seq, seq), lambda cb, i: (0, 0), pipeline_mode=pl.Buffered(1)),
        pl.BlockSpec((seq, seq), lambda cb, i: (0, 0), pipeline_mode=pl.Buffered(1)),
    ]
    return pl.pallas_call(
        functools.partial(_long_conv_kernel, z_from_u=z_src is None),
        grid=(ncb, b),
        in_specs=specs,
        out_specs=pl.BlockSpec((1, seq, tc), lambda cb, i: (i, 0, cb)),
        out_shape=jax.ShapeDtypeStruct((b, seq, chans), out_dtype),
        scratch_shapes=[pltpu.VMEM((seq, tc), BF16)] * 2,
        compiler_params=_params("parallel", "parallel"),
        name=f"long_conv{order}",
    )(*args)


def _merge_kernel(x_ref, g_ref, oa_ref, oh_ref, wga_ref, wgh_ref, bga_ref, bgh_ref, wpa_ref, wph_ref, wo_ref, o_ref, h_ref):
    c = pl.program_id(1)

    @pl.when(c == 0)
    def _():
        x = x_ref[...]
        h_ref[...] = _rmsnorm(x, g_ref[...]).astype(BF16)
        o_ref[...] = x

    h = h_ref[...]
    gate_a = 1.0 / (1.0 + jnp.exp(-(jnp.dot(h, wga_ref[...], preferred_element_type=F32) + bga_ref[...])))
    gate_h = 1.0 / (1.0 + jnp.exp(-(jnp.dot(h, wgh_ref[...], preferred_element_type=F32) + bgh_ref[...])))
    pa = jnp.dot(oa_ref[...], wpa_ref[...], preferred_element_type=F32)
    ph = jnp.dot(oh_ref[...], wph_ref[...], preferred_element_type=F32)
    mixed = (gate_a * pa + gate_h * ph).astype(BF16)
    o_ref[...] += jnp.dot(mixed, wo_ref[...], preferred_element_type=F32)


def _merge(x2d, gain, o_attn, o_hy, w_gate, b_gate, w_pa, w_ph, w_out, *, tm=512, tc=512):
    rows, d = x2d.shape
    ncb = d // tc
    b_gate2 = b_gate.reshape(1, -1)
    return pl.pallas_call(
        _merge_kernel,
        grid=(rows // tm, ncb),
        in_specs=[
            pl.BlockSpec((tm, d), lambda i, c: (i, 0)),
            pl.BlockSpec((1, d), lambda i, c: (0, 0)),
            pl.BlockSpec((tm, o_attn.shape[1]), lambda i, c: (i, 0)),
            pl.BlockSpec((tm, o_hy.shape[1]), lambda i, c: (i, 0)),
            pl.BlockSpec((d, tc), lambda i, c: (0, c)),
            pl.BlockSpec((d, tc), lambda i, c: (0, ncb + c)),
            pl.BlockSpec((1, tc), lambda i, c: (0, c)),
            pl.BlockSpec((1, tc), lambda i, c: (0, ncb + c)),
            pl.BlockSpec((w_pa.shape[0], tc), lambda i, c: (0, c)),
            pl.BlockSpec((w_ph.shape[0], tc), lambda i, c: (0, c)),
            pl.BlockSpec((tc, d), lambda i, c: (c, 0)),
        ],
        out_specs=pl.BlockSpec((tm, d), lambda i, c: (i, 0)),
        out_shape=jax.ShapeDtypeStruct((rows, d), F32),
        scratch_shapes=[pltpu.VMEM((tm, d), BF16)],
        compiler_params=_params("parallel", "arbitrary"),
        name="merge",
    )(x2d, gain.reshape(1, d), o_attn, o_hy, w_gate, w_gate, b_gate2, b_gate2, w_pa, w_ph, w_out)


HALO = 16


def _conv_ffn_kernel(*refs, blocks_per_seq, final_norm):
    if final_norm:
        x_ref, xp_ref, xn_ref, g_ref, wa_ref, wb_ref, cw_ref, cb_ref, wd_ref, gf_ref, o_ref, h_ref = refs
    else:
        x_ref, xp_ref, xn_ref, g_ref, wa_ref, wb_ref, cw_ref, cb_ref, wd_ref, o_ref, h_ref = refs
    i, f = pl.program_id(0), pl.program_id(1)
    tm = x_ref.shape[0]

    @pl.when(f == 0)
    def _():
        x = x_ref[...]
        g = g_ref[...]
        h_ref[0:HALO, :] = _rmsnorm(xp_ref[...], g).astype(BF16)
        h_ref[HALO:HALO + tm, :] = _rmsnorm(x, g).astype(BF16)
        h_ref[HALO + tm:, :] = _rmsnorm(xn_ref[...], g).astype(BF16)
        o_ref[...] = x

    up_a = jnp.dot(h_ref[...], wa_ref[...], preferred_element_type=F32)
    up_b = jnp.dot(h_ref[HALO:HALO + tm, :], wb_ref[...], preferred_element_type=F32)
    n_ext = up_a.shape[0]
    row = lax.broadcasted_iota(jnp.int32, (tm, 1), 0)
    seq_pos = i % blocks_per_seq
    at_start = jnp.logical_and(row == 0, seq_pos == 0)
    at_end = jnp.logical_and(row == tm - 1, seq_pos == blocks_per_seq - 1)
    a_prev = jnp.where(at_start, 0.0, pltpu.roll(up_a, 1, 0)[HALO:HALO + tm])
    a_next = jnp.where(at_end, 0.0, pltpu.roll(up_a, n_ext - 1, 0)[HALO:HALO + tm])
    a = a_prev * cw_ref[0:1, :] + up_a[HALO:HALO + tm] * cw_ref[1:2, :] + a_next * cw_ref[2:3, :] + cb_ref[...]
    gelu = 0.5 * a * (1.0 + jnp.tanh(math.sqrt(2.0 / math.pi) * (a + 0.044715 * (a * a * a))))
    o_ref[...] += jnp.dot((gelu * up_b).astype(BF16), wd_ref[...], preferred_element_type=F32)

    if final_norm:
        @pl.when(f == pl.num_programs(1) - 1)
        def _():
            o_ref[...] = _rmsnorm(o_ref[...], gf_ref[...])


def _conv_ffn(x2d, gain, w_up, conv_w, conv_b, w_down, final_gain, *, seq, tm=512, tf=512):
    rows, d = x2d.shape
    d_ff = w_down.shape[0]
    nfb = d_ff // tf
    halo_per_block = tm // HALO
    n_halo_blocks = rows // HALO
    args = [x2d, x2d, x2d, gain.reshape(1, d), w_up, w_up, conv_w, conv_b.reshape(1, d_ff), w_down]
    specs = [
        pl.BlockSpec((tm, d), lambda i, f: (i, 0)),
        pl.BlockSpec((HALO, d), lambda i, f: (jnp.maximum(i * halo_per_block - 1, 0), 0)),
        pl.BlockSpec((HALO, d), lambda i, f: (jnp.minimum((i + 1) * halo_per_block, n_halo_blocks - 1), 0)),
        pl.BlockSpec((1, d), lambda i, f: (0, 0)),
        pl.BlockSpec((d, tf), lambda i, f: (0, f)),
        pl.BlockSpec((d, tf), lambda i, f: (0, nfb + f)),
        pl.BlockSpec((3, tf), lambda i, f: (0, f)),
        pl.BlockSpec((1, tf), lambda i, f: (0, f)),
        pl.BlockSpec((tf, d), lambda i, f: (f, 0)),
    ]
    if final_gain is not None:
        args.append(final_gain.reshape(1, d))
        specs.append(pl.BlockSpec((1, d), lambda i, f: (0, 0)))
    return pl.pallas_call(
        functools.partial(_conv_ffn_kernel, blocks_per_seq=seq // tm, final_norm=final_gain is not None),
        grid=(rows // tm, nfb),
        in_specs=specs,
        out_specs=pl.BlockSpec((tm, d), lambda i, f: (i, 0)),
        out_shape=jax.ShapeDtypeStruct((rows, d), F32),
        scratch_shapes=[pltpu.VMEM((tm + 2 * HALO, d), BF16)],
        compiler_params=_params("parallel", "arbitrary"),
        name="conv_ffn",
    )(*args)


def _rope_tables(seq):
    pos = jnp.arange(seq, dtype=F32)
    inv = 1.0 / (ROPE_THETA ** (jnp.arange(0, HEAD_DIM, 2, dtype=F32) / HEAD_DIM))
    ang = pos[:, None] * inv[None, :]
    ang = jnp.concatenate([ang, ang] * (V7X_LANES // HEAD_DIM), axis=-1)
    cos, sin = jnp.cos(ang), jnp.sin(ang)
    first_half = (jnp.arange(V7X_LANES) % HEAD_DIM) < HEAD_DIM // 2
    return cos, jnp.where(first_half, -sin, 0.0), jnp.where(first_half, 0.0, sin)


def _hyena_tables(length, chans):
    t = jnp.linspace(0.0, 1.0, length, dtype=F32)[:, None]
    bands = (HYENA_EMB_DIM - 1) // 2
    w = 2.0 * math.pi * jnp.arange(length, dtype=F32)[:, None] / length
    f = jnp.linspace(1e-4, bands - 1, bands, dtype=F32)[None, :]
    feats = jnp.concatenate([t, jnp.cos(f * w), -jnp.sin(f * w)], axis=-1)
    feats = jnp.pad(feats, ((0, 0), (0, V7X_LANES - HYENA_EMB_DIM)))
    max_decay = math.log(HYENA_DECAY_TARGET) / HYENA_FAST_DECAY
    min_decay = math.log(HYENA_DECAY_TARGET) / HYENA_SLOW_DECAY
    deltas = jnp.linspace(min_decay, max_decay, chans, dtype=F32)[None, :]
    idx = jnp.arange(length, dtype=jnp.int32)
    ang = ((idx[:, None] * idx[None, :]) % (2 * length)).astype(F32) * (math.pi / length)
    sgn = (1 - 2 * (idx % 2)).astype(F32)[:, None]
    return feats, t, deltas, sgn, jnp.cos(ang), -jnp.sin(ang)


def kernel(x, attn_norm, w_in, hy_conv_w, hy_conv_b, f_w1, f_b1, f_freq, f_w2, f_b2, f_w3, hy_skip, w_proj_attn, w_proj_hyena, w_gate, b_gate, w_out, ffn_norm, w_up, ffn_conv_w, ffn_conv_b, w_down, final_norm):
    b, seq, d = x.shape
    depth = w_in.shape[0]
    chans = hy_skip.shape[2]
    cos, sin_lo, sin_hi = _rope_tables(seq)
    feats, t, deltas, sgn, cosm, sinm = _hyena_tables(seq, chans)
    cosm16, sinm16 = cosm.astype(BF16), sinm.astype(BF16)
    w1_pad = jnp.pad(f_w1, ((0, 0), (0, V7X_LANES - HYENA_EMB_DIM), (0, 0)))
    w_in16, w_gate16, w_pa16, w_ph16, w_out16, w_up16, w_down16 = (
        w.astype(BF16) for w in (w_in, w_gate, w_proj_attn, w_proj_hyena, w_out, w_up, w_down))

    x2d = x.reshape(b * seq, d)
    for l in range(depth):
        qk = _norm_proj(x2d, attn_norm[l], w_in16[l], 0, 2 * ATTN_WIDTH, (cos, sin_lo, sin_hi), seq=seq)
        vu = _norm_proj(x2d, attn_norm[l], w_in16[l], 2 * ATTN_WIDTH, w_in.shape[2] - 2 * ATTN_WIDTH, None, seq=seq)
        vu3d = vu.reshape(b, seq, -1)
        o_attn = _attention(qk.reshape(b, seq, -1), vu3d)
        kr, ki = _filter_spectrum(feats, t, w1_pad[l], f_b1[l], f_freq[l], f_w2[l], f_b2[l], f_w3[l], deltas, sgn, cosm, sinm)
        conv = functools.partial(_long_conv, u3d=vu3d, conv_w=hy_conv_w[l], conv_b=hy_conv_b[l], kr=kr, ki=ki,
                                 skip=hy_skip[l], sgn=sgn, cosm=cosm16, sinm=sinm16, u_col0=ATTN_WIDTH)
        z1 = conv(None, order=0, z_part=0, gate_part=1, out_dtype=F32)
        o_hy = conv(z1, order=1, z_part=None, gate_part=2, out_dtype=BF16)
        x2d = _merge(x2d, attn_norm[l], o_attn.reshape(b * seq, -1), o_hy.reshape(b * seq, -1),
                     w_gate16[l], b_gate[l], w_pa16[l], w_ph16[l], w_out16[l])
        x2d = _conv_ffn(x2d, ffn_norm[l], w_up16[l], ffn_conv_w[l], ffn_conv_b[l], w_down16[l],
                        final_norm if l == depth - 1 else None, seq=seq)
    return x2d.reshape(b, seq, d)
```

```python
import functools
import math

import jax
import jax.numpy as jnp
from jax import lax
from jax.experimental import pallas as pl
from jax.experimental.pallas import tpu as pltpu

HEAD_DIM = 64
HEADS_PER_GROUP = 8
DILATED_GROUPS = ((128, 1), (512, 4), (2048, 16))
N_GROUPS = len(DILATED_GROUPS)
GROUP_WIDTH = HEADS_PER_GROUP * HEAD_DIM
ATTN_WIDTH = N_GROUPS * GROUP_WIDTH
ROPE_THETA = 10000.0
HYENA_ORDER = 2
HYENA_EMB_DIM = 33
HYENA_DECAY_TARGET = 1e-2
HYENA_FAST_DECAY = 0.3
HYENA_SLOW_DECAY = 1.5
RMS_EPS = 1e-6
MASK_VALUE = -1e30

V7X_LANES = 128
V7X_VMEM_LIMIT_BYTES = 60 * 1024 * 1024

F32 = jnp.float32
BF16 = jnp.bfloat16
HIGHEST = lax.Precision.HIGHEST


def _params(*semantics):
    return pltpu.CompilerParams(dimension_semantics=semantics, vmem_limit_bytes=V7X_VMEM_LIMIT_BYTES)


def _rmsnorm(x, g):
    return x * lax.rsqrt(jnp.mean(x * x, axis=-1, keepdims=True) + RMS_EPS) * g


def _norm_proj_kernel(x_ref, g_ref, w_ref, cos_ref, sin_ref, o_ref, h_ref, *, n_q_blocks, n_rope_blocks):
    j = pl.program_id(1)

    @pl.when(j == 0)
    def _():
        h_ref[...] = _rmsnorm(x_ref[...], g_ref[...]).astype(BF16)

    acc = jnp.dot(h_ref[...], w_ref[...], preferred_element_type=F32)
    scale = jnp.where(j < n_q_blocks, 1.0 / math.sqrt(HEAD_DIM), 1.0)
    is_rope = j < n_rope_blocks
    cos = jnp.where(is_rope, cos_ref[...] * scale, 1.0)
    sin = jnp.where(is_rope, sin_ref[...] * scale, 0.0)
    half = acc.shape[1] // 2
    for s in range(half // V7X_LANES):
        lo_cols = slice(s * V7X_LANES, (s + 1) * V7X_LANES)
        hi_cols = slice(half + s * V7X_LANES, half + (s + 1) * V7X_LANES)
        lo, hi = acc[:, lo_cols], acc[:, hi_cols]
        o_ref[:, lo_cols] = (lo * cos - hi * sin).astype(o_ref.dtype)
        o_ref[:, hi_cols] = (hi * cos + lo * sin).astype(o_ref.dtype)


def _norm_proj(x2d, gain, w, layer, cos, sin, *, seq, tm=1024, tn=GROUP_WIDTH):
    rows, d = x2d.shape
    n = w.shape[2]
    table_spec = pl.BlockSpec((tm, V7X_LANES), lambda i, j: (i % (seq // tm), 0))
    return pl.pallas_call(
        functools.partial(_norm_proj_kernel, n_q_blocks=ATTN_WIDTH // tn, n_rope_blocks=2 * ATTN_WIDTH // tn),
        grid=(rows // tm, n // tn),
        in_specs=[
            pl.BlockSpec((tm, d), lambda i, j: (i, 0)),
            pl.BlockSpec((1, d), lambda i, j: (0, 0)),
            pl.BlockSpec((None, d, tn), lambda i, j: (layer, 0, j)),
            table_spec,
            table_spec,
        ],
        out_specs=pl.BlockSpec((tm, tn), lambda i, j: (i, j)),
        out_shape=jax.ShapeDtypeStruct((rows, n), BF16),
        scratch_shapes=[pltpu.VMEM((tm, d), BF16)],
        compiler_params=_params("parallel", "arbitrary"),
        name="norm_proj",
    )(x2d, gain.reshape(1, d), w, cos, sin)


HEADS_PER_STEP = 4
HALF_DIM = HEAD_DIM // 2
QUAD_WIDTH = HEADS_PER_STEP * HEAD_DIM


def _attn_group(q_lo_ref, q_hi_ref, k_lo_ref, k_hi_ref, v_ref, bias_ref, acc_ref, m_ref, lsw_ref, g, r, t_len, nside):
    qb = 2 * nside
    kw = min(t_len, qb + 2 * nside)
    nb = t_len // qb
    lane = lax.broadcasted_iota(jnp.int32, (1, V7X_LANES), 1)
    lane_lo = lane < HEAD_DIM
    head_lanes = [jnp.logical_and(lane >= h * HALF_DIM, lane < (h + 1) * HALF_DIM) for h in range(HEADS_PER_STEP)]

    def body(idx, carry):
        c = idx // nb
        q0 = pl.multiple_of((idx % nb) * qb, qb)
        ks = pl.multiple_of(jnp.clip(q0 - nside, 0, t_len - kw), nside)
        q_lo, q_hi = q_lo_ref[0, c, pl.ds(q0, qb), :], q_hi_ref[0, c, pl.ds(q0, qb), :]
        k2 = jnp.concatenate([k_lo_ref[0, c, pl.ds(ks, kw), :], k_hi_ref[0, c, pl.ds(ks, kw), :]], axis=1)
        zero = jnp.zeros_like(q_lo)
        qs = jnp.concatenate([jnp.concatenate([jnp.where(hl, q_lo, zero), jnp.where(hl, q_hi, zero)], axis=1)
                              for hl in head_lanes], axis=0)
        s = lax.dot_general(qs, k2, (((1,), (1,)), ((), ())), preferred_element_type=F32)
        bias = bias_ref[(q0 - ks) // nside, :, :kw]
        s = (s.reshape(HEADS_PER_STEP, qb, kw) + bias[None]).reshape(HEADS_PER_STEP * qb, kw)
        m = jnp.max(s, axis=-1, keepdims=True)
        p = jnp.exp(s - m).astype(BF16)
        rows = pl.ds(q0 * r + c, qb, stride=r) if r > 1 else pl.ds(q0, qb)
        for pair in range(HEADS_PER_STEP // 2):
            cols = slice(pair * V7X_LANES, (pair + 1) * V7X_LANES)
            v = v_ref[0, c, pl.ds(ks, kw), cols]
            one = jnp.ones_like(v)
            pv_a = jnp.dot(p[(2 * pair) * qb:(2 * pair + 1) * qb], jnp.where(lane_lo, v, one), preferred_element_type=F32)
            pv_b = jnp.dot(p[(2 * pair + 1) * qb:(2 * pair + 2) * qb], jnp.where(lane_lo, one, v), preferred_element_type=F32)
            slot = g * (HEADS_PER_STEP // 2) + pair
            acc_ref[slot, rows, :] = jnp.where(lane_lo, pv_a, pv_b)
            lsw_ref[slot, rows, :] = jnp.where(lane_lo, pv_b, pv_a)
            m_ref[slot, rows, :] = jnp.where(lane_lo, m[(2 * pair) * qb:(2 * pair + 1) * qb],
                                             m[(2 * pair + 1) * qb:(2 * pair + 2) * qb])
        return carry

    lax.fori_loop(0, r * nb, body, 0, unroll=8)


def _attention_kernel(*refs, seq, chunk):
    qkv = refs[:5 * N_GROUPS]
    bias_ref, o_ref, acc_ref, m_ref, lsw_ref = refs[5 * N_GROUPS:]
    for g, (window, r) in enumerate(DILATED_GROUPS):
        _attn_group(*qkv[5 * g:5 * g + 5], bias_ref, acc_ref, m_ref, lsw_ref, g, r, seq // r, window // (2 * r))

    def combine(i, carry):
        rows = pl.ds(pl.multiple_of(i * chunk, chunk), chunk)
        for pair in range(HEADS_PER_STEP // 2):
            cols = slice(pair * V7X_LANES, (pair + 1) * V7X_LANES)
            slots = [g * (HEADS_PER_STEP // 2) + pair for g in range(N_GROUPS)]
            ms = [m_ref[slot, rows, :] for slot in slots]
            top = functools.reduce(jnp.maximum, ms)
            ws = [jnp.exp(m - top) for m in ms]
            num = sum(w * acc_ref[slot, rows, :] for slot, w in zip(slots, ws))
            den = sum(w * pltpu.roll(lsw_ref[slot, rows, :], HEAD_DIM, 1) for slot, w in zip(slots, ws))
            o_ref[0, rows, cols] = (num / den).astype(o_ref.dtype)
        return carry

    lax.fori_loop(0, seq // chunk, combine, 0)


def _band_bias(nside):
    u = jnp.arange(2 * nside)[None, :, None]
    j = jnp.arange(4 * nside)[None, None, :]
    off = (jnp.arange(3) * nside)[:, None, None]
    return jnp.where(jnp.abs(j - u - off) <= nside, 0.0, MASK_VALUE).astype(F32)


def _attention(qkv3d):
    b, seq, width = qkv3d.shape
    n_quads = GROUP_WIDTH // QUAD_WIDTH
    tiles_per_half = GROUP_WIDTH // (2 * V7X_LANES)
    nside = DILATED_GROUPS[0][0] // (2 * DILATED_GROUPS[0][1])
    assert all(w // (2 * r) == nside for w, r in DILATED_GROUPS)
    args, specs = [], []
    for g, (_, r) in enumerate(DILATED_GROUPS):
        for part in range(3):
            col = part * ATTN_WIDTH + g * GROUP_WIDTH
            if r == 1:
                src, t0 = qkv3d.reshape(b, 1, seq, width), col // V7X_LANES
            else:
                src = qkv3d[:, :, col:col + GROUP_WIDTH].reshape(b, seq // r, r, GROUP_WIDTH).transpose(0, 2, 1, 3)
                t0 = 0
            if part < 2:
                for half in range(2):
                    args.append(src)
                    specs.append(pl.BlockSpec((1, r, seq // r, V7X_LANES),
                                              lambda i, hq, t=t0 + half * tiles_per_half: (i, 0, 0, t + hq)))
            else:
                args.append(src)
                specs.append(pl.BlockSpec((1, r, seq // r, QUAD_WIDTH),
                                          lambda i, hq, t=t0 * V7X_LANES // QUAD_WIDTH: (i, 0, 0, t + hq)))
    args.append(_band_bias(nside))
    specs.append(pl.BlockSpec((3, 2 * nside, 4 * nside), lambda i, hq: (0, 0, 0)))
    return pl.pallas_call(
        functools.partial(_attention_kernel, seq=seq, chunk=256),
        grid=(b, n_quads),
        in_specs=specs,
        out_specs=pl.BlockSpec((1, seq, QUAD_WIDTH), lambda i, hq: (i, 0, hq)),
        out_shape=jax.ShapeDtypeStruct((b, seq, GROUP_WIDTH), BF16),
        scratch_shapes=[pltpu.VMEM((N_GROUPS * HEADS_PER_STEP // 2, seq, V7X_LANES), F32)] * 3,
        compiler_params=_params("parallel", "parallel"),
        name="attention",
    )(*args)


def _filter_kernel(feats_ref, t_ref, w1_ref, b1_ref, freq_ref, w2_ref, b2_ref, w3f_ref, w3b_ref, delta_ref, sgn_ref,
                   cosm_ref, sinm_ref, kr_ref, ki_ref, sum_ref, dif_ref):
    fb = pl.program_id(2)

    @pl.when(fb == 0)
    def _():
        dot = functools.partial(jnp.dot, precision=HIGHEST, preferred_element_type=F32)
        h = jnp.sin(freq_ref[0:1, :] * (dot(feats_ref[...], w1_ref[...]) + b1_ref[...]))
        h = jnp.sin(freq_ref[1:2, :] * (dot(h, w2_ref[...]) + b2_ref[...]))
        decay = jnp.exp(-t_ref[...] * jnp.abs(delta_ref[...]))
        fwd = dot(h, w3f_ref[...]) * decay
        bwd = dot(h, w3b_ref[...]) * decay
        row = lax.broadcasted_iota(jnp.int32, (fwd.shape[0], 1), 0)
        bwd = jnp.where(row == 0, 0.0, bwd)
        norm = jnp.sum(jnp.abs(fwd), axis=0, keepdims=True) + jnp.sum(jnp.abs(bwd), axis=0, keepdims=True)
        fwd = fwd / norm
        bwd = bwd / norm
        sum_ref[...] = fwd + bwd
        dif_ref[...] = fwd - bwd

    kr = jnp.dot(cosm_ref[...], sum_ref[...], precision=HIGHEST, preferred_element_type=F32)
    ki = jnp.dot(sinm_ref[...], dif_ref[...], precision=HIGHEST, preferred_element_type=F32)
    nyq = jnp.sum(sgn_ref[...] * sum_ref[...], axis=0, keepdims=True)
    row0 = jnp.logical_and(lax.broadcasted_iota(jnp.int32, (kr.shape[0], 1), 0) == 0, fb == 0)
    kr_ref[0] = jnp.where(row0, 0.5 * kr, kr)
    ki_ref[0] = jnp.where(row0, nyq, ki)


def _filter_spectrum(feats, t, w1, b1, freq, w2, b2, w3, deltas, sgn, cosm, sinm, *, tc=256, tf=512):
    length = feats.shape[0]
    hid = w2.shape[0]
    chans = deltas.shape[1]
    ncb = chans // tc
    const = lambda o, cb, fb: (0, 0)
    out = jax.ShapeDtypeStruct((HYENA_ORDER, length, chans), F32)
    return pl.pallas_call(
        _filter_kernel,
        grid=(HYENA_ORDER, ncb, length // tf),
        in_specs=[
            pl.BlockSpec(feats.shape, const),
            pl.BlockSpec((length, 1), const),
            pl.BlockSpec(w1.shape, const),
            pl.BlockSpec((1, hid), const),
            pl.BlockSpec((2, hid), const),
            pl.BlockSpec((hid, hid), const),
            pl.BlockSpec((1, hid), const),
            pl.BlockSpec((hid, tc), lambda o, cb, fb: (0, (2 * o) * ncb + cb)),
            pl.BlockSpec((hid, tc), lambda o, cb, fb: (0, (2 * o + 1) * ncb + cb)),
            pl.BlockSpec((1, tc), lambda o, cb, fb: (0, cb)),
            pl.BlockSpec((length, 1), const),
            pl.BlockSpec((tf, length), lambda o, cb, fb: (fb, 0)),
            pl.BlockSpec((tf, length), lambda o, cb, fb: (fb, 0)),
        ],
        out_specs=[pl.BlockSpec((1, tf, tc), lambda o, cb, fb: (o, fb, cb))] * 2,
        out_shape=[out, out],
        scratch_shapes=[pltpu.VMEM((length, tc), F32)] * 2,
        compiler_params=_params("parallel", "parallel", "arbitrary"),
        name="hyena_filter",
    )(feats, t, w1, b1.reshape(1, hid), freq, w2, b2.reshape(1, hid), w3, w3, deltas, sgn, cosm, sinm)


def _dwconv3_full(a, w_ref, b_ref):
    n = a.shape[0]
    row = lax.broadcasted_iota(jnp.int32, (n, 1), 0)
    prev = jnp.where(row == 0, 0.0, pltpu.roll(a, 1, 0))
    nxt = jnp.where(row == n - 1, 0.0, pltpu.roll(a, n - 1, 0))
    return prev * w_ref[0:1, :] + a * w_ref[1:2, :] + nxt * w_ref[2:3, :] + b_ref[...]


DFT_ROWS = 512


def _long_conv_kernel(*refs, z_from_u):
    if z_from_u:
        (uz_ref, cwz_ref, cbz_ref, ug_ref, cwg_ref, cbg_ref, kr_ref, ki_ref, skip_ref, sgn_ref, cosm_ref, sinm_ref,
         o_ref, yr_ref, yi_ref) = refs
        z = _dwconv3_full(uz_ref[0].astype(F32), cwz_ref, cbz_ref)
    else:
        z_ref, ug_ref, cwg_ref, cbg_ref, kr_ref, ki_ref, skip_ref, sgn_ref, cosm_ref, sinm_ref, o_ref, yr_ref, yi_ref = refs
        z = z_ref[0]
    gate = _dwconv3_full(ug_ref[0].astype(F32), cwg_ref, cbg_ref)
    n = z.shape[0]
    zb = z.astype(BF16)
    sgn = sgn_ref[...]
    xnyq = jnp.sum(sgn * zb.astype(F32), axis=0, keepdims=True)
    nyq = 0.5 * (xnyq * ki_ref[0, 0:1, :])
    skip = skip_ref[0]
    for c in range(n // DFT_ROWS):
        rows = slice(c * DFT_ROWS, (c + 1) * DFT_ROWS)
        xr = jnp.dot(cosm_ref[rows, :], zb, preferred_element_type=F32)
        xi = jnp.dot(sinm_ref[rows, :], zb, preferred_element_type=F32)
        kr, ki = kr_ref[0, rows, :], ki_ref[0, rows, :]
        yr_ref[rows, :] = (xr * kr - xi * ki).astype(BF16)
        yi_ref[rows, :] = (xr * ki + xi * kr).astype(BF16)
    for c in range(n // DFT_ROWS):
        rows = slice(c * DFT_ROWS, (c + 1) * DFT_ROWS)
        y = (jnp.dot(cosm_ref[rows, :], yr_ref[...], preferred_element_type=F32)
             + jnp.dot(sinm_ref[rows, :], yi_ref[...], preferred_element_type=F32))
        y = (y + sgn[rows, :] * nyq) * (1.0 / n)
        o_ref[0, rows, :] = (gate[rows, :] * (y + z[rows, :] * skip)).astype(o_ref.dtype)


def _long_conv(z_src, u3d, conv_w, conv_b, kr, ki, skip, sgn, cosm, sinm, *, order, z_part, gate_part, u_col0, out_dtype, tc=256):
    b, seq, _ = u3d.shape
    chans = kr.shape[2]
    ncb = chans // tc
    u_cb0 = u_col0 // tc

    def u_specs(part):
        return [
            pl.BlockSpec((1, seq, tc), lambda cb, i: (i, 0, u_cb0 + part * ncb + cb)),
            pl.BlockSpec((3, tc), lambda cb, i: (0, part * ncb + cb)),
            pl.BlockSpec((1, tc), lambda cb, i: (0, part * ncb + cb)),
        ]

    conv_b2 = conv_b.reshape(1, -1)
    if z_src is None:
        args, specs = [u3d, conv_w, conv_b2], u_specs(z_part)
    else:
        args, specs = [z_src], [pl.BlockSpec((1, seq, tc), lambda cb, i: (i, 0, cb))]
    args += [u3d, conv_w, conv_b2, kr, ki, skip.reshape(HYENA_ORDER, 1, chans), sgn, cosm, sinm]
    specs += u_specs(gate_part) + [
        pl.BlockSpec((1, seq, tc), lambda cb, i: (order, 0, cb)),
        pl.BlockSpec((1, seq, tc), lambda cb, i: (order, 0, cb)),
        pl.BlockSpec((1, 1, tc), lambda cb, i: (order, 0, cb)),
        pl.BlockSpec((seq, 1), lambda cb, i: (0, 0)),
        pl.BlockSpec((seq, seq), lambda cb, i: (0, 0), pipeline_mode=pl.Buffered(1)),
        pl.BlockSpec((seq, seq), lambda cb, i: (0, 0), pipeline_mode=pl.Buffered(1)),
    ]
    return pl.pallas_call(
        functools.partial(_long_conv_kernel, z_from_u=z_src is None),
        grid=(ncb, b),
        in_specs=specs,
        out_specs=pl.BlockSpec((1, seq, tc), lambda cb, i: (i, 0, cb)),
        out_shape=jax.ShapeDtypeStruct((b, seq, chans), out_dtype),
        scratch_shapes=[pltpu.VMEM((seq, tc), BF16)] * 2,
        compiler_params=_params("parallel", "parallel"),
        name=f"long_conv{order}",
    )(*args)


def _merge_kernel(x_ref, g_ref, oa_ref, oh_ref, wga_ref, wgh_ref, bga_ref, bgh_ref, wpa_ref, wph_ref, wo_ref, o_ref, h_ref):
    c = pl.program_id(1)

    @pl.when(c == 0)
    def _():
        x = x_ref[...]
        h_ref[...] = _rmsnorm(x, g_ref[...]).astype(BF16)
        o_ref[...] = x

    h = h_ref[...]
    gate_a = 1.0 / (1.0 + jnp.exp(-(jnp.dot(h, wga_ref[...], preferred_element_type=F32) + bga_ref[...])))
    gate_h = 1.0 / (1.0 + jnp.exp(-(jnp.dot(h, wgh_ref[...], preferred_element_type=F32) + bgh_ref[...])))
    pa = jnp.dot(oa_ref[...], wpa_ref[...], preferred_element_type=F32)
    ph = jnp.dot(oh_ref[...], wph_ref[...], preferred_element_type=F32)
    mixed = (gate_a * pa + gate_h * ph).astype(BF16)
    o_ref[...] += jnp.dot(mixed, wo_ref[...], preferred_element_type=F32)


def _merge(x2d, gain, o_attn, o_hy, w_gate, b_gate, w_pa, w_ph, w_out, layer, *, tm=512, tc=512):
    rows, d = x2d.shape
    ncb = d // tc
    b_gate2 = b_gate.reshape(1, -1)
    return pl.pallas_call(
        _merge_kernel,
        grid=(rows // tm, ncb),
        in_specs=[
            pl.BlockSpec((tm, d), lambda i, c: (i, 0)),
            pl.BlockSpec((1, d), lambda i, c: (0, 0)),
            pl.BlockSpec((tm, o_attn.shape[1]), lambda i, c: (i, 0)),
            pl.BlockSpec((tm, o_hy.shape[1]), lambda i, c: (i, 0)),
            pl.BlockSpec((None, d, tc), lambda i, c: (layer, 0, c)),
            pl.BlockSpec((None, d, tc), lambda i, c: (layer, 0, ncb + c)),
            pl.BlockSpec((1, tc), lambda i, c: (0, c)),
            pl.BlockSpec((1, tc), lambda i, c: (0, ncb + c)),
            pl.BlockSpec((None, w_pa.shape[1], tc), lambda i, c: (layer, 0, c)),
            pl.BlockSpec((None, w_ph.shape[1], tc), lambda i, c: (layer, 0, c)),
            pl.BlockSpec((None, tc, d), lambda i, c: (layer, c, 0)),
        ],
        out_specs=pl.BlockSpec((tm, d), lambda i, c: (i, 0)),
        out_shape=jax.ShapeDtypeStruct((rows, d), F32),
        scratch_shapes=[pltpu.VMEM((tm, d), BF16)],
        compiler_params=_params("parallel", "arbitrary"),
        name="merge",
    )(x2d, gain.reshape(1, d), o_attn, o_hy, w_gate, w_gate, b_gate2, b_gate2, w_pa, w_ph, w_out)


HALO = 16


def _conv_ffn_kernel(*refs, blocks_per_seq, final_norm):
    if final_norm:
        x_ref, xp_ref, xn_ref, g_ref, wa_ref, wb_ref, cw_ref, cb_ref, wd_ref, gf_ref, o_ref, h_ref = refs
    else:
        x_ref, xp_ref, xn_ref, g_ref, wa_ref, wb_ref, cw_ref, cb_ref, wd_ref, o_ref, h_ref = refs
    i, f = pl.program_id(0), pl.program_id(1)
    tm = x_ref.shape[0]

    @pl.when(f == 0)
    def _():
        x = x_ref[...]
        g = g_ref[...]
        h_ref[0:HALO, :] = _rmsnorm(xp_ref[...], g).astype(BF16)
        h_ref[HALO:HALO + tm, :] = _rmsnorm(x, g).astype(BF16)
        h_ref[HALO + tm:, :] = _rmsnorm(xn_ref[...], g).astype(BF16)
        o_ref[...] = x

    up_a = jnp.dot(h_ref[...], wa_ref[...], preferred_element_type=F32)
    up_b = jnp.dot(h_ref[HALO:HALO + tm, :], wb_ref[...], preferred_element_type=F32)
    n_ext = up_a.shape[0]
    row = lax.broadcasted_iota(jnp.int32, (tm, 1), 0)
    seq_pos = i % blocks_per_seq
    at_start = jnp.logical_and(row == 0, seq_pos == 0)
    at_end = jnp.logical_and(row == tm - 1, seq_pos == blocks_per_seq - 1)
    a_prev = jnp.where(at_start, 0.0, pltpu.roll(up_a, 1, 0)[HALO:HALO + tm])
    a_next = jnp.where(at_end, 0.0, pltpu.roll(up_a, n_ext - 1, 0)[HALO:HALO + tm])
    a = a_prev * cw_ref[0:1, :] + up_a[HALO:HALO + tm] * cw_ref[1:2, :] + a_next * cw_ref[2:3, :] + cb_ref[...]
    gelu = 0.5 * a * (1.0 + jnp.tanh(math.sqrt(2.0 / math.pi) * (a + 0.044715 * (a * a * a))))
    o_ref[...] += jnp.dot((gelu * up_b).astype(BF16), wd_ref[...], preferred_element_type=F32)

    if final_norm:
        @pl.when(f == pl.num_programs(1) - 1)
        def _():
            o_ref[...] = _rmsnorm(o_ref[...], gf_ref[...])


def _conv_ffn(x2d, gain, w_up, conv_w, conv_b, w_down, layer, final_gain, *, seq, tm=1024, tf=512):
    rows, d = x2d.shape
    d_ff = w_down.shape[1]
    nfb = d_ff // tf
    halo_per_block = tm // HALO
    n_halo_blocks = rows // HALO
    args = [x2d, x2d, x2d, gain.reshape(1, d), w_up, w_up, conv_w, conv_b.reshape(1, d_ff), w_down]
    specs = [
        pl.BlockSpec((tm, d), lambda i, f: (i, 0)),
        pl.BlockSpec((HALO, d), lambda i, f: (jnp.maximum(i * halo_per_block - 1, 0), 0)),
        pl.BlockSpec((HALO, d), lambda i, f: (jnp.minimum((i + 1) * halo_per_block, n_halo_blocks - 1), 0)),
        pl.BlockSpec((1, d), lambda i, f: (0, 0)),
        pl.BlockSpec((None, d, tf), lambda i, f: (layer, 0, f)),
        pl.BlockSpec((None, d, tf), lambda i, f: (layer, 0, nfb + f)),
        pl.BlockSpec((3, tf), lambda i, f: (0, f)),
        pl.BlockSpec((1, tf), lambda i, f: (0, f)),
        pl.BlockSpec((None, tf, d), lambda i, f: (layer, f, 0)),
    ]
    if final_gain is not None:
        args.append(final_gain.reshape(1, d))
        specs.append(pl.BlockSpec((1, d), lambda i, f: (0, 0)))
    return pl.pallas_call(
        functools.partial(_conv_ffn_kernel, blocks_per_seq=seq // tm, final_norm=final_gain is not None),
        grid=(rows // tm, nfb),
        in_specs=specs,
        out_specs=pl.BlockSpec((tm, d), lambda i, f: (i, 0)),
        out_shape=jax.ShapeDtypeStruct((rows, d), F32),
        scratch_shapes=[pltpu.VMEM((tm + 2 * HALO, d), BF16)],
        compiler_params=_params("parallel", "arbitrary"),
        name="conv_ffn",
    )(*args)


def _rope_tables(seq):
    pos = jnp.arange(seq, dtype=F32)
    inv = 1.0 / (ROPE_THETA ** (jnp.arange(0, HEAD_DIM, 2, dtype=F32) / HEAD_DIM))
    ang = pos[:, None] * inv[None, :]
    ang = jnp.concatenate([ang] * (V7X_LANES // HALF_DIM), axis=-1)
    return jnp.cos(ang), jnp.sin(ang)


def _split_rotary_halves(w_in):
    layers, d, _ = w_in.shape
    n_groups = 2 * ATTN_WIDTH // GROUP_WIDTH
    qk = w_in[:, :, :2 * ATTN_WIDTH].reshape(layers, d, n_groups, HEADS_PER_GROUP, 2, HALF_DIM)
    qk = qk.transpose(0, 1, 2, 4, 3, 5).reshape(layers, d, 2 * ATTN_WIDTH)
    return jnp.concatenate([qk, w_in[:, :, 2 * ATTN_WIDTH:]], axis=-1)


def _hyena_tables(length, chans):
    t = jnp.linspace(0.0, 1.0, length, dtype=F32)[:, None]
    bands = (HYENA_EMB_DIM - 1) // 2
    w = 2.0 * math.pi * jnp.arange(length, dtype=F32)[:, None] / length
    f = jnp.linspace(1e-4, bands - 1, bands, dtype=F32)[None, :]
    feats = jnp.concatenate([t, jnp.cos(f * w), -jnp.sin(f * w)], axis=-1)
    feats = jnp.pad(feats, ((0, 0), (0, V7X_LANES - HYENA_EMB_DIM)))
    max_decay = math.log(HYENA_DECAY_TARGET) / HYENA_FAST_DECAY
    min_decay = math.log(HYENA_DECAY_TARGET) / HYENA_SLOW_DECAY
    deltas = jnp.linspace(min_decay, max_decay, chans, dtype=F32)[None, :]
    idx = jnp.arange(length, dtype=jnp.int32)
    ang = ((idx[:, None] * idx[None, :]) % (2 * length)).astype(F32) * (math.pi / length)
    sgn = (1 - 2 * (idx % 2)).astype(F32)[:, None]
    return feats, t, deltas, sgn, jnp.cos(ang), -jnp.sin(ang)


def kernel(x, attn_norm, w_in, hy_conv_w, hy_conv_b, f_w1, f_b1, f_freq, f_w2, f_b2, f_w3, hy_skip, w_proj_attn, w_proj_hyena, w_gate, b_gate, w_out, ffn_norm, w_up, ffn_conv_w, ffn_conv_b, w_down, final_norm):
    b, seq, d = x.shape
    depth = w_in.shape[0]
    chans = hy_skip.shape[2]
    cos, sin = _rope_tables(seq)
    feats, t, deltas, sgn, cosm, sinm = _hyena_tables(seq, chans)
    cosm16, sinm16 = cosm.astype(BF16), sinm.astype(BF16)
    w1_pad = jnp.pad(f_w1, ((0, 0), (0, V7X_LANES - HYENA_EMB_DIM), (0, 0)))
    w_in16 = _split_rotary_halves(w_in).astype(BF16)
    w_gate16, w_pa16, w_ph16, w_out16, w_up16, w_down16 = (
        w.astype(BF16) for w in (w_gate, w_proj_attn, w_proj_hyena, w_out, w_up, w_down))

    x2d = x.reshape(b * seq, d)
    for l in range(depth):
        proj3d = _norm_proj(x2d, attn_norm[l], w_in16, l, cos, sin, seq=seq).reshape(b, seq, -1)
        o_attn = _attention(proj3d)
        kr, ki = _filter_spectrum(feats, t, w1_pad[l], f_b1[l], f_freq[l], f_w2[l], f_b2[l], f_w3[l], deltas, sgn, cosm, sinm)
        conv = functools.partial(_long_conv, u3d=proj3d, conv_w=hy_conv_w[l], conv_b=hy_conv_b[l], kr=kr, ki=ki,
                                 skip=hy_skip[l], sgn=sgn, cosm=cosm16, sinm=sinm16, u_col0=3 * ATTN_WIDTH)
        z1 = conv(None, order=0, z_part=0, gate_part=1, out_dtype=F32)
        o_hy = conv(z1, order=1, z_part=None, gate_part=2, out_dtype=BF16)
        x2d = _merge(x2d, attn_norm[l], o_attn.reshape(b * seq, -1), o_hy.reshape(b * seq, -1),
                     w_gate16, b_gate[l], w_pa16, w_ph16, w_out16, l)
        x2d = _conv_ffn(x2d, ffn_norm[l], w_up16, ffn_conv_w[l], ffn_conv_b[l], w_down16, l,
                        final_norm if l == depth - 1 else None, seq=seq)
    return x2d.reshape(b, seq, d)
```

```python
import functools
import math

import jax
import jax.numpy as jnp
from jax import lax
from jax.experimental import pallas as pl
from jax.experimental.pallas import tpu as pltpu

HEAD_DIM = 64
HEADS_PER_GROUP = 8
DILATED_GROUPS = ((128, 1), (512, 4), (2048, 16))
N_GROUPS = len(DILATED_GROUPS)
GROUP_WIDTH = HEADS_PER_GROUP * HEAD_DIM
ATTN_WIDTH = N_GROUPS * GROUP_WIDTH
ROPE_THETA = 10000.0
HYENA_ORDER = 2
HYENA_EMB_DIM = 33
HYENA_DECAY_TARGET = 1e-2
HYENA_FAST_DECAY = 0.3
HYENA_SLOW_DECAY = 1.5
RMS_EPS = 1e-6
MASK_VALUE = -1e30

V7X_LANES = 128
V7X_VMEM_LIMIT_BYTES = 60 * 1024 * 1024

F32 = jnp.float32
BF16 = jnp.bfloat16
HIGHEST = lax.Precision.HIGHEST


def _params(*semantics):
    return pltpu.CompilerParams(dimension_semantics=semantics, vmem_limit_bytes=V7X_VMEM_LIMIT_BYTES)


def _rmsnorm(x, g):
    return x * lax.rsqrt(jnp.mean(x * x, axis=-1, keepdims=True) + RMS_EPS) * g


QKV_WIDTH = 3 * GROUP_WIDTH


def _proj_kernel(*refs, normalise, dilation):
    if normalise:
        x_ref, g_ref, w_ref, cos_ref, sin_ref, o_ref, h_ref = refs
    elif dilation > 1:
        h_ref, w_ref, cos_ref, sin_ref, o_ref, stage_ref = refs
    else:
        h_ref, w_ref, cos_ref, sin_ref, o_ref = refs
    j = pl.program_id(1)

    if normalise:
        @pl.when(j == 0)
        def _():
            h_ref[...] = _rmsnorm(x_ref[...], g_ref[...]).astype(BF16)

    acc = jnp.dot(h_ref[...], w_ref[...], preferred_element_type=F32)
    tm = acc.shape[0]

    def put(tile, val):
        if dilation > 1:
            stage_ref[tile] = val
        else:
            o_ref[:, tile * V7X_LANES:(tile + 1) * V7X_LANES] = val.astype(o_ref.dtype)

    is_qkv = j == 0
    cos, sin = cos_ref[...], sin_ref[...]
    tiles_per_group = GROUP_WIDTH // V7X_LANES
    for part, scale in ((0, 1.0 / math.sqrt(HEAD_DIM)), (1, 1.0)):
        cos_p = jnp.where(is_qkv, cos * scale, 1.0)
        sin_p = jnp.where(is_qkv, sin * scale, 0.0)
        for s in range(tiles_per_group // 2):
            lo_tile = part * tiles_per_group + s
            hi_tile = lo_tile + tiles_per_group // 2
            lo = acc[:, lo_tile * V7X_LANES:(lo_tile + 1) * V7X_LANES]
            hi = acc[:, hi_tile * V7X_LANES:(hi_tile + 1) * V7X_LANES]
            put(lo_tile, lo * cos_p - hi * sin_p)
            put(hi_tile, hi * cos_p + lo * sin_p)
    for tile in range(2 * tiles_per_group, acc.shape[1] // V7X_LANES):
        put(tile, acc[:, tile * V7X_LANES:(tile + 1) * V7X_LANES])
    if dilation > 1:
        for c in range(dilation):
            for tile in range(acc.shape[1] // V7X_LANES):
                o_ref[0, c, :, tile * V7X_LANES:(tile + 1) * V7X_LANES] = (
                    stage_ref[tile, pl.ds(c, tm // dilation, stride=dilation), :].astype(o_ref.dtype))


def _norm_proj(x2d, gain, w, layer, n_blocks, cos, sin, *, seq, tm=1024):
    rows, d = x2d.shape
    table_spec = pl.BlockSpec((tm, V7X_LANES), lambda i, j: (i % (seq // tm), 0))
    return pl.pallas_call(
        functools.partial(_proj_kernel, normalise=True, dilation=1),
        grid=(rows // tm, n_blocks),
        in_specs=[
            pl.BlockSpec((tm, d), lambda i, j: (i, 0)),
            pl.BlockSpec((1, d), lambda i, j: (0, 0)),
            pl.BlockSpec((None, d, QKV_WIDTH), lambda i, j: (layer, 0, j)),
            table_spec,
            table_spec,
        ],
        out_specs=[pl.BlockSpec((tm, QKV_WIDTH), lambda i, j: (i, j)), pl.BlockSpec((tm, d), lambda i, j: (i, 0))],
        out_shape=[jax.ShapeDtypeStruct((rows, n_blocks * QKV_WIDTH), BF16), jax.ShapeDtypeStruct((rows, d), BF16)],
        compiler_params=_params("parallel", "arbitrary"),
        name="norm_proj",
    )(x2d, gain.reshape(1, d), w, cos, sin)


def _dilated_proj(h2d, w, layer, block, dilation, cos, sin, *, batch, seq, tm=1024):
    rows, d = h2d.shape
    blocks_per_seq = seq // tm
    table_spec = pl.BlockSpec((tm, V7X_LANES), lambda i, j: (i % blocks_per_seq, 0))
    return pl.pallas_call(
        functools.partial(_proj_kernel, normalise=False, dilation=dilation),
        grid=(rows // tm, 1),
        in_specs=[
            pl.BlockSpec((tm, d), lambda i, j: (i, 0)),
            pl.BlockSpec((None, d, QKV_WIDTH), lambda i, j: (layer, 0, block), pipeline_mode=pl.Buffered(1)),
            table_spec,
            table_spec,
        ],
        out_specs=pl.BlockSpec((1, dilation, tm // dilation, QKV_WIDTH),
                               lambda i, j: (i // blocks_per_seq, 0, i % blocks_per_seq, 0)),
        out_shape=jax.ShapeDtypeStruct((batch, dilation, seq // dilation, QKV_WIDTH), BF16),
        scratch_shapes=[pltpu.VMEM((QKV_WIDTH // V7X_LANES, tm, V7X_LANES), F32)],
        compiler_params=_params("parallel", "arbitrary"),
        name=f"dilated_proj{dilation}",
    )(h2d, w, cos, sin)


HEADS_PER_STEP = 4
HALF_DIM = HEAD_DIM // 2
QUAD_WIDTH = HEADS_PER_STEP * HEAD_DIM


def _attn_group(q_lo_ref, q_hi_ref, k_lo_ref, k_hi_ref, v_ref, bias_ref, acc_ref, m_ref, lsw_ref, g, r, t_len, nside):
    qb = 2 * nside
    kw = min(t_len, qb + 2 * nside)
    nb = t_len // qb
    lane = lax.broadcasted_iota(jnp.int32, (1, V7X_LANES), 1)
    lane_lo = lane < HEAD_DIM
    head_lanes = [jnp.logical_and(lane >= h * HALF_DIM, lane < (h + 1) * HALF_DIM) for h in range(HEADS_PER_STEP)]

    def body(idx, carry):
        c = idx // nb
        q0 = pl.multiple_of((idx % nb) * qb, qb)
        ks = pl.multiple_of(jnp.clip(q0 - nside, 0, t_len - kw), nside)
        q_lo, q_hi = q_lo_ref[0, c, pl.ds(q0, qb), :], q_hi_ref[0, c, pl.ds(q0, qb), :]
        k2 = jnp.concatenate([k_lo_ref[0, c, pl.ds(ks, kw), :], k_hi_ref[0, c, pl.ds(ks, kw), :]], axis=1)
        zero = jnp.zeros_like(q_lo)
        qs = jnp.concatenate([jnp.concatenate([jnp.where(hl, q_lo, zero), jnp.where(hl, q_hi, zero)], axis=1)
                              for hl in head_lanes], axis=0)
        s = lax.dot_general(qs, k2, (((1,), (1,)), ((), ())), preferred_element_type=F32)
        bias = bias_ref[(q0 - ks) // nside, :, :kw]
        s = (s.reshape(HEADS_PER_STEP, qb, kw) + bias[None]).reshape(HEADS_PER_STEP * qb, kw)
        m = jnp.max(s, axis=-1, keepdims=True)
        p = jnp.exp(s - m).astype(BF16)
        rows = pl.ds(q0 * r + c, qb, stride=r) if r > 1 else pl.ds(q0, qb)
        for pair in range(HEADS_PER_STEP // 2):
            cols = slice(pair * V7X_LANES, (pair + 1) * V7X_LANES)
            v = v_ref[0, c, pl.ds(ks, kw), cols]
            one = jnp.ones_like(v)
            pv_a = jnp.dot(p[(2 * pair) * qb:(2 * pair + 1) * qb], jnp.where(lane_lo, v, one), preferred_element_type=F32)
            pv_b = jnp.dot(p[(2 * pair + 1) * qb:(2 * pair + 2) * qb], jnp.where(lane_lo, one, v), preferred_element_type=F32)
            slot = g * (HEADS_PER_STEP // 2) + pair
            acc_ref[slot, rows, :] = jnp.where(lane_lo, pv_a, pv_b)
            lsw_ref[slot, rows, :] = jnp.where(lane_lo, pv_b, pv_a)
            m_ref[slot, rows, :] = jnp.where(lane_lo, m[(2 * pair) * qb:(2 * pair + 1) * qb],
                                             m[(2 * pair + 1) * qb:(2 * pair + 2) * qb])
        return carry

    lax.fori_loop(0, r * nb, body, 0, unroll=8)


def _attention_kernel(*refs, seq, chunk):
    qkv = refs[:5 * N_GROUPS]
    bias_ref, o_ref, acc_ref, m_ref, lsw_ref = refs[5 * N_GROUPS:]
    for g, (window, r) in enumerate(DILATED_GROUPS):
        _attn_group(*qkv[5 * g:5 * g + 5], bias_ref, acc_ref, m_ref, lsw_ref, g, r, seq // r, window // (2 * r))

    def combine(i, carry):
        rows = pl.ds(pl.multiple_of(i * chunk, chunk), chunk)
        for pair in range(HEADS_PER_STEP // 2):
            cols = slice(pair * V7X_LANES, (pair + 1) * V7X_LANES)
            slots = [g * (HEADS_PER_STEP // 2) + pair for g in range(N_GROUPS)]
            ms = [m_ref[slot, rows, :] for slot in slots]
            top = functools.reduce(jnp.maximum, ms)
            ws = [jnp.exp(m - top) for m in ms]
            num = sum(w * acc_ref[slot, rows, :] for slot, w in zip(slots, ws))
            den = sum(w * pltpu.roll(lsw_ref[slot, rows, :], HEAD_DIM, 1) for slot, w in zip(slots, ws))
            o_ref[0, rows, cols] = (num / den).astype(o_ref.dtype)
        return carry

    lax.fori_loop(0, seq // chunk, combine, 0)


def _band_bias(nside):
    u = jnp.arange(2 * nside)[None, :, None]
    j = jnp.arange(4 * nside)[None, None, :]
    off = (jnp.arange(3) * nside)[:, None, None]
    return jnp.where(jnp.abs(j - u - off) <= nside, 0.0, MASK_VALUE).astype(F32)


def _attention(group_qkv, seq):
    b = group_qkv[0].shape[0]
    n_quads = GROUP_WIDTH // QUAD_WIDTH
    tiles_per_half = GROUP_WIDTH // (2 * V7X_LANES)
    nside = DILATED_GROUPS[0][0] // (2 * DILATED_GROUPS[0][1])
    assert all(w // (2 * r) == nside for w, r in DILATED_GROUPS)
    args, specs = [], []
    for src, (_, r) in zip(group_qkv, DILATED_GROUPS):
        assert src.shape[:3] == (b, r, seq // r)
        for part in range(3):
            t0 = part * GROUP_WIDTH // V7X_LANES
            if part < 2:
                for half in range(2):
                    args.append(src)
                    specs.append(pl.BlockSpec((1, r, seq // r, V7X_LANES),
                                              lambda i, hq, t=t0 + half * tiles_per_half: (i, 0, 0, t + hq)))
            else:
                args.append(src)
                specs.append(pl.BlockSpec((1, r, seq // r, QUAD_WIDTH),
                                          lambda i, hq, t=t0 * V7X_LANES // QUAD_WIDTH: (i, 0, 0, t + hq)))
    args.append(_band_bias(nside))
    specs.append(pl.BlockSpec((3, 2 * nside, 4 * nside), lambda i, hq: (0, 0, 0)))
    return pl.pallas_call(
        functools.partial(_attention_kernel, seq=seq, chunk=256),
        grid=(b, n_quads),
        in_specs=specs,
        out_specs=pl.BlockSpec((1, seq, QUAD_WIDTH), lambda i, hq: (i, 0, hq)),
        out_shape=jax.ShapeDtypeStruct((b, seq, GROUP_WIDTH), BF16),
        scratch_shapes=[pltpu.VMEM((N_GROUPS * HEADS_PER_STEP // 2, seq, V7X_LANES), F32)] * 3,
        compiler_params=_params("parallel", "parallel"),
        name="attention",
    )(*args)


def _filter_kernel(feats_ref, t_ref, w1_ref, b1_ref, freq_ref, w2_ref, b2_ref, w3f_ref, w3b_ref, delta_ref, sgn_ref,
                   cos_hi_ref, cos_lo_ref, sin_hi_ref, sin_lo_ref, kr_ref, ki_ref, sum_ref, dif_ref, nyq_ref):
    fb = pl.program_id(2)
    tc = kr_ref.shape[2]

    @pl.when(fb == 0)
    def _():
        dot = functools.partial(jnp.dot, precision=HIGHEST, preferred_element_type=F32)
        h = jnp.sin(freq_ref[0:1, :] * (dot(feats_ref[...], w1_ref[...]) + b1_ref[...]))
        h = jnp.sin(freq_ref[1:2, :] * (dot(h, w2_ref[...]) + b2_ref[...]))
        decay = jnp.exp(-t_ref[...] * jnp.abs(delta_ref[...]))
        fwd = dot(h, w3f_ref[...]) * decay
        bwd = dot(h, w3b_ref[...]) * decay
        row = lax.broadcasted_iota(jnp.int32, (fwd.shape[0], 1), 0)
        bwd = jnp.where(row == 0, 0.0, bwd)
        norm = jnp.sum(jnp.abs(fwd), axis=0, keepdims=True) + jnp.sum(jnp.abs(bwd), axis=0, keepdims=True)
        fwd = fwd / norm
        bwd = bwd / norm
        nyq_ref[...] = jnp.sum(sgn_ref[...] * (fwd + bwd), axis=0, keepdims=True)
        for ref, val in ((sum_ref, fwd + bwd), (dif_ref, fwd - bwd)):
            hi = val.astype(BF16)
            ref[:, :tc] = hi
            ref[:, tc:] = (val - hi.astype(F32)).astype(BF16)

    def dft(m_hi_ref, m_lo_ref, v_ref):
        hi_terms = jnp.dot(m_hi_ref[...], v_ref[...], preferred_element_type=F32)
        return hi_terms[:, :tc] + hi_terms[:, tc:] + jnp.dot(m_lo_ref[...], v_ref[:, :tc], preferred_element_type=F32)

    kr = dft(cos_hi_ref, cos_lo_ref, sum_ref)
    ki = dft(sin_hi_ref, sin_lo_ref, dif_ref)
    row0 = jnp.logical_and(lax.broadcasted_iota(jnp.int32, (kr.shape[0], 1), 0) == 0, fb == 0)
    kr_ref[0] = jnp.where(row0, 0.5 * kr, kr)
    ki_ref[0] = jnp.where(row0, nyq_ref[...], ki)


def _filter_spectrum(feats, t, w1, b1, freq, w2, b2, w3, deltas, sgn, dft_hi_lo, *, tc=256, tf=512):
    length = feats.shape[0]
    hid = w2.shape[0]
    chans = deltas.shape[1]
    ncb = chans // tc
    const = lambda o, cb, fb: (0, 0)
    out = jax.ShapeDtypeStruct((HYENA_ORDER, length, chans), F32)
    return pl.pallas_call(
        _filter_kernel,
        grid=(HYENA_ORDER, ncb, length // tf),
        in_specs=[
            pl.BlockSpec(feats.shape, const),
            pl.BlockSpec((length, 1), const),
            pl.BlockSpec(w1.shape, const),
            pl.BlockSpec((1, hid), const),
            pl.BlockSpec((2, hid), const),
            pl.BlockSpec((hid, hid), const),
            pl.BlockSpec((1, hid), const),
            pl.BlockSpec((hid, tc), lambda o, cb, fb: (0, (2 * o) * ncb + cb)),
            pl.BlockSpec((hid, tc), lambda o, cb, fb: (0, (2 * o + 1) * ncb + cb)),
            pl.BlockSpec((1, tc), lambda o, cb, fb: (0, cb)),
            pl.BlockSpec((length, 1), const),
        ] + [pl.BlockSpec((tf, length), lambda o, cb, fb: (fb, 0))] * 4,
        out_specs=[pl.BlockSpec((1, tf, tc), lambda o, cb, fb: (o, fb, cb))] * 2,
        out_shape=[out, out],
        scratch_shapes=[pltpu.VMEM((length, 2 * tc), BF16)] * 2 + [pltpu.VMEM((1, tc), F32)],
        compiler_params=_params("parallel", "parallel", "arbitrary"),
        name="hyena_filter",
    )(feats, t, w1, b1.reshape(1, hid), freq, w2, b2.reshape(1, hid), w3, w3, deltas, sgn, *dft_hi_lo)


def _dwconv3_full(a, w_ref, b_ref):
    n = a.shape[0]
    row = lax.broadcasted_iota(jnp.int32, (n, 1), 0)
    prev = jnp.where(row == 0, 0.0, pltpu.roll(a, 1, 0))
    nxt = jnp.where(row == n - 1, 0.0, pltpu.roll(a, n - 1, 0))
    return prev * w_ref[0:1, :] + a * w_ref[1:2, :] + nxt * w_ref[2:3, :] + b_ref[...]


DFT_ROWS = 512


def _long_conv_kernel(*refs, z_from_u):
    if z_from_u:
        (uz_ref, cwz_ref, cbz_ref, ug_ref, cwg_ref, cbg_ref, kr_ref, ki_ref, skip_ref, sgn_ref, cosm_ref, sinm_ref,
         o_ref, yr_ref, yi_ref) = refs
        z = _dwconv3_full(uz_ref[0].astype(F32), cwz_ref, cbz_ref)
    else:
        z_ref, ug_ref, cwg_ref, cbg_ref, kr_ref, ki_ref, skip_ref, sgn_ref, cosm_ref, sinm_ref, o_ref, yr_ref, yi_ref = refs
        z = z_ref[0]
    gate = _dwconv3_full(ug_ref[0].astype(F32), cwg_ref, cbg_ref)
    n = z.shape[0]
    zb = z.astype(BF16)
    sgn = sgn_ref[...]
    xnyq = jnp.sum(sgn * zb.astype(F32), axis=0, keepdims=True)
    nyq = 0.5 * (xnyq * ki_ref[0, 0:1, :])
    skip = skip_ref[0]
    for c in range(n // DFT_ROWS):
        rows = slice(c * DFT_ROWS, (c + 1) * DFT_ROWS)
        xr = jnp.dot(cosm_ref[rows, :], zb, preferred_element_type=F32)
        xi = jnp.dot(sinm_ref[rows, :], zb, preferred_element_type=F32)
        kr, ki = kr_ref[0, rows, :], ki_ref[0, rows, :]
        yr_ref[rows, :] = (xr * kr - xi * ki).astype(BF16)
        yi_ref[rows, :] = (xr * ki + xi * kr).astype(BF16)
    for c in range(n // DFT_ROWS):
        rows = slice(c * DFT_ROWS, (c + 1) * DFT_ROWS)
        y = (jnp.dot(cosm_ref[rows, :], yr_ref[...], preferred_element_type=F32)
             + jnp.dot(sinm_ref[rows, :], yi_ref[...], preferred_element_type=F32))
        y = (y + sgn[rows, :] * nyq) * (1.0 / n)
        o_ref[0, rows, :] = (gate[rows, :] * (y + z[rows, :] * skip)).astype(o_ref.dtype)


def _long_conv(z_src, u3d, conv_w, conv_b, kr, ki, skip, sgn, cosm, sinm, *, order, z_part, gate_part, u_col0, out_dtype, tc=256):
    b, seq, _ = u3d.shape
    chans = kr.shape[2]
    ncb = chans // tc
    u_cb0 = u_col0 // tc

    def u_specs(part):
        return [
            pl.BlockSpec((1, seq, tc), lambda cb, i: (i, 0, u_cb0 + part * ncb + cb)),
            pl.BlockSpec((3, tc), lambda cb, i: (0, part * ncb + cb)),
            pl.BlockSpec((1, tc), lambda cb, i: (0, part * ncb + cb)),
        ]

    conv_b2 = conv_b.reshape(1, -1)
    if z_src is None:
        args, specs = [u3d, conv_w, conv_b2], u_specs(z_part)
    else:
        args, specs = [z_src], [pl.BlockSpec((1, seq, tc), lambda cb, i: (i, 0, cb))]
    args += [u3d, conv_w, conv_b2, kr, ki, skip.reshape(HYENA_ORDER, 1, chans), sgn, cosm, sinm]
    specs += u_specs(gate_part) + [
        pl.BlockSpec((1, seq, tc), lambda cb, i: (order, 0, cb)),
        pl.BlockSpec((1, seq, tc), lambda cb, i: (order, 0, cb)),
        pl.BlockSpec((1, 1, tc), lambda cb, i: (order, 0, cb)),
        pl.BlockSpec((seq, 1), lambda cb, i: (0, 0)),
        pl.BlockSpec((seq, seq), lambda cb, i: (0, 0), pipeline_mode=pl.Buffered(1)),
        pl.BlockSpec((seq, seq), lambda cb, i: (0, 0), pipeline_mode=pl.Buffered(1)),
    ]
    return pl.pallas_call(
        functools.partial(_long_conv_kernel, z_from_u=z_src is None),
        grid=(ncb, b),
        in_specs=specs,
        out_specs=pl.BlockSpec((1, seq, tc), lambda cb, i: (i, 0, cb)),
        out_shape=jax.ShapeDtypeStruct((b, seq, chans), out_dtype),
        scratch_shapes=[pltpu.VMEM((seq, tc), BF16)] * 2,
        compiler_params=_params("parallel", "parallel"),
        name=f"long_conv{order}",
    )(*args)


def _merge_kernel(x_ref, g_ref, oa_ref, oh_ref, wga_ref, wgh_ref, bga_ref, bgh_ref, wpa_ref, wph_ref, wo_ref, o_ref, h_ref):
    c = pl.program_id(1)

    @pl.when(c == 0)
    def _():
        x = x_ref[...]
        h_ref[...] = _rmsnorm(x, g_ref[...]).astype(BF16)
        o_ref[...] = x

    h = h_ref[...]
    gate_a = 1.0 / (1.0 + jnp.exp(-(jnp.dot(h, wga_ref[...], preferred_element_type=F32) + bga_ref[...])))
    gate_h = 1.0 / (1.0 + jnp.exp(-(jnp.dot(h, wgh_ref[...], preferred_element_type=F32) + bgh_ref[...])))
    pa = jnp.dot(oa_ref[...], wpa_ref[...], preferred_element_type=F32)
    ph = jnp.dot(oh_ref[...], wph_ref[...], preferred_element_type=F32)
    mixed = (gate_a * pa + gate_h * ph).astype(BF16)
    o_ref[...] += jnp.dot(mixed, wo_ref[...], preferred_element_type=F32)


def _merge(x2d, gain, o_attn, o_hy, w_gate, b_gate, w_pa, w_ph, w_out, layer, *, tm=512, tc=512):
    rows, d = x2d.shape
    ncb = d // tc
    b_gate2 = b_gate.reshape(1, -1)
    return pl.pallas_call(
        _merge_kernel,
        grid=(rows // tm, ncb),
        in_specs=[
            pl.BlockSpec((tm, d), lambda i, c: (i, 0)),
            pl.BlockSpec((1, d), lambda i, c: (0, 0)),
            pl.BlockSpec((tm, o_attn.shape[1]), lambda i, c: (i, 0)),
            pl.BlockSpec((tm, o_hy.shape[1]), lambda i, c: (i, 0)),
            pl.BlockSpec((None, d, tc), lambda i, c: (layer, 0, c)),
            pl.BlockSpec((None, d, tc), lambda i, c: (layer, 0, ncb + c)),
            pl.BlockSpec((1, tc), lambda i, c: (0, c)),
            pl.BlockSpec((1, tc), lambda i, c: (0, ncb + c)),
            pl.BlockSpec((None, w_pa.shape[1], tc), lambda i, c: (layer, 0, c)),
            pl.BlockSpec((None, w_ph.shape[1], tc), lambda i, c: (layer, 0, c)),
            pl.BlockSpec((None, tc, d), lambda i, c: (layer, c, 0)),
        ],
        out_specs=pl.BlockSpec((tm, d), lambda i, c: (i, 0)),
        out_shape=jax.ShapeDtypeStruct((rows, d), F32),
        scratch_shapes=[pltpu.VMEM((tm, d), BF16)],
        compiler_params=_params("parallel", "arbitrary"),
        name="merge",
    )(x2d, gain.reshape(1, d), o_attn, o_hy, w_gate, w_gate, b_gate2, b_gate2, w_pa, w_ph, w_out)


HALO = 16


def _conv_ffn_kernel(*refs, blocks_per_seq, final_norm):
    if final_norm:
        x_ref, xp_ref, xn_ref, g_ref, wa_ref, wb_ref, cw_ref, cb_ref, wd_ref, gf_ref, o_ref, h_ref = refs
    else:
        x_ref, xp_ref, xn_ref, g_ref, wa_ref, wb_ref, cw_ref, cb_ref, wd_ref, o_ref, h_ref = refs
    i, f = pl.program_id(0), pl.program_id(1)
    tm = x_ref.shape[0]

    @pl.when(f == 0)
    def _():
        x = x_ref[...]
        g = g_ref[...]
        h_ref[0:HALO, :] = _rmsnorm(xp_ref[...], g).astype(BF16)
        h_ref[HALO:HALO + tm, :] = _rmsnorm(x, g).astype(BF16)
        h_ref[HALO + tm:, :] = _rmsnorm(xn_ref[...], g).astype(BF16)
        o_ref[...] = x

    up_a = jnp.dot(h_ref[...], wa_ref[...], preferred_element_type=F32)
    up_b = jnp.dot(h_ref[HALO:HALO + tm, :], wb_ref[...], preferred_element_type=F32)
    n_ext = up_a.shape[0]
    row = lax.broadcasted_iota(jnp.int32, (tm, 1), 0)
    seq_pos = i % blocks_per_seq
    at_start = jnp.logical_and(row == 0, seq_pos == 0)
    at_end = jnp.logical_and(row == tm - 1, seq_pos == blocks_per_seq - 1)
    a_prev = jnp.where(at_start, 0.0, pltpu.roll(up_a, 1, 0)[HALO:HALO + tm])
    a_next = jnp.where(at_end, 0.0, pltpu.roll(up_a, n_ext - 1, 0)[HALO:HALO + tm])
    a = a_prev * cw_ref[0:1, :] + up_a[HALO:HALO + tm] * cw_ref[1:2, :] + a_next * cw_ref[2:3, :] + cb_ref[...]
    gelu = 0.5 * a * (1.0 + jnp.tanh(math.sqrt(2.0 / math.pi) * (a + 0.044715 * (a * a * a))))
    o_ref[...] += jnp.dot((gelu * up_b).astype(BF16), wd_ref[...], preferred_element_type=F32)

    if final_norm:
        @pl.when(f == pl.num_programs(1) - 1)
        def _():
            o_ref[...] = _rmsnorm(o_ref[...], gf_ref[...])


def _conv_ffn(x2d, gain, w_up, conv_w, conv_b, w_down, layer, final_gain, *, seq, tm=1024, tf=512):
    rows, d = x2d.shape
    d_ff = w_down.shape[1]
    nfb = d_ff // tf
    halo_per_block = tm // HALO
    n_halo_blocks = rows // HALO
    args = [x2d, x2d, x2d, gain.reshape(1, d), w_up, w_up, conv_w, conv_b.reshape(1, d_ff), w_down]
    specs = [
        pl.BlockSpec((tm, d), lambda i, f: (i, 0)),
        pl.BlockSpec((HALO, d), lambda i, f: (jnp.maximum(i * halo_per_block - 1, 0), 0)),
        pl.BlockSpec((HALO, d), lambda i, f: (jnp.minimum((i + 1) * halo_per_block, n_halo_blocks - 1), 0)),
        pl.BlockSpec((1, d), lambda i, f: (0, 0)),
        pl.BlockSpec((None, d, tf), lambda i, f: (layer, 0, f)),
        pl.BlockSpec((None, d, tf), lambda i, f: (layer, 0, nfb + f)),
        pl.BlockSpec((3, tf), lambda i, f: (0, f)),
        pl.BlockSpec((1, tf), lambda i, f: (0, f)),
        pl.BlockSpec((None, tf, d), lambda i, f: (layer, f, 0)),
    ]
    if final_gain is not None:
        args.append(final_gain.reshape(1, d))
        specs.append(pl.BlockSpec((1, d), lambda i, f: (0, 0)))
    return pl.pallas_call(
        functools.partial(_conv_ffn_kernel, blocks_per_seq=seq // tm, final_norm=final_gain is not None),
        grid=(rows // tm, nfb),
        in_specs=specs,
        out_specs=pl.BlockSpec((tm, d), lambda i, f: (i, 0)),
        out_shape=jax.ShapeDtypeStruct((rows, d), F32),
        scratch_shapes=[pltpu.VMEM((tm + 2 * HALO, d), BF16)],
        compiler_params=_params("parallel", "arbitrary"),
        name="conv_ffn",
    )(*args)


def _rope_tables(seq):
    pos = jnp.arange(seq, dtype=F32)
    inv = 1.0 / (ROPE_THETA ** (jnp.arange(0, HEAD_DIM, 2, dtype=F32) / HEAD_DIM))
    ang = pos[:, None] * inv[None, :]
    ang = jnp.concatenate([ang] * (V7X_LANES // HALF_DIM), axis=-1)
    return jnp.cos(ang), jnp.sin(ang)


def _arrange_w_in(w_in):
    layers, d, _ = w_in.shape
    qk = w_in[:, :, :2 * ATTN_WIDTH].reshape(layers, d, 2, N_GROUPS, HEADS_PER_GROUP, 2, HALF_DIM)
    qk = qk.transpose(0, 1, 2, 3, 5, 4, 6).reshape(layers, d, 2, N_GROUPS, GROUP_WIDTH)
    v = w_in[:, :, 2 * ATTN_WIDTH:3 * ATTN_WIDTH].reshape(layers, d, N_GROUPS, GROUP_WIDTH)
    groups = [jnp.concatenate([qk[:, :, 0, g], qk[:, :, 1, g], v[:, :, g]], axis=-1) for g in range(N_GROUPS)]
    return jnp.concatenate([groups[0], w_in[:, :, 3 * ATTN_WIDTH:]] + groups[1:], axis=-1)


def _hyena_tables(length, chans):
    t = jnp.linspace(0.0, 1.0, length, dtype=F32)[:, None]
    bands = (HYENA_EMB_DIM - 1) // 2
    w = 2.0 * math.pi * jnp.arange(length, dtype=F32)[:, None] / length
    f = jnp.linspace(1e-4, bands - 1, bands, dtype=F32)[None, :]
    feats = jnp.concatenate([t, jnp.cos(f * w), -jnp.sin(f * w)], axis=-1)
    feats = jnp.pad(feats, ((0, 0), (0, V7X_LANES - HYENA_EMB_DIM)))
    max_decay = math.log(HYENA_DECAY_TARGET) / HYENA_FAST_DECAY
    min_decay = math.log(HYENA_DECAY_TARGET) / HYENA_SLOW_DECAY
    deltas = jnp.linspace(min_decay, max_decay, chans, dtype=F32)[None, :]
    idx = jnp.arange(length, dtype=jnp.int32)
    ang = ((idx[:, None] * idx[None, :]) % (2 * length)).astype(F32) * (math.pi / length)
    sgn = (1 - 2 * (idx % 2)).astype(F32)[:, None]
    return feats, t, deltas, sgn, jnp.cos(ang), -jnp.sin(ang)


def kernel(x, attn_norm, w_in, hy_conv_w, hy_conv_b, f_w1, f_b1, f_freq, f_w2, f_b2, f_w3, hy_skip, w_proj_attn, w_proj_hyena, w_gate, b_gate, w_out, ffn_norm, w_up, ffn_conv_w, ffn_conv_b, w_down, final_norm):
    b, seq, d = x.shape
    depth = w_in.shape[0]
    chans = hy_skip.shape[2]
    cos, sin = _rope_tables(seq)
    feats, t, deltas, sgn, cosm, sinm = _hyena_tables(seq, chans)
    cosm16, sinm16 = cosm.astype(BF16), sinm.astype(BF16)
    dft_hi_lo = (cosm16, (cosm - cosm16.astype(F32)).astype(BF16), sinm16, (sinm - sinm16.astype(F32)).astype(BF16))
    w1_pad = jnp.pad(f_w1, ((0, 0), (0, V7X_LANES - HYENA_EMB_DIM), (0, 0)))
    w_in16 = _arrange_w_in(w_in).astype(BF16)
    w_gate16, w_pa16, w_ph16, w_out16, w_up16, w_down16 = (
        w.astype(BF16) for w in (w_gate, w_proj_attn, w_proj_hyena, w_out, w_up, w_down))
    n_nat_blocks = (w_in.shape[2] - 2 * QKV_WIDTH) // QKV_WIDTH

    x2d = x.reshape(b * seq, d)
    for l in range(depth):
        nat, h2d = _norm_proj(x2d, attn_norm[l], w_in16, l, n_nat_blocks, cos, sin, seq=seq)
        nat3d = nat.reshape(b, seq, -1)
        group_qkv = [nat3d.reshape(b, 1, seq, -1)] + [
            _dilated_proj(h2d, w_in16, l, n_nat_blocks + g - 1, r, cos, sin, batch=b, seq=seq)
            for g, (_, r) in enumerate(DILATED_GROUPS) if g > 0]
        o_attn = _attention(group_qkv, seq)
        kr, ki = _filter_spectrum(feats, t, w1_pad[l], f_b1[l], f_freq[l], f_w2[l], f_b2[l], f_w3[l], deltas, sgn, dft_hi_lo)
        conv = functools.partial(_long_conv, u3d=nat3d, conv_w=hy_conv_w[l], conv_b=hy_conv_b[l], kr=kr, ki=ki,
                                 skip=hy_skip[l], sgn=sgn, cosm=cosm16, sinm=sinm16, u_col0=QKV_WIDTH)
        z1 = conv(None, order=0, z_part=0, gate_part=1, out_dtype=F32)
        o_hy = conv(z1, order=1, z_part=None, gate_part=2, out_dtype=BF16)
        x2d = _merge(x2d, attn_norm[l], o_attn.reshape(b * seq, -1), o_hy.reshape(b * seq, -1),
                     w_gate16, b_gate[l], w_pa16, w_ph16, w_out16, l)
        x2d = _conv_ffn(x2d, ffn_norm[l], w_up16, ffn_conv_w[l], ffn_conv_b[l], w_down16, l,
                        final_norm if l == depth - 1 else None, seq=seq)
    return x2d.reshape(b, seq, d)
```

```python
import functools
import math

import jax
import jax.numpy as jnp
from jax import lax
from jax.experimental import pallas as pl
from jax.experimental.pallas import tpu as pltpu

HEAD_DIM = 64
HEADS_PER_GROUP = 8
DILATED_GROUPS = ((128, 1), (512, 4), (2048, 16))
N_GROUPS = len(DILATED_GROUPS)
GROUP_WIDTH = HEADS_PER_GROUP * HEAD_DIM
ATTN_WIDTH = N_GROUPS * GROUP_WIDTH
ROPE_THETA = 10000.0
HYENA_ORDER = 2
HYENA_EMB_DIM = 33
HYENA_DECAY_TARGET = 1e-2
HYENA_FAST_DECAY = 0.3
HYENA_SLOW_DECAY = 1.5
RMS_EPS = 1e-6
MASK_VALUE = -1e30

V7X_LANES = 128
V7X_VMEM_LIMIT_BYTES = 60 * 1024 * 1024

F32 = jnp.float32
BF16 = jnp.bfloat16
HIGHEST = lax.Precision.HIGHEST


def _params(*semantics):
    return pltpu.CompilerParams(dimension_semantics=semantics, vmem_limit_bytes=V7X_VMEM_LIMIT_BYTES)


def _rmsnorm(x, g):
    return x * lax.rsqrt(jnp.mean(x * x, axis=-1, keepdims=True) + RMS_EPS) * g


QKV_WIDTH = 3 * GROUP_WIDTH
MAX_SUBLANE_STRIDE = 4


def _proj_kernel(*refs, normalise, dilation):
    if normalise:
        x_ref, g_ref, w_ref, cos_ref, sin_ref, o_ref, h_ref = refs
    elif dilation > MAX_SUBLANE_STRIDE:
        h_ref, w_ref, cos_ref, sin_ref, o_ref, stage_ref, stage2_ref = refs
    elif dilation > 1:
        h_ref, w_ref, cos_ref, sin_ref, o_ref, stage_ref = refs
    else:
        h_ref, w_ref, cos_ref, sin_ref, o_ref = refs
    j = pl.program_id(1)

    if normalise:
        @pl.when(j == 0)
        def _():
            h_ref[...] = _rmsnorm(x_ref[...], g_ref[...]).astype(BF16)

    acc = jnp.dot(h_ref[...], w_ref[...], preferred_element_type=F32)
    tm = acc.shape[0]

    def put(tile, val):
        if dilation > 1:
            stage_ref[tile] = val
        else:
            o_ref[:, tile * V7X_LANES:(tile + 1) * V7X_LANES] = val.astype(o_ref.dtype)

    is_qkv = j == 0
    cos, sin = cos_ref[...], sin_ref[...]
    tiles_per_group = GROUP_WIDTH // V7X_LANES
    for part, scale in ((0, 1.0 / math.sqrt(HEAD_DIM)), (1, 1.0)):
        cos_p = jnp.where(is_qkv, cos * scale, 1.0)
        sin_p = jnp.where(is_qkv, sin * scale, 0.0)
        for s in range(tiles_per_group // 2):
            lo_tile = part * tiles_per_group + s
            hi_tile = lo_tile + tiles_per_group // 2
            lo = acc[:, lo_tile * V7X_LANES:(lo_tile + 1) * V7X_LANES]
            hi = acc[:, hi_tile * V7X_LANES:(hi_tile + 1) * V7X_LANES]
            put(lo_tile, lo * cos_p - hi * sin_p)
            put(hi_tile, hi * cos_p + lo * sin_p)
    for tile in range(2 * tiles_per_group, acc.shape[1] // V7X_LANES):
        put(tile, acc[:, tile * V7X_LANES:(tile + 1) * V7X_LANES])
    if dilation > 1:
        n_tiles = acc.shape[1] // V7X_LANES
        first = min(dilation, MAX_SUBLANE_STRIDE)
        second = dilation // first
        if second > 1:
            for c1 in range(first):
                for tile in range(n_tiles):
                    stage2_ref[tile, c1 * (tm // first):(c1 + 1) * (tm // first), :] = (
                        stage_ref[tile, pl.ds(c1, tm // first, stride=first), :])
        for c in range(dilation):
            c1, c2 = c % first, c // first
            for tile in range(n_tiles):
                if second > 1:
                    rows = stage2_ref[tile, pl.ds(c1 * (tm // first) + c2, tm // dilation, stride=second), :]
                else:
                    rows = stage_ref[tile, pl.ds(c, tm // dilation, stride=dilation), :]
                o_ref[0, c, :, tile * V7X_LANES:(tile + 1) * V7X_LANES] = rows.astype(o_ref.dtype)


def _norm_proj(x2d, gain, w, layer, n_blocks, cos, sin, *, seq, tm=1024):
    rows, d = x2d.shape
    table_spec = pl.BlockSpec((tm, V7X_LANES), lambda i, j: (i % (seq // tm), 0))
    return pl.pallas_call(
        functools.partial(_proj_kernel, normalise=True, dilation=1),
        grid=(rows // tm, n_blocks),
        in_specs=[
            pl.BlockSpec((tm, d), lambda i, j: (i, 0)),
            pl.BlockSpec((1, d), lambda i, j: (0, 0)),
            pl.BlockSpec((None, d, QKV_WIDTH), lambda i, j: (layer, 0, j)),
            table_spec,
            table_spec,
        ],
        out_specs=[pl.BlockSpec((tm, QKV_WIDTH), lambda i, j: (i, j)), pl.BlockSpec((tm, d), lambda i, j: (i, 0))],
        out_shape=[jax.ShapeDtypeStruct((rows, n_blocks * QKV_WIDTH), BF16), jax.ShapeDtypeStruct((rows, d), BF16)],
        compiler_params=_params("parallel", "arbitrary"),
        name="norm_proj",
    )(x2d, gain.reshape(1, d), w, cos, sin)


def _dilated_proj(h2d, w, layer, block, dilation, cos, sin, *, batch, seq, tm=1024):
    rows, d = h2d.shape
    blocks_per_seq = seq // tm
    table_spec = pl.BlockSpec((tm, V7X_LANES), lambda i, j: (i % blocks_per_seq, 0))
    return pl.pallas_call(
        functools.partial(_proj_kernel, normalise=False, dilation=dilation),
        grid=(rows // tm, 1),
        in_specs=[
            pl.BlockSpec((tm, d), lambda i, j: (i, 0)),
            pl.BlockSpec((None, d, QKV_WIDTH), lambda i, j: (layer, 0, block), pipeline_mode=pl.Buffered(1)),
            table_spec,
            table_spec,
        ],
        out_specs=pl.BlockSpec((1, dilation, tm // dilation, QKV_WIDTH),
                               lambda i, j: (i // blocks_per_seq, 0, i % blocks_per_seq, 0)),
        out_shape=jax.ShapeDtypeStruct((batch, dilation, seq // dilation, QKV_WIDTH), BF16),
        scratch_shapes=[pltpu.VMEM((QKV_WIDTH // V7X_LANES, tm, V7X_LANES), F32)]
        * (2 if dilation > MAX_SUBLANE_STRIDE else 1),
        compiler_params=_params("parallel", "arbitrary"),
        name=f"dilated_proj{dilation}",
    )(h2d, w, cos, sin)


HEADS_PER_STEP = 4
HALF_DIM = HEAD_DIM // 2
QUAD_WIDTH = HEADS_PER_STEP * HEAD_DIM


def _attn_group(q_lo_ref, q_hi_ref, k_lo_ref, k_hi_ref, v_ref, bias_ref, acc_ref, m_ref, lsw_ref, g, r, t_len, nside):
    qb = 2 * nside
    kw = min(t_len, qb + 2 * nside)
    nb = t_len // qb
    lane = lax.broadcasted_iota(jnp.int32, (1, V7X_LANES), 1)
    lane_lo = lane < HEAD_DIM
    head_lanes = [jnp.logical_and(lane >= h * HALF_DIM, lane < (h + 1) * HALF_DIM) for h in range(HEADS_PER_STEP)]

    def body(idx, carry):
        c = idx // nb
        q0 = pl.multiple_of((idx % nb) * qb, qb)
        ks = pl.multiple_of(jnp.clip(q0 - nside, 0, t_len - kw), nside)
        q_lo, q_hi = q_lo_ref[0, c, pl.ds(q0, qb), :], q_hi_ref[0, c, pl.ds(q0, qb), :]
        k2 = jnp.concatenate([k_lo_ref[0, c, pl.ds(ks, kw), :], k_hi_ref[0, c, pl.ds(ks, kw), :]], axis=1)
        zero = jnp.zeros_like(q_lo)
        qs = jnp.concatenate([jnp.concatenate([jnp.where(hl, q_lo, zero), jnp.where(hl, q_hi, zero)], axis=1)
                              for hl in head_lanes], axis=0)
        s = lax.dot_general(qs, k2, (((1,), (1,)), ((), ())), preferred_element_type=F32)
        bias = bias_ref[(q0 - ks) // nside, :, :kw]
        s = (s.reshape(HEADS_PER_STEP, qb, kw) + bias[None]).reshape(HEADS_PER_STEP * qb, kw)
        m = jnp.max(s, axis=-1, keepdims=True)
        p = jnp.exp(s - m).astype(BF16)
        rows = pl.ds(q0 * r + c, qb, stride=r) if r > 1 else pl.ds(q0, qb)
        for pair in range(HEADS_PER_STEP // 2):
            cols = slice(pair * V7X_LANES, (pair + 1) * V7X_LANES)
            v = v_ref[0, c, pl.ds(ks, kw), cols]
            one = jnp.ones_like(v)
            pv_a = jnp.dot(p[(2 * pair) * qb:(2 * pair + 1) * qb], jnp.where(lane_lo, v, one), preferred_element_type=F32)
            pv_b = jnp.dot(p[(2 * pair + 1) * qb:(2 * pair + 2) * qb], jnp.where(lane_lo, one, v), preferred_element_type=F32)
            slot = g * (HEADS_PER_STEP // 2) + pair
            acc_ref[slot, rows, :] = jnp.where(lane_lo, pv_a, pv_b)
            lsw_ref[slot, rows, :] = jnp.where(lane_lo, pv_b, pv_a)
            m_ref[slot, rows, :] = jnp.where(lane_lo, m[(2 * pair) * qb:(2 * pair + 1) * qb],
                                             m[(2 * pair + 1) * qb:(2 * pair + 2) * qb])
        return carry

    lax.fori_loop(0, r * nb, body, 0, unroll=8)


def _attention_kernel(*refs, seq, chunk):
    qkv = refs[:5 * N_GROUPS]
    bias_ref, o_ref, acc_ref, m_ref, lsw_ref = refs[5 * N_GROUPS:]
    for g, (window, r) in enumerate(DILATED_GROUPS):
        _attn_group(*qkv[5 * g:5 * g + 5], bias_ref, acc_ref, m_ref, lsw_ref, g, r, seq // r, window // (2 * r))

    def combine(i, carry):
        rows = pl.ds(pl.multiple_of(i * chunk, chunk), chunk)
        for pair in range(HEADS_PER_STEP // 2):
            cols = slice(pair * V7X_LANES, (pair + 1) * V7X_LANES)
            slots = [g * (HEADS_PER_STEP // 2) + pair for g in range(N_GROUPS)]
            ms = [m_ref[slot, rows, :] for slot in slots]
            top = functools.reduce(jnp.maximum, ms)
            ws = [jnp.exp(m - top) for m in ms]
            num = sum(w * acc_ref[slot, rows, :] for slot, w in zip(slots, ws))
            den = sum(w * pltpu.roll(lsw_ref[slot, rows, :], HEAD_DIM, 1) for slot, w in zip(slots, ws))
            o_ref[0, rows, cols] = (num / den).astype(o_ref.dtype)
        return carry

    lax.fori_loop(0, seq // chunk, combine, 0)


def _band_bias(nside):
    u = jnp.arange(2 * nside)[None, :, None]
    j = jnp.arange(4 * nside)[None, None, :]
    off = (jnp.arange(3) * nside)[:, None, None]
    return jnp.where(jnp.abs(j - u - off) <= nside, 0.0, MASK_VALUE).astype(F32)


def _attention(group_qkv, seq):
    b = group_qkv[0].shape[0]
    n_quads = GROUP_WIDTH // QUAD_WIDTH
    tiles_per_half = GROUP_WIDTH // (2 * V7X_LANES)
    nside = DILATED_GROUPS[0][0] // (2 * DILATED_GROUPS[0][1])
    assert all(w // (2 * r) == nside for w, r in DILATED_GROUPS)
    args, specs = [], []
    for src, (_, r) in zip(group_qkv, DILATED_GROUPS):
        assert src.shape[:3] == (b, r, seq // r)
        for part in range(3):
            t0 = part * GROUP_WIDTH // V7X_LANES
            if part < 2:
                for half in range(2):
                    args.append(src)
                    specs.append(pl.BlockSpec((1, r, seq // r, V7X_LANES),
                                              lambda i, hq, t=t0 + half * tiles_per_half: (i, 0, 0, t + hq)))
            else:
                args.append(src)
                specs.append(pl.BlockSpec((1, r, seq // r, QUAD_WIDTH),
                                          lambda i, hq, t=t0 * V7X_LANES // QUAD_WIDTH: (i, 0, 0, t + hq)))
    args.append(_band_bias(nside))
    specs.append(pl.BlockSpec((3, 2 * nside, 4 * nside), lambda i, hq: (0, 0, 0)))
    return pl.pallas_call(
        functools.partial(_attention_kernel, seq=seq, chunk=256),
        grid=(b, n_quads),
        in_specs=specs,
        out_specs=pl.BlockSpec((1, seq, QUAD_WIDTH), lambda i, hq: (i, 0, hq)),
        out_shape=jax.ShapeDtypeStruct((b, seq, GROUP_WIDTH), BF16),
        scratch_shapes=[pltpu.VMEM((N_GROUPS * HEADS_PER_STEP // 2, seq, V7X_LANES), F32)] * 3,
        compiler_params=_params("parallel", "parallel"),
        name="attention",
    )(*args)


def _filter_kernel(feats_ref, t_ref, w1_ref, b1_ref, freq_ref, w2_ref, b2_ref, w3f_ref, w3b_ref, delta_ref, sgn_ref,
                   cos_hi_ref, cos_lo_ref, sin_hi_ref, sin_lo_ref, kr_ref, ki_ref, sum_ref, dif_ref, nyq_ref, hid_ref):
    fb = pl.program_id(2)
    tc = kr_ref.shape[2]

    dot = functools.partial(jnp.dot, precision=HIGHEST, preferred_element_type=F32)

    @pl.when(jnp.logical_and(fb == 0, jnp.logical_and(pl.program_id(0) == 0, pl.program_id(1) == 0)))
    def _():
        h = jnp.sin(freq_ref[0:1, :] * (dot(feats_ref[...], w1_ref[...]) + b1_ref[...]))
        hid_ref[...] = jnp.sin(freq_ref[1:2, :] * (dot(h, w2_ref[...]) + b2_ref[...]))

    @pl.when(fb == 0)
    def _():
        h = hid_ref[...]
        decay = jnp.exp(-t_ref[...] * jnp.abs(delta_ref[...]))
        fwd = dot(h, w3f_ref[...]) * decay
        bwd = dot(h, w3b_ref[...]) * decay
        row = lax.broadcasted_iota(jnp.int32, (fwd.shape[0], 1), 0)
        bwd = jnp.where(row == 0, 0.0, bwd)
        norm = jnp.sum(jnp.abs(fwd), axis=0, keepdims=True) + jnp.sum(jnp.abs(bwd), axis=0, keepdims=True)
        fwd = fwd / norm
        bwd = bwd / norm
        nyq_ref[...] = jnp.sum(sgn_ref[...] * (fwd + bwd), axis=0, keepdims=True)
        for ref, val in ((sum_ref, fwd + bwd), (dif_ref, fwd - bwd)):
            hi = val.astype(BF16)
            ref[:, :tc] = hi
            ref[:, tc:] = (val - hi.astype(F32)).astype(BF16)

    def dft(m_hi_ref, m_lo_ref, v_ref):
        hi_terms = jnp.dot(m_hi_ref[...], v_ref[...], preferred_element_type=F32)
        return hi_terms[:, :tc] + hi_terms[:, tc:] + jnp.dot(m_lo_ref[...], v_ref[:, :tc], preferred_element_type=F32)

    kr = dft(cos_hi_ref, cos_lo_ref, sum_ref)
    ki = dft(sin_hi_ref, sin_lo_ref, dif_ref)
    row0 = jnp.logical_and(lax.broadcasted_iota(jnp.int32, (kr.shape[0], 1), 0) == 0, fb == 0)
    kr_ref[0] = jnp.where(row0, 0.5 * kr, kr)
    ki_ref[0] = jnp.where(row0, nyq_ref[...], ki)


def _filter_spectrum(feats, t, w1, b1, freq, w2, b2, w3, deltas, sgn, dft_hi_lo, *, tc=256, tf=512):
    length = feats.shape[0]
    hid = w2.shape[0]
    chans = deltas.shape[1]
    ncb = chans // tc
    const = lambda o, cb, fb: (0, 0)
    out = jax.ShapeDtypeStruct((HYENA_ORDER, length, chans), F32)
    return pl.pallas_call(
        _filter_kernel,
        grid=(HYENA_ORDER, ncb, length // tf),
        in_specs=[
            pl.BlockSpec(feats.shape, const),
            pl.BlockSpec((length, 1), const),
            pl.BlockSpec(w1.shape, const),
            pl.BlockSpec((1, hid), const),
            pl.BlockSpec((2, hid), const),
            pl.BlockSpec((hid, hid), const),
            pl.BlockSpec((1, hid), const),
            pl.BlockSpec((hid, tc), lambda o, cb, fb: (0, (2 * o) * ncb + cb)),
            pl.BlockSpec((hid, tc), lambda o, cb, fb: (0, (2 * o + 1) * ncb + cb)),
            pl.BlockSpec((1, tc), lambda o, cb, fb: (0, cb)),
            pl.BlockSpec((length, 1), const),
        ] + [pl.BlockSpec((tf, length), lambda o, cb, fb: (fb, 0))] * 4,
        out_specs=[pl.BlockSpec((1, tf, tc), lambda o, cb, fb: (o, fb, cb))] * 2,
        out_shape=[out, out],
        scratch_shapes=[pltpu.VMEM((length, 2 * tc), BF16)] * 2 + [pltpu.VMEM((1, tc), F32), pltpu.VMEM((length, hid), F32)],
        compiler_params=_params("arbitrary", "arbitrary", "arbitrary"),
        name="hyena_filter",
    )(feats, t, w1, b1.reshape(1, hid), freq, w2, b2.reshape(1, hid), w3, w3, deltas, sgn, *dft_hi_lo)


def _dwconv3_full(a, w_ref, b_ref):
    n = a.shape[0]
    row = lax.broadcasted_iota(jnp.int32, (n, 1), 0)
    prev = jnp.where(row == 0, 0.0, pltpu.roll(a, 1, 0))
    nxt = jnp.where(row == n - 1, 0.0, pltpu.roll(a, n - 1, 0))
    return prev * w_ref[0:1, :] + a * w_ref[1:2, :] + nxt * w_ref[2:3, :] + b_ref[...]


DFT_ROWS = 512


def _long_conv_kernel(*refs, z_from_u):
    if z_from_u:
        (uz_ref, cwz_ref, cbz_ref, ug_ref, cwg_ref, cbg_ref, kr_ref, ki_ref, skip_ref, sgn_ref, cosm_ref, sinm_ref,
         o_ref, yr_ref, yi_ref) = refs
        z = _dwconv3_full(uz_ref[0].astype(F32), cwz_ref, cbz_ref)
    else:
        z_ref, ug_ref, cwg_ref, cbg_ref, kr_ref, ki_ref, skip_ref, sgn_ref, cosm_ref, sinm_ref, o_ref, yr_ref, yi_ref = refs
        z = z_ref[0]
    gate = _dwconv3_full(ug_ref[0].astype(F32), cwg_ref, cbg_ref)
    n = z.shape[0]
    zb = z.astype(BF16)
    sgn = sgn_ref[...]
    xnyq = jnp.sum(sgn * zb.astype(F32), axis=0, keepdims=True)
    nyq = 0.5 * (xnyq * ki_ref[0, 0:1, :])
    skip = skip_ref[0]
    for c in range(n // DFT_ROWS):
        rows = slice(c * DFT_ROWS, (c + 1) * DFT_ROWS)
        xr = jnp.dot(cosm_ref[rows, :], zb, preferred_element_type=F32)
        xi = jnp.dot(sinm_ref[rows, :], zb, preferred_element_type=F32)
        kr, ki = kr_ref[0, rows, :], ki_ref[0, rows, :]
        yr_ref[rows, :] = (xr * kr - xi * ki).astype(BF16)
        yi_ref[rows, :] = (xr * ki + xi * kr).astype(BF16)
    for c in range(n // DFT_ROWS):
        rows = slice(c * DFT_ROWS, (c + 1) * DFT_ROWS)
        y = (jnp.dot(cosm_ref[rows, :], yr_ref[...], preferred_element_type=F32)
             + jnp.dot(sinm_ref[rows, :], yi_ref[...], preferred_element_type=F32))
        y = (y + sgn[rows, :] * nyq) * (1.0 / n)
        o_ref[0, rows, :] = (gate[rows, :] * (y + z[rows, :] * skip)).astype(o_ref.dtype)


def _long_conv(z_src, u3d, conv_w, conv_b, kr, ki, skip, sgn, cosm, sinm, *, order, z_part, gate_part, u_col0, out_dtype, tc=256):
    b, seq, _ = u3d.shape
    chans = kr.shape[2]
    ncb = chans // tc
    u_cb0 = u_col0 // tc

    def u_specs(part):
        return [
            pl.BlockSpec((1, seq, tc), lambda cb, i: (i, 0, u_cb0 + part * ncb + cb)),
            pl.BlockSpec((3, tc), lambda cb, i: (0, part * ncb + cb)),
            pl.BlockSpec((1, tc), lambda cb, i: (0, part * ncb + cb)),
        ]

    conv_b2 = conv_b.reshape(1, -1)
    if z_src is None:
        args, specs = [u3d, conv_w, conv_b2], u_specs(z_part)
    else:
        args, specs = [z_src], [pl.BlockSpec((1, seq, tc), lambda cb, i: (i, 0, cb))]
    args += [u3d, conv_w, conv_b2, kr, ki, skip.reshape(HYENA_ORDER, 1, chans), sgn, cosm, sinm]
    specs += u_specs(gate_part) + [
        pl.BlockSpec((1, seq, tc), lambda cb, i: (order, 0, cb)),
        pl.BlockSpec((1, seq, tc), lambda cb, i: (order, 0, cb)),
        pl.BlockSpec((1, 1, tc), lambda cb, i: (order, 0, cb)),
        pl.BlockSpec((seq, 1), lambda cb, i: (0, 0)),
        pl.BlockSpec((seq, seq), lambda cb, i: (0, 0), pipeline_mode=pl.Buffered(1)),
        pl.BlockSpec((seq, seq), lambda cb, i: (0, 0), pipeline_mode=pl.Buffered(1)),
    ]
    return pl.pallas_call(
        functools.partial(_long_conv_kernel, z_from_u=z_src is None),
        grid=(ncb, b),
        in_specs=specs,
        out_specs=pl.BlockSpec((1, seq, tc), lambda cb, i: (i, 0, cb)),
        out_shape=jax.ShapeDtypeStruct((b, seq, chans), out_dtype),
        scratch_shapes=[pltpu.VMEM((seq, tc), BF16)] * 2,
        compiler_params=_params("parallel", "parallel"),
        name=f"long_conv{order}",
    )(*args)


def _merge_kernel(x_ref, h_ref, oa_ref, oh_ref, wga_ref, wgh_ref, bga_ref, bgh_ref, wpa_ref, wph_ref, wo_ref, o_ref):
    c = pl.program_id(1)

    @pl.when(c == 0)
    def _():
        o_ref[...] = x_ref[...]

    h = h_ref[...]
    gate_a = 1.0 / (1.0 + jnp.exp(-(jnp.dot(h, wga_ref[...], preferred_element_type=F32) + bga_ref[...])))
    gate_h = 1.0 / (1.0 + jnp.exp(-(jnp.dot(h, wgh_ref[...], preferred_element_type=F32) + bgh_ref[...])))
    pa = jnp.dot(oa_ref[...], wpa_ref[...], preferred_element_type=F32)
    ph = jnp.dot(oh_ref[...], wph_ref[...], preferred_element_type=F32)
    mixed = (gate_a * pa + gate_h * ph).astype(BF16)
    o_ref[...] += jnp.dot(mixed, wo_ref[...], preferred_element_type=F32)


def _merge(x2d, h2d, o_attn, o_hy, w_gate, b_gate, w_pa, w_ph, w_out, layer, *, tm=512, tc=512):
    rows, d = x2d.shape
    ncb = d // tc
    b_gate2 = b_gate.reshape(1, -1)
    return pl.pallas_call(
        _merge_kernel,
        grid=(rows // tm, ncb),
        in_specs=[
            pl.BlockSpec((tm, d), lambda i, c: (i, 0)),
            pl.BlockSpec((tm, d), lambda i, c: (i, 0)),
            pl.BlockSpec((tm, o_attn.shape[1]), lambda i, c: (i, 0)),
            pl.BlockSpec((tm, o_hy.shape[1]), lambda i, c: (i, 0)),
            pl.BlockSpec((None, d, tc), lambda i, c: (layer, 0, c)),
            pl.BlockSpec((None, d, tc), lambda i, c: (layer, 0, ncb + c)),
            pl.BlockSpec((1, tc), lambda i, c: (0, c)),
            pl.BlockSpec((1, tc), lambda i, c: (0, ncb + c)),
            pl.BlockSpec((None, w_pa.shape[1], tc), lambda i, c: (layer, 0, c)),
            pl.BlockSpec((None, w_ph.shape[1], tc), lambda i, c: (layer, 0, c)),
            pl.BlockSpec((None, tc, d), lambda i, c: (layer, c, 0)),
        ],
        out_specs=pl.BlockSpec((tm, d), lambda i, c: (i, 0)),
        out_shape=jax.ShapeDtypeStruct((rows, d), F32),
        compiler_params=_params("parallel", "arbitrary"),
        name="merge",
    )(x2d, h2d, o_attn, o_hy, w_gate, w_gate, b_gate2, b_gate2, w_pa, w_ph, w_out)


HALO = 16


def _conv_ffn_kernel(*refs, blocks_per_seq, final_norm):
    if final_norm:
        x_ref, xp_ref, xn_ref, g_ref, wa_ref, wb_ref, cw_ref, cb_ref, wd_ref, gf_ref, o_ref, h_ref = refs
    else:
        x_ref, xp_ref, xn_ref, g_ref, wa_ref, wb_ref, cw_ref, cb_ref, wd_ref, o_ref, h_ref = refs
    i, f = pl.program_id(0), pl.program_id(1)
    tm = x_ref.shape[0]

    @pl.when(f == 0)
    def _():
        x = x_ref[...]
        g = g_ref[...]
        h_ref[0:HALO, :] = _rmsnorm(xp_ref[...], g).astype(BF16)
        h_ref[HALO:HALO + tm, :] = _rmsnorm(x, g).astype(BF16)
        h_ref[HALO + tm:, :] = _rmsnorm(xn_ref[...], g).astype(BF16)
        o_ref[...] = x

    up_a = jnp.dot(h_ref[...], wa_ref[...], preferred_element_type=F32)
    up_b = jnp.dot(h_ref[HALO:HALO + tm, :], wb_ref[...], preferred_element_type=F32)
    n_ext = up_a.shape[0]
    row = lax.broadcasted_iota(jnp.int32, (tm, 1), 0)
    seq_pos = i % blocks_per_seq
    at_start = jnp.logical_and(row == 0, seq_pos == 0)
    at_end = jnp.logical_and(row == tm - 1, seq_pos == blocks_per_seq - 1)
    a_prev = jnp.where(at_start, 0.0, pltpu.roll(up_a, 1, 0)[HALO:HALO + tm])
    a_next = jnp.where(at_end, 0.0, pltpu.roll(up_a, n_ext - 1, 0)[HALO:HALO + tm])
    a = a_prev * cw_ref[0:1, :] + up_a[HALO:HALO + tm] * cw_ref[1:2, :] + a_next * cw_ref[2:3, :] + cb_ref[...]
    gelu = 0.5 * a * (1.0 + jnp.tanh(math.sqrt(2.0 / math.pi) * (a + 0.044715 * (a * a * a))))
    o_ref[...] += jnp.dot((gelu * up_b).astype(BF16), wd_ref[...], preferred_element_type=F32)

    if final_norm:
        @pl.when(f == pl.num_programs(1) - 1)
        def _():
            o_ref[...] = _rmsnorm(o_ref[...], gf_ref[...])


def _conv_ffn(x2d, gain, w_up, conv_w, conv_b, w_down, layer, final_gain, *, seq, tm=1024, tf=512):
    rows, d = x2d.shape
    d_ff = w_down.shape[1]
    nfb = d_ff // tf
    halo_per_block = tm // HALO
    n_halo_blocks = rows // HALO
    args = [x2d, x2d, x2d, gain.reshape(1, d), w_up, w_up, conv_w, conv_b.reshape(1, d_ff), w_down]
    specs = [
        pl.BlockSpec((tm, d), lambda i, f: (i, 0)),
        pl.BlockSpec((HALO, d), lambda i, f: (jnp.maximum(i * halo_per_block - 1, 0), 0)),
        pl.BlockSpec((HALO, d), lambda i, f: (jnp.minimum((i + 1) * halo_per_block, n_halo_blocks - 1), 0)),
        pl.BlockSpec((1, d), lambda i, f: (0, 0)),
        pl.BlockSpec((None, d, tf), lambda i, f: (layer, 0, f)),
        pl.BlockSpec((None, d, tf), lambda i, f: (layer, 0, nfb + f)),
        pl.BlockSpec((3, tf), lambda i, f: (0, f)),
        pl.BlockSpec((1, tf), lambda i, f: (0, f)),
        pl.BlockSpec((None, tf, d), lambda i, f: (layer, f, 0)),
    ]
    if final_gain is not None:
        args.append(final_gain.reshape(1, d))
        specs.append(pl.BlockSpec((1, d), lambda i, f: (0, 0)))
    return pl.pallas_call(
        functools.partial(_conv_ffn_kernel, blocks_per_seq=seq // tm, final_norm=final_gain is not None),
        grid=(rows // tm, nfb),
        in_specs=specs,
        out_specs=pl.BlockSpec((tm, d), lambda i, f: (i, 0)),
        out_shape=jax.ShapeDtypeStruct((rows, d), F32),
        scratch_shapes=[pltpu.VMEM((tm + 2 * HALO, d), BF16)],
        compiler_params=_params("parallel", "arbitrary"),
        name="conv_ffn",
    )(*args)


def _rope_tables(seq):
    pos = jnp.arange(seq, dtype=F32)
    inv = 1.0 / (ROPE_THETA ** (jnp.arange(0, HEAD_DIM, 2, dtype=F32) / HEAD_DIM))
    ang = pos[:, None] * inv[None, :]
    ang = jnp.concatenate([ang] * (V7X_LANES // HALF_DIM), axis=-1)
    return jnp.cos(ang), jnp.sin(ang)


def _arrange_w_in(w_in):
    layers, d, _ = w_in.shape
    qk = w_in[:, :, :2 * ATTN_WIDTH].reshape(layers, d, 2, N_GROUPS, HEADS_PER_GROUP, 2, HALF_DIM)
    qk = qk.transpose(0, 1, 2, 3, 5, 4, 6).reshape(layers, d, 2, N_GROUPS, GROUP_WIDTH)
    v = w_in[:, :, 2 * ATTN_WIDTH:3 * ATTN_WIDTH].reshape(layers, d, N_GROUPS, GROUP_WIDTH)
    groups = [jnp.concatenate([qk[:, :, 0, g], qk[:, :, 1, g], v[:, :, g]], axis=-1) for g in range(N_GROUPS)]
    return jnp.concatenate([groups[0], w_in[:, :, 3 * ATTN_WIDTH:]] + groups[1:], axis=-1)


def _hyena_tables(length, chans):
    t = jnp.linspace(0.0, 1.0, length, dtype=F32)[:, None]
    bands = (HYENA_EMB_DIM - 1) // 2
    w = 2.0 * math.pi * jnp.arange(length, dtype=F32)[:, None] / length
    f = jnp.linspace(1e-4, bands - 1, bands, dtype=F32)[None, :]
    feats = jnp.concatenate([t, jnp.cos(f * w), -jnp.sin(f * w)], axis=-1)
    feats = jnp.pad(feats, ((0, 0), (0, V7X_LANES - HYENA_EMB_DIM)))
    max_decay = math.log(HYENA_DECAY_TARGET) / HYENA_FAST_DECAY
    min_decay = math.log(HYENA_DECAY_TARGET) / HYENA_SLOW_DECAY
    deltas = jnp.linspace(min_decay, max_decay, chans, dtype=F32)[None, :]
    idx = jnp.arange(length, dtype=jnp.int32)
    ang = ((idx[:, None] * idx[None, :]) % (2 * length)).astype(F32) * (math.pi / length)
    sgn = (1 - 2 * (idx % 2)).astype(F32)[:, None]
    return feats, t, deltas, sgn, jnp.cos(ang), -jnp.sin(ang)


def kernel(x, attn_norm, w_in, hy_conv_w, hy_conv_b, f_w1, f_b1, f_freq, f_w2, f_b2, f_w3, hy_skip, w_proj_attn, w_proj_hyena, w_gate, b_gate, w_out, ffn_norm, w_up, ffn_conv_w, ffn_conv_b, w_down, final_norm):
    b, seq, d = x.shape
    depth = w_in.shape[0]
    chans = hy_skip.shape[2]
    cos, sin = _rope_tables(seq)
    feats, t, deltas, sgn, cosm, sinm = _hyena_tables(seq, chans)
    cosm16, sinm16 = cosm.astype(BF16), sinm.astype(BF16)
    dft_hi_lo = (cosm16, (cosm - cosm16.astype(F32)).astype(BF16), sinm16, (sinm - sinm16.astype(F32)).astype(BF16))
    w1_pad = jnp.pad(f_w1, ((0, 0), (0, V7X_LANES - HYENA_EMB_DIM), (0, 0)))
    w_in16 = _arrange_w_in(w_in).astype(BF16)
    w_gate16, w_pa16, w_ph16, w_out16, w_up16, w_down16 = (
        w.astype(BF16) for w in (w_gate, w_proj_attn, w_proj_hyena, w_out, w_up, w_down))
    n_nat_blocks = (w_in.shape[2] - 2 * QKV_WIDTH) // QKV_WIDTH

    x2d = x.reshape(b * seq, d)
    for l in range(depth):
        nat, h2d = _norm_proj(x2d, attn_norm[l], w_in16, l, n_nat_blocks, cos, sin, seq=seq)
        nat3d = nat.reshape(b, seq, -1)
        group_qkv = [nat3d.reshape(b, 1, seq, -1)] + [
            _dilated_proj(h2d, w_in16, l, n_nat_blocks + g - 1, r, cos, sin, batch=b, seq=seq)
            for g, (_, r) in enumerate(DILATED_GROUPS) if g > 0]
        o_attn = _attention(group_qkv, seq)
        kr, ki = _filter_spectrum(feats, t, w1_pad[l], f_b1[l], f_freq[l], f_w2[l], f_b2[l], f_w3[l], deltas, sgn, dft_hi_lo)
        conv = functools.partial(_long_conv, u3d=nat3d, conv_w=hy_conv_w[l], conv_b=hy_conv_b[l], kr=kr, ki=ki,
                                 skip=hy_skip[l], sgn=sgn, cosm=cosm16, sinm=sinm16, u_col0=QKV_WIDTH)
        z1 = conv(None, order=0, z_part=0, gate_part=1, out_dtype=F32)
        o_hy = conv(z1, order=1, z_part=None, gate_part=2, out_dtype=BF16)
        x2d = _merge(x2d, h2d, o_attn.reshape(b * seq, -1), o_hy.reshape(b * seq, -1),
                     w_gate16, b_gate[l], w_pa16, w_ph16, w_out16, l)
        x2d = _conv_ffn(x2d, ffn_norm[l], w_up16, ffn_conv_w[l], ffn_conv_b[l], w_down16, l,
                        final_norm if l == depth - 1 else None, seq=seq)
    return x2d.reshape(b, seq, d)
```

```python
import functools
import math

import jax
import jax.numpy as jnp
from jax import lax
from jax.experimental import pallas as pl
from jax.experimental.pallas import tpu as pltpu

HEAD_DIM = 64
HEADS_PER_GROUP = 8
DILATED_GROUPS = ((128, 1), (512, 4), (2048, 16))
N_GROUPS = len(DILATED_GROUPS)
GROUP_WIDTH = HEADS_PER_GROUP * HEAD_DIM
ATTN_WIDTH = N_GROUPS * GROUP_WIDTH
ROPE_THETA = 10000.0
HYENA_ORDER = 2
HYENA_EMB_DIM = 33
HYENA_DECAY_TARGET = 1e-2
HYENA_FAST_DECAY = 0.3
HYENA_SLOW_DECAY = 1.5
RMS_EPS = 1e-6
MASK_VALUE = -1e30

V7X_LANES = 128
V7X_VMEM_LIMIT_BYTES = 60 * 1024 * 1024

F32 = jnp.float32
BF16 = jnp.bfloat16
HIGHEST = lax.Precision.HIGHEST


def _params(*semantics):
    return pltpu.CompilerParams(dimension_semantics=semantics, vmem_limit_bytes=V7X_VMEM_LIMIT_BYTES)


def _rmsnorm(x, g):
    return x * lax.rsqrt(jnp.mean(x * x, axis=-1, keepdims=True) + RMS_EPS) * g


QKV_WIDTH = 3 * GROUP_WIDTH
MAX_SUBLANE_STRIDE = 4


def _proj_kernel(*refs, normalise, dilation):
    if normalise:
        x_ref, g_ref, w_ref, cos_ref, sin_ref, o_ref, h_ref = refs
    elif dilation > MAX_SUBLANE_STRIDE:
        h_ref, w_ref, cos_ref, sin_ref, o_ref, stage_ref, stage2_ref = refs
    elif dilation > 1:
        h_ref, w_ref, cos_ref, sin_ref, o_ref, stage_ref = refs
    else:
        h_ref, w_ref, cos_ref, sin_ref, o_ref = refs
    j = pl.program_id(1)

    if normalise:
        @pl.when(j == 0)
        def _():
            h_ref[...] = _rmsnorm(x_ref[...], g_ref[...]).astype(BF16)

    acc = jnp.dot(h_ref[...], w_ref[...], preferred_element_type=F32)
    tm = acc.shape[0]

    def put(tile, val):
        if dilation > 1:
            stage_ref[tile] = val
        else:
            o_ref[:, tile * V7X_LANES:(tile + 1) * V7X_LANES] = val.astype(o_ref.dtype)

    is_qkv = j == 0
    cos, sin = cos_ref[...], sin_ref[...]
    tiles_per_group = GROUP_WIDTH // V7X_LANES
    for part, scale in ((0, 1.0 / math.sqrt(HEAD_DIM)), (1, 1.0)):
        cos_p = jnp.where(is_qkv, cos * scale, 1.0)
        sin_p = jnp.where(is_qkv, sin * scale, 0.0)
        for s in range(tiles_per_group // 2):
            lo_tile = part * tiles_per_group + s
            hi_tile = lo_tile + tiles_per_group // 2
            lo = acc[:, lo_tile * V7X_LANES:(lo_tile + 1) * V7X_LANES]
            hi = acc[:, hi_tile * V7X_LANES:(hi_tile + 1) * V7X_LANES]
            put(lo_tile, lo * cos_p - hi * sin_p)
            put(hi_tile, hi * cos_p + lo * sin_p)
    for tile in range(2 * tiles_per_group, acc.shape[1] // V7X_LANES):
        put(tile, acc[:, tile * V7X_LANES:(tile + 1) * V7X_LANES])
    if dilation > 1:
        n_tiles = acc.shape[1] // V7X_LANES
        first = min(dilation, MAX_SUBLANE_STRIDE)
        second = dilation // first
        if second > 1:
            for c1 in range(first):
                for tile in range(n_tiles):
                    stage2_ref[tile, c1 * (tm // first):(c1 + 1) * (tm // first), :] = (
                        stage_ref[tile, pl.ds(c1, tm // first, stride=first), :])
        for c in range(dilation):
            c1, c2 = c % first, c // first
            for tile in range(n_tiles):
                if second > 1:
                    rows = stage2_ref[tile, pl.ds(c1 * (tm // first) + c2, tm // dilation, stride=second), :]
                else:
                    rows = stage_ref[tile, pl.ds(c, tm // dilation, stride=dilation), :]
                o_ref[0, c, :, tile * V7X_LANES:(tile + 1) * V7X_LANES] = rows.astype(o_ref.dtype)


def _norm_proj(x2d, gain, w, layer, n_blocks, cos, sin, *, seq, tm=1024):
    rows, d = x2d.shape
    table_spec = pl.BlockSpec((tm, V7X_LANES), lambda i, j: (i % (seq // tm), 0))
    return pl.pallas_call(
        functools.partial(_proj_kernel, normalise=True, dilation=1),
        grid=(rows // tm, n_blocks),
        in_specs=[
            pl.BlockSpec((tm, d), lambda i, j: (i, 0)),
            pl.BlockSpec((1, d), lambda i, j: (0, 0)),
            pl.BlockSpec((None, d, QKV_WIDTH), lambda i, j: (layer, 0, j)),
            table_spec,
            table_spec,
        ],
        out_specs=[pl.BlockSpec((tm, QKV_WIDTH), lambda i, j: (i, j)), pl.BlockSpec((tm, d), lambda i, j: (i, 0))],
        out_shape=[jax.ShapeDtypeStruct((rows, n_blocks * QKV_WIDTH), BF16), jax.ShapeDtypeStruct((rows, d), BF16)],
        compiler_params=_params("parallel", "arbitrary"),
        name="norm_proj",
    )(x2d, gain.reshape(1, d), w, cos, sin)


def _dilated_proj(h2d, w, layer, block, dilation, cos, sin, *, batch, seq, tm=1024):
    rows, d = h2d.shape
    blocks_per_seq = seq // tm
    table_spec = pl.BlockSpec((tm, V7X_LANES), lambda i, j: (i % blocks_per_seq, 0))
    return pl.pallas_call(
        functools.partial(_proj_kernel, normalise=False, dilation=dilation),
        grid=(rows // tm, 1),
        in_specs=[
            pl.BlockSpec((tm, d), lambda i, j: (i, 0)),
            pl.BlockSpec((None, d, QKV_WIDTH), lambda i, j: (layer, 0, block), pipeline_mode=pl.Buffered(1)),
            table_spec,
            table_spec,
        ],
        out_specs=pl.BlockSpec((1, dilation, tm // dilation, QKV_WIDTH),
                               lambda i, j: (i // blocks_per_seq, 0, i % blocks_per_seq, 0)),
        out_shape=jax.ShapeDtypeStruct((batch, dilation, seq // dilation, QKV_WIDTH), BF16),
        scratch_shapes=[pltpu.VMEM((QKV_WIDTH // V7X_LANES, tm, V7X_LANES), F32)]
        * (2 if dilation > MAX_SUBLANE_STRIDE else 1),
        compiler_params=_params("parallel", "arbitrary"),
        name=f"dilated_proj{dilation}",
    )(h2d, w, cos, sin)


HEADS_PER_STEP = 4
HALF_DIM = HEAD_DIM // 2
QUAD_WIDTH = HEADS_PER_STEP * HEAD_DIM


def _attn_group(q_lo_ref, q_hi_ref, k_lo_ref, k_hi_ref, v_ref, bias_ref, acc_ref, m_ref, lsw_ref, g, r, t_len, nside):
    qb = 2 * nside
    kw = min(t_len, qb + 2 * nside)
    nb = t_len // qb
    lane = lax.broadcasted_iota(jnp.int32, (1, V7X_LANES), 1)
    lane_lo = lane < HEAD_DIM
    head_lanes = [jnp.logical_and(lane >= h * HALF_DIM, lane < (h + 1) * HALF_DIM) for h in range(HEADS_PER_STEP)]

    def body(idx, carry):
        c = idx // nb
        q0 = pl.multiple_of((idx % nb) * qb, qb)
        ks = pl.multiple_of(jnp.clip(q0 - nside, 0, t_len - kw), nside)
        q_lo, q_hi = q_lo_ref[0, c, pl.ds(q0, qb), :], q_hi_ref[0, c, pl.ds(q0, qb), :]
        k2 = jnp.concatenate([k_lo_ref[0, c, pl.ds(ks, kw), :], k_hi_ref[0, c, pl.ds(ks, kw), :]], axis=1)
        zero = jnp.zeros_like(q_lo)
        qs = jnp.concatenate([jnp.concatenate([jnp.where(hl, q_lo, zero), jnp.where(hl, q_hi, zero)], axis=1)
                              for hl in head_lanes], axis=0)
        s = lax.dot_general(qs, k2, (((1,), (1,)), ((), ())), preferred_element_type=F32)
        bias = bias_ref[(q0 - ks) // nside, :, :kw]
        s = (s.reshape(HEADS_PER_STEP, qb, kw) + bias[None]).reshape(HEADS_PER_STEP * qb, kw)
        m = jnp.max(s, axis=-1, keepdims=True)
        p = jnp.exp(s - m).astype(BF16)
        rows = pl.ds(q0 * r + c, qb, stride=r) if r > 1 else pl.ds(q0, qb)
        for pair in range(HEADS_PER_STEP // 2):
            cols = slice(pair * V7X_LANES, (pair + 1) * V7X_LANES)
            v = v_ref[0, c, pl.ds(ks, kw), cols]
            one = jnp.ones_like(v)
            pv_a = jnp.dot(p[(2 * pair) * qb:(2 * pair + 1) * qb], jnp.where(lane_lo, v, one), preferred_element_type=F32)
            pv_b = jnp.dot(p[(2 * pair + 1) * qb:(2 * pair + 2) * qb], jnp.where(lane_lo, one, v), preferred_element_type=F32)
            slot = g * (HEADS_PER_STEP // 2) + pair
            acc_ref[slot, rows, :] = jnp.where(lane_lo, pv_a, pv_b)
            lsw_ref[slot, rows, :] = jnp.where(lane_lo, pv_b, pv_a)
            m_ref[slot, rows, :] = jnp.where(lane_lo, m[(2 * pair) * qb:(2 * pair + 1) * qb],
                                             m[(2 * pair + 1) * qb:(2 * pair + 2) * qb])
        return carry

    lax.fori_loop(0, r * nb, body, 0, unroll=8)


def _attention_kernel(*refs, seq, chunk):
    qkv = refs[:5 * N_GROUPS]
    bias_ref, o_ref, acc_ref, m_ref, lsw_ref = refs[5 * N_GROUPS:]
    for g, (window, r) in enumerate(DILATED_GROUPS):
        _attn_group(*qkv[5 * g:5 * g + 5], bias_ref, acc_ref, m_ref, lsw_ref, g, r, seq // r, window // (2 * r))

    def combine(i, carry):
        rows = pl.ds(pl.multiple_of(i * chunk, chunk), chunk)
        for pair in range(HEADS_PER_STEP // 2):
            cols = slice(pair * V7X_LANES, (pair + 1) * V7X_LANES)
            slots = [g * (HEADS_PER_STEP // 2) + pair for g in range(N_GROUPS)]
            ms = [m_ref[slot, rows, :] for slot in slots]
            top = functools.reduce(jnp.maximum, ms)
            ws = [jnp.exp(m - top) for m in ms]
            num = sum(w * acc_ref[slot, rows, :] for slot, w in zip(slots, ws))
            den = sum(w * pltpu.roll(lsw_ref[slot, rows, :], HEAD_DIM, 1) for slot, w in zip(slots, ws))
            o_ref[0, rows, cols] = (num / den).astype(o_ref.dtype)
        return carry

    lax.fori_loop(0, seq // chunk, combine, 0)


def _band_bias(nside):
    u = jnp.arange(2 * nside)[None, :, None]
    j = jnp.arange(4 * nside)[None, None, :]
    off = (jnp.arange(3) * nside)[:, None, None]
    return jnp.where(jnp.abs(j - u - off) <= nside, 0.0, MASK_VALUE).astype(F32)


def _attention(group_qkv, seq):
    b = group_qkv[0].shape[0]
    n_quads = GROUP_WIDTH // QUAD_WIDTH
    tiles_per_half = GROUP_WIDTH // (2 * V7X_LANES)
    nside = DILATED_GROUPS[0][0] // (2 * DILATED_GROUPS[0][1])
    assert all(w // (2 * r) == nside for w, r in DILATED_GROUPS)
    args, specs = [], []
    for src, (_, r) in zip(group_qkv, DILATED_GROUPS):
        assert src.shape[:3] == (b, r, seq // r)
        for part in range(3):
            t0 = part * GROUP_WIDTH // V7X_LANES
            if part < 2:
                for half in range(2):
                    args.append(src)
                    specs.append(pl.BlockSpec((1, r, seq // r, V7X_LANES),
                                              lambda i, hq, t=t0 + half * tiles_per_half: (i, 0, 0, t + hq)))
            else:
                args.append(src)
                specs.append(pl.BlockSpec((1, r, seq // r, QUAD_WIDTH),
                                          lambda i, hq, t=t0 * V7X_LANES // QUAD_WIDTH: (i, 0, 0, t + hq)))
    args.append(_band_bias(nside))
    specs.append(pl.BlockSpec((3, 2 * nside, 4 * nside), lambda i, hq: (0, 0, 0)))
    return pl.pallas_call(
        functools.partial(_attention_kernel, seq=seq, chunk=256),
        grid=(b, n_quads),
        in_specs=specs,
        out_specs=pl.BlockSpec((1, seq, QUAD_WIDTH), lambda i, hq: (i, 0, hq)),
        out_shape=jax.ShapeDtypeStruct((b, seq, GROUP_WIDTH), BF16),
        scratch_shapes=[pltpu.VMEM((N_GROUPS * HEADS_PER_STEP // 2, seq, V7X_LANES), F32)] * 3,
        compiler_params=_params("parallel", "parallel"),
        name="attention",
    )(*args)


def _filter_kernel(feats_ref, t_ref, w1_ref, b1_ref, freq_ref, w2_ref, b2_ref, w3f_ref, w3b_ref, delta_ref, sgn_ref,
                   cos_hi_ref, cos_lo_ref, sin_hi_ref, sin_lo_ref, kr_ref, ki_ref, sum_ref, dif_ref, nyq_ref, hid_ref):
    fb = pl.program_id(2)
    tc = kr_ref.shape[2]

    dot = functools.partial(jnp.dot, precision=HIGHEST, preferred_element_type=F32)

    @pl.when(jnp.logical_and(fb == 0, jnp.logical_and(pl.program_id(0) == 0, pl.program_id(1) == 0)))
    def _():
        h = jnp.sin(freq_ref[0:1, :] * (dot(feats_ref[...], w1_ref[...]) + b1_ref[...]))
        hid_ref[...] = jnp.sin(freq_ref[1:2, :] * (dot(h, w2_ref[...]) + b2_ref[...]))

    @pl.when(fb == 0)
    def _():
        h = hid_ref[...]
        decay = jnp.exp(-t_ref[...] * jnp.abs(delta_ref[...]))
        fwd = dot(h, w3f_ref[...]) * decay
        bwd = dot(h, w3b_ref[...]) * decay
        row = lax.broadcasted_iota(jnp.int32, (fwd.shape[0], 1), 0)
        bwd = jnp.where(row == 0, 0.0, bwd)
        norm = jnp.sum(jnp.abs(fwd), axis=0, keepdims=True) + jnp.sum(jnp.abs(bwd), axis=0, keepdims=True)
        fwd = fwd / norm
        bwd = bwd / norm
        nyq_ref[...] = jnp.sum(sgn_ref[...] * (fwd + bwd), axis=0, keepdims=True)
        for ref, val in ((sum_ref, fwd + bwd), (dif_ref, fwd - bwd)):
            hi = val.astype(BF16)
            ref[:, :tc] = hi
            ref[:, tc:] = (val - hi.astype(F32)).astype(BF16)

    def dft(m_hi_ref, m_lo_ref, v_ref):
        hi_terms = jnp.dot(m_hi_ref[...], v_ref[...], preferred_element_type=F32)
        return hi_terms[:, :tc] + hi_terms[:, tc:] + jnp.dot(m_lo_ref[...], v_ref[:, :tc], preferred_element_type=F32)

    kr = dft(cos_hi_ref, cos_lo_ref, sum_ref)
    ki = dft(sin_hi_ref, sin_lo_ref, dif_ref)
    row0 = jnp.logical_and(lax.broadcasted_iota(jnp.int32, (kr.shape[0], 1), 0) == 0, fb == 0)
    kr_ref[0] = jnp.where(row0, 0.5 * kr, kr)
    ki_ref[0] = jnp.where(row0, nyq_ref[...], ki)


def _filter_spectrum(feats, t, w1, b1, freq, w2, b2, w3, deltas, sgn, dft_hi_lo, *, tc=256, tf=512):
    length = feats.shape[0]
    hid = w2.shape[0]
    chans = deltas.shape[1]
    ncb = chans // tc
    const = lambda o, cb, fb: (0, 0)
    out = jax.ShapeDtypeStruct((HYENA_ORDER, length, chans), F32)
    return pl.pallas_call(
        _filter_kernel,
        grid=(HYENA_ORDER, ncb, length // tf),
        in_specs=[
            pl.BlockSpec(feats.shape, const),
            pl.BlockSpec((length, 1), const),
            pl.BlockSpec(w1.shape, const),
            pl.BlockSpec((1, hid), const),
            pl.BlockSpec((2, hid), const),
            pl.BlockSpec((hid, hid), const),
            pl.BlockSpec((1, hid), const),
            pl.BlockSpec((hid, tc), lambda o, cb, fb: (0, (2 * o) * ncb + cb)),
            pl.BlockSpec((hid, tc), lambda o, cb, fb: (0, (2 * o + 1) * ncb + cb)),
            pl.BlockSpec((1, tc), lambda o, cb, fb: (0, cb)),
            pl.BlockSpec((length, 1), const),
        ] + [pl.BlockSpec((tf, length), lambda o, cb, fb: (fb, 0))] * 4,
        out_specs=[pl.BlockSpec((1, tf, tc), lambda o, cb, fb: (o, fb, cb))] * 2,
        out_shape=[out, out],
        scratch_shapes=[pltpu.VMEM((length, 2 * tc), BF16)] * 2 + [pltpu.VMEM((1, tc), F32), pltpu.VMEM((length, hid), F32)],
        compiler_params=_params("arbitrary", "arbitrary", "arbitrary"),
        name="hyena_filter",
    )(feats, t, w1, b1.reshape(1, hid), freq, w2, b2.reshape(1, hid), w3, w3, deltas, sgn, *dft_hi_lo)


def _fold_rows(a):
    h = a.shape[1] // 2
    return jnp.stack([a[:, :h], jnp.concatenate([a[:, h:h + 1], a[:, :h:-1]], axis=1)], axis=1)


def _unfold_rows(f):
    return jnp.concatenate([f[:, 0], f[:, 1, 0:1], f[:, 1, :0:-1]], axis=1)


def _dwconv3_folded(u2, w_ref, b_ref):
    a, b = u2[0], u2[1]
    h = a.shape[0]
    row = lax.broadcasted_iota(jnp.int32, (h, 1), 0)
    a_prev = jnp.where(row == 0, 0.0, pltpu.roll(a, 1, 0))
    a_next = jnp.where(row == h - 1, b[0:1, :], pltpu.roll(a, h - 1, 0))
    b_prev = jnp.where(row == 0, a[h - 1:h, :], pltpu.roll(b, h - 1, 0))
    b_next = jnp.where(row == 1, 0.0, pltpu.roll(b, 1, 0))
    w0, w1, w2, bias = w_ref[0:1, :], w_ref[1:2, :], w_ref[2:3, :], b_ref[...]
    return a_prev * w0 + a * w1 + a_next * w2 + bias, b_prev * w0 + b * w1 + b_next * w2 + bias


DFT_ROWS = 512


def _long_conv_kernel(*refs, z_from_u):
    if z_from_u:
        (uz_ref, cwz_ref, cbz_ref, ug_ref, cwg_ref, cbg_ref, kr_ref, ki_ref, skip_ref, sgn_ref,
         c1_ref, s1_ref, c2_ref, s2_ref, c2t_ref, s2t_ref, o_ref, yre_ref, yie_ref, yro_ref, yio_ref) = refs
        za, zb = _dwconv3_folded(uz_ref[0].astype(F32), cwz_ref, cbz_ref)
    else:
        (z_ref, ug_ref, cwg_ref, cbg_ref, kr_ref, ki_ref, skip_ref, sgn_ref,
         c1_ref, s1_ref, c2_ref, s2_ref, c2t_ref, s2t_ref, o_ref, yre_ref, yie_ref, yro_ref, yio_ref) = refs
        za, zb = z_ref[0, 0], z_ref[0, 1]
    ga, gb = _dwconv3_folded(ug_ref[0].astype(F32), cwg_ref, cbg_ref)
    h = za.shape[0]
    first = lax.broadcasted_iota(jnp.int32, (h, 1), 0) == 0
    sgn = sgn_ref[...]
    p = jnp.where(first, za, za + zb).astype(BF16)
    d = jnp.where(first, za, za - zb).astype(BF16)
    b0 = zb[0:1, :]
    xnyq = jnp.sum(sgn * (za + zb), axis=0, keepdims=True)
    ynyq_half = 0.5 * (xnyq * ki_ref[0, 0:1, :])
    skip = skip_ref[0]
    mid = jnp.zeros_like(b0)
    for c in range(h // DFT_ROWS):
        rows = slice(c * DFT_ROWS, (c + 1) * DFT_ROWS)
        odd_rows = slice(h + c * DFT_ROWS, h + (c + 1) * DFT_ROWS)
        edge = sgn[rows, :] * b0
        xr_e = jnp.dot(c1_ref[rows, :], p, preferred_element_type=F32) + edge
        xi_e = jnp.dot(s1_ref[rows, :], d, preferred_element_type=F32)
        xr_o = jnp.dot(c2_ref[rows, :], d, preferred_element_type=F32)
        xi_o = jnp.dot(s2_ref[rows, :], p, preferred_element_type=F32) - edge
        kr_e, ki_e, kr_o, ki_o = kr_ref[0, rows, :], ki_ref[0, rows, :], kr_ref[0, odd_rows, :], ki_ref[0, odd_rows, :]
        yr_e = xr_e * kr_e - xi_e * ki_e
        yi_o = xr_o * ki_o + xi_o * kr_o
        mid = mid + jnp.sum(sgn[rows, :] * (yr_e - yi_o), axis=0, keepdims=True)
        yre_ref[rows, :] = yr_e.astype(BF16)
        yie_ref[rows, :] = (xr_e * ki_e + xi_e * kr_e).astype(BF16)
        yro_ref[rows, :] = (xr_o * kr_o - xi_o * ki_o).astype(BF16)
        yio_ref[rows, :] = yi_o.astype(BF16)
    inv_n = 1.0 / (2 * h)
    for c in range(h // DFT_ROWS):
        rows = slice(c * DFT_ROWS, (c + 1) * DFT_ROWS)
        pp = (jnp.dot(c1_ref[rows, :], yre_ref[...], preferred_element_type=F32)
              + jnp.dot(s2t_ref[rows, :], yio_ref[...], preferred_element_type=F32)) + sgn[rows, :] * ynyq_half
        qq = (jnp.dot(s1_ref[rows, :], yie_ref[...], preferred_element_type=F32)
              + jnp.dot(c2t_ref[rows, :], yro_ref[...], preferred_element_type=F32))
        ya = (pp + qq) * inv_n
        yb = jnp.where(first[rows, :], mid + ynyq_half, pp - qq) * inv_n
        o_ref[0, 0, rows, :] = (ga[rows, :] * (ya + za[rows, :] * skip)).astype(o_ref.dtype)
        o_ref[0, 1, rows, :] = (gb[rows, :] * (yb + zb[rows, :] * skip)).astype(o_ref.dtype)


def _long_conv(z_src, u4d, conv_w, conv_b, kr, ki, skip, sgn, dft, *, order, z_part, gate_part, out_dtype, tc=256):
    b, _, half, _ = u4d.shape
    chans = kr.shape[2]
    ncb = chans // tc

    def u_specs(part):
        return [
            pl.BlockSpec((1, 2, half, tc), lambda cb, i: (i, 0, 0, part * ncb + cb)),
            pl.BlockSpec((3, tc), lambda cb, i: (0, part * ncb + cb)),
            pl.BlockSpec((1, tc), lambda cb, i: (0, part * ncb + cb)),
        ]

    def dft_block(row_block):
        return pl.BlockSpec((half, half), lambda cb, i: (row_block, 0), pipeline_mode=pl.Buffered(1))

    cosm, sinm, cos_odd_t, sin_odd_t = dft
    conv_b2 = conv_b.reshape(1, -1)
    if z_src is None:
        args, specs = [u4d, conv_w, conv_b2], u_specs(z_part)
    else:
        args, specs = [z_src], [pl.BlockSpec((1, 2, half, tc), lambda cb, i: (i, 0, 0, cb))]
    args += [u4d, conv_w, conv_b2, kr, ki, skip.reshape(HYENA_ORDER, 1, chans), sgn,
             cosm, sinm, cosm, sinm, cos_odd_t, sin_odd_t]
    specs += u_specs(gate_part) + [
        pl.BlockSpec((1, 2 * half, tc), lambda cb, i: (order, 0, cb)),
        pl.BlockSpec((1, 2 * half, tc), lambda cb, i: (order, 0, cb)),
        pl.BlockSpec((1, 1, tc), lambda cb, i: (order, 0, cb)),
        pl.BlockSpec((half, 1), lambda cb, i: (0, 0)),
        dft_block(0), dft_block(0), dft_block(1), dft_block(1), dft_block(0), dft_block(0),
    ]
    return pl.pallas_call(
        functools.partial(_long_conv_kernel, z_from_u=z_src is None),
        grid=(ncb, b),
        in_specs=specs,
        out_specs=pl.BlockSpec((1, 2, half, tc), lambda cb, i: (i, 0, 0, cb)),
        out_shape=jax.ShapeDtypeStruct((b, 2, half, chans), out_dtype),
        scratch_shapes=[pltpu.VMEM((half, tc), BF16)] * 4,
        compiler_params=_params("parallel", "parallel"),
        name=f"long_conv{order}",
    )(*args)


def _merge_kernel(x_ref, h_ref, oa_ref, oh_ref, wga_ref, wgh_ref, bga_ref, bgh_ref, wpa_ref, wph_ref, wo_ref, o_ref):
    c = pl.program_id(1)

    @pl.when(c == 0)
    def _():
        o_ref[...] = x_ref[...]

    h = h_ref[...]
    gate_a = 1.0 / (1.0 + jnp.exp(-(jnp.dot(h, wga_ref[...], preferred_element_type=F32) + bga_ref[...])))
    gate_h = 1.0 / (1.0 + jnp.exp(-(jnp.dot(h, wgh_ref[...], preferred_element_type=F32) + bgh_ref[...])))
    pa = jnp.dot(oa_ref[...], wpa_ref[...], preferred_element_type=F32)
    ph = jnp.dot(oh_ref[...], wph_ref[...], preferred_element_type=F32)
    mixed = (gate_a * pa + gate_h * ph).astype(BF16)
    o_ref[...] += jnp.dot(mixed, wo_ref[...], preferred_element_type=F32)


def _merge(x2d, h2d, o_attn, o_hy, w_gate, b_gate, w_pa, w_ph, w_out, layer, *, tm=512, tc=512):
    rows, d = x2d.shape
    ncb = d // tc
    b_gate2 = b_gate.reshape(1, -1)
    return pl.pallas_call(
        _merge_kernel,
        grid=(rows // tm, ncb),
        in_specs=[
            pl.BlockSpec((tm, d), lambda i, c: (i, 0)),
            pl.BlockSpec((tm, d), lambda i, c: (i, 0)),
            pl.BlockSpec((tm, o_attn.shape[1]), lambda i, c: (i, 0)),
            pl.BlockSpec((tm, o_hy.shape[1]), lambda i, c: (i, 0)),
            pl.BlockSpec((None, d, tc), lambda i, c: (layer, 0, c)),
            pl.BlockSpec((None, d, tc), lambda i, c: (layer, 0, ncb + c)),
            pl.BlockSpec((1, tc), lambda i, c: (0, c)),
            pl.BlockSpec((1, tc), lambda i, c: (0, ncb + c)),
            pl.BlockSpec((None, w_pa.shape[1], tc), lambda i, c: (layer, 0, c)),
            pl.BlockSpec((None, w_ph.shape[1], tc), lambda i, c: (layer, 0, c)),
            pl.BlockSpec((None, tc, d), lambda i, c: (layer, c, 0)),
        ],
        out_specs=pl.BlockSpec((tm, d), lambda i, c: (i, 0)),
        out_shape=jax.ShapeDtypeStruct((rows, d), F32),
        compiler_params=_params("parallel", "arbitrary"),
        name="merge",
    )(x2d, h2d, o_attn, o_hy, w_gate, w_gate, b_gate2, b_gate2, w_pa, w_ph, w_out)


HALO = 16


def _conv_ffn_kernel(*refs, blocks_per_seq, final_norm):
    if final_norm:
        x_ref, xp_ref, xn_ref, g_ref, wa_ref, wb_ref, cw_ref, cb_ref, wd_ref, gf_ref, o_ref, h_ref = refs
    else:
        x_ref, xp_ref, xn_ref, g_ref, wa_ref, wb_ref, cw_ref, cb_ref, wd_ref, o_ref, h_ref = refs
    i, f = pl.program_id(0), pl.program_id(1)
    tm = x_ref.shape[0]

    @pl.when(f == 0)
    def _():
        x = x_ref[...]
        g = g_ref[...]
        h_ref[0:HALO, :] = _rmsnorm(xp_ref[...], g).astype(BF16)
        h_ref[HALO:HALO + tm, :] = _rmsnorm(x, g).astype(BF16)
        h_ref[HALO + tm:, :] = _rmsnorm(xn_ref[...], g).astype(BF16)
        o_ref[...] = x

    up_a = jnp.dot(h_ref[...], wa_ref[...], preferred_element_type=F32)
    up_b = jnp.dot(h_ref[HALO:HALO + tm, :], wb_ref[...], preferred_element_type=F32)
    n_ext = up_a.shape[0]
    row = lax.broadcasted_iota(jnp.int32, (tm, 1), 0)
    seq_pos = i % blocks_per_seq
    at_start = jnp.logical_and(row == 0, seq_pos == 0)
    at_end = jnp.logical_and(row == tm - 1, seq_pos == blocks_per_seq - 1)
    a_prev = jnp.where(at_start, 0.0, pltpu.roll(up_a, 1, 0)[HALO:HALO + tm])
    a_next = jnp.where(at_end, 0.0, pltpu.roll(up_a, n_ext - 1, 0)[HALO:HALO + tm])
    a = a_prev * cw_ref[0:1, :] + up_a[HALO:HALO + tm] * cw_ref[1:2, :] + a_next * cw_ref[2:3, :] + cb_ref[...]
    gelu = 0.5 * a * (1.0 + jnp.tanh(math.sqrt(2.0 / math.pi) * (a + 0.044715 * (a * a * a))))
    o_ref[...] += jnp.dot((gelu * up_b).astype(BF16), wd_ref[...], preferred_element_type=F32)

    if final_norm:
        @pl.when(f == pl.num_programs(1) - 1)
        def _():
            o_ref[...] = _rmsnorm(o_ref[...], gf_ref[...])


def _conv_ffn(x2d, gain, w_up, conv_w, conv_b, w_down, layer, final_gain, *, seq, tm=1024, tf=512):
    rows, d = x2d.shape
    d_ff = w_down.shape[1]
    nfb = d_ff // tf
    halo_per_block = tm // HALO
    n_halo_blocks = rows // HALO
    args = [x2d, x2d, x2d, gain.reshape(1, d), w_up, w_up, conv_w, conv_b.reshape(1, d_ff), w_down]
    specs = [
        pl.BlockSpec((tm, d), lambda i, f: (i, 0)),
        pl.BlockSpec((HALO, d), lambda i, f: (jnp.maximum(i * halo_per_block - 1, 0), 0)),
        pl.BlockSpec((HALO, d), lambda i, f: (jnp.minimum((i + 1) * halo_per_block, n_halo_blocks - 1), 0)),
        pl.BlockSpec((1, d), lambda i, f: (0, 0)),
        pl.BlockSpec((None, d, tf), lambda i, f: (layer, 0, f)),
        pl.BlockSpec((None, d, tf), lambda i, f: (layer, 0, nfb + f)),
        pl.BlockSpec((3, tf), lambda i, f: (0, f)),
        pl.BlockSpec((1, tf), lambda i, f: (0, f)),
        pl.BlockSpec((None, tf, d), lambda i, f: (layer, f, 0)),
    ]
    if final_gain is not None:
        args.append(final_gain.reshape(1, d))
        specs.append(pl.BlockSpec((1, d), lambda i, f: (0, 0)))
    return pl.pallas_call(
        functools.partial(_conv_ffn_kernel, blocks_per_seq=seq // tm, final_norm=final_gain is not None),
        grid=(rows // tm, nfb),
        in_specs=specs,
        out_specs=pl.BlockSpec((tm, d), lambda i, f: (i, 0)),
        out_shape=jax.ShapeDtypeStruct((rows, d), F32),
        scratch_shapes=[pltpu.VMEM((tm + 2 * HALO, d), BF16)],
        compiler_params=_params("parallel", "arbitrary"),
        name="conv_ffn",
    )(*args)


def _rope_tables(seq):
    pos = jnp.arange(seq, dtype=F32)
    inv = 1.0 / (ROPE_THETA ** (jnp.arange(0, HEAD_DIM, 2, dtype=F32) / HEAD_DIM))
    ang = pos[:, None] * inv[None, :]
    ang = jnp.concatenate([ang] * (V7X_LANES // HALF_DIM), axis=-1)
    return jnp.cos(ang), jnp.sin(ang)


def _arrange_w_in(w_in):
    layers, d, _ = w_in.shape
    qk = w_in[:, :, :2 * ATTN_WIDTH].reshape(layers, d, 2, N_GROUPS, HEADS_PER_GROUP, 2, HALF_DIM)
    qk = qk.transpose(0, 1, 2, 3, 5, 4, 6).reshape(layers, d, 2, N_GROUPS, GROUP_WIDTH)
    v = w_in[:, :, 2 * ATTN_WIDTH:3 * ATTN_WIDTH].reshape(layers, d, N_GROUPS, GROUP_WIDTH)
    groups = [jnp.concatenate([qk[:, :, 0, g], qk[:, :, 1, g], v[:, :, g]], axis=-1) for g in range(N_GROUPS)]
    return jnp.concatenate([groups[0], w_in[:, :, 3 * ATTN_WIDTH:]] + groups[1:], axis=-1)


def _hyena_tables(length, chans):
    t = jnp.linspace(0.0, 1.0, length, dtype=F32)[:, None]
    bands = (HYENA_EMB_DIM - 1) // 2
    w = 2.0 * math.pi * jnp.arange(length, dtype=F32)[:, None] / length
    f = jnp.linspace(1e-4, bands - 1, bands, dtype=F32)[None, :]
    feats = jnp.concatenate([t, jnp.cos(f * w), -jnp.sin(f * w)], axis=-1)
    feats = jnp.pad(feats, ((0, 0), (0, V7X_LANES - HYENA_EMB_DIM)))
    max_decay = math.log(HYENA_DECAY_TARGET) / HYENA_FAST_DECAY
    min_decay = math.log(HYENA_DECAY_TARGET) / HYENA_SLOW_DECAY
    deltas = jnp.linspace(min_decay, max_decay, chans, dtype=F32)[None, :]
    idx = jnp.arange(length, dtype=jnp.int32)
    freq = jnp.concatenate([idx[0::2], idx[1::2]])
    ang = ((freq[:, None] * idx[None, :]) % (2 * length)).astype(F32) * (math.pi / length)
    sgn = (1 - 2 * (idx % 2)).astype(F32)[:, None]
    return feats, t, deltas, sgn, jnp.cos(ang), -jnp.sin(ang)


def kernel(x, attn_norm, w_in, hy_conv_w, hy_conv_b, f_w1, f_b1, f_freq, f_w2, f_b2, f_w3, hy_skip, w_proj_attn, w_proj_hyena, w_gate, b_gate, w_out, ffn_norm, w_up, ffn_conv_w, ffn_conv_b, w_down, final_norm):
    b, seq, d = x.shape
    depth = w_in.shape[0]
    chans = hy_skip.shape[2]
    cos, sin = _rope_tables(seq)
    feats, t, deltas, sgn, cosm, sinm = _hyena_tables(seq, chans)
    cosm16, sinm16 = cosm.astype(BF16), sinm.astype(BF16)
    dft_hi_lo = (cosm16, (cosm - cosm16.astype(F32)).astype(BF16), sinm16, (sinm - sinm16.astype(F32)).astype(BF16))
    half = seq // 2
    dft16 = (cosm16, sinm16, cosm16[half:, :half].T, sinm16[half:, :half].T)
    w1_pad = jnp.pad(f_w1, ((0, 0), (0, V7X_LANES - HYENA_EMB_DIM), (0, 0)))
    w_in16 = _arrange_w_in(w_in).astype(BF16)
    w_gate16, w_pa16, w_ph16, w_out16, w_up16, w_down16 = (
        w.astype(BF16) for w in (w_gate, w_proj_attn, w_proj_hyena, w_out, w_up, w_down))
    n_nat_blocks = (w_in.shape[2] - 2 * QKV_WIDTH) // QKV_WIDTH

    x2d = x.reshape(b * seq, d)
    for l in range(depth):
        nat, h2d = _norm_proj(x2d, attn_norm[l], w_in16, l, n_nat_blocks, cos, sin, seq=seq)
        nat3d = nat.reshape(b, seq, -1)
        group_qkv = [nat3d.reshape(b, 1, seq, -1)] + [
            _dilated_proj(h2d, w_in16, l, n_nat_blocks + g - 1, r, cos, sin, batch=b, seq=seq)
            for g, (_, r) in enumerate(DILATED_GROUPS) if g > 0]
        o_attn = _attention(group_qkv, seq)
        kr, ki = _filter_spectrum(feats, t, w1_pad[l], f_b1[l], f_freq[l], f_w2[l], f_b2[l], f_w3[l], deltas, sgn, dft_hi_lo)
        conv = functools.partial(_long_conv, u4d=_fold_rows(nat3d[:, :, QKV_WIDTH:]), conv_w=hy_conv_w[l],
                                 conv_b=hy_conv_b[l], kr=kr, ki=ki, skip=hy_skip[l], sgn=sgn[:seq // 2], dft=dft16)
        z1 = conv(None, order=0, z_part=0, gate_part=1, out_dtype=F32)
        o_hy = _unfold_rows(conv(z1, order=1, z_part=None, gate_part=2, out_dtype=BF16))
        x2d = _merge(x2d, h2d, o_attn.reshape(b * seq, -1), o_hy.reshape(b * seq, -1),
                     w_gate16, b_gate[l], w_pa16, w_ph16, w_out16, l)
        x2d = _conv_ffn(x2d, ffn_norm[l], w_up16, ffn_conv_w[l], ffn_conv_b[l], w_down16, l,
                        final_norm if l == depth - 1 else None, seq=seq)
    return x2d.reshape(b, seq, d)
```

```python
import functools
import math

import jax
import jax.numpy as jnp
from jax import lax
from jax.experimental import pallas as pl
from jax.experimental.pallas import tpu as pltpu

HEAD_DIM = 64
HEADS_PER_GROUP = 8
DILATED_GROUPS = ((128, 1), (512, 4), (2048, 16))
N_GROUPS = len(DILATED_GROUPS)
GROUP_WIDTH = HEADS_PER_GROUP * HEAD_DIM
ATTN_WIDTH = N_GROUPS * GROUP_WIDTH
ROPE_THETA = 10000.0
HYENA_ORDER = 2
HYENA_EMB_DIM = 33
HYENA_DECAY_TARGET = 1e-2
HYENA_FAST_DECAY = 0.3
HYENA_SLOW_DECAY = 1.5
RMS_EPS = 1e-6
MASK_VALUE = -1e30

V7X_LANES = 128
V7X_VMEM_LIMIT_BYTES = 60 * 1024 * 1024

F32 = jnp.float32
BF16 = jnp.bfloat16
HIGHEST = lax.Precision.HIGHEST


def _params(*semantics):
    return pltpu.CompilerParams(dimension_semantics=semantics, vmem_limit_bytes=V7X_VMEM_LIMIT_BYTES)


def _rmsnorm(x, g):
    return x * lax.rsqrt(jnp.mean(x * x, axis=-1, keepdims=True) + RMS_EPS) * g


QKV_WIDTH = 3 * GROUP_WIDTH
MAX_SUBLANE_STRIDE = 4


def _proj_kernel(*refs, normalise, dilation):
    if normalise:
        x_ref, g_ref, w_ref, cos_ref, sin_ref, o_ref, h_ref = refs
    elif dilation > MAX_SUBLANE_STRIDE:
        h_ref, w_ref, cos_ref, sin_ref, o_ref, stage_ref, stage2_ref = refs
    elif dilation > 1:
        h_ref, w_ref, cos_ref, sin_ref, o_ref, stage_ref = refs
    else:
        h_ref, w_ref, cos_ref, sin_ref, o_ref = refs
    j = pl.program_id(1)

    if normalise:
        @pl.when(j == 0)
        def _():
            h_ref[...] = _rmsnorm(x_ref[...], g_ref[...]).astype(BF16)

    acc = jnp.dot(h_ref[...], w_ref[...], preferred_element_type=F32)
    tm = acc.shape[0]

    def put(tile, val):
        if dilation > 1:
            stage_ref[tile] = val
        else:
            o_ref[:, tile * V7X_LANES:(tile + 1) * V7X_LANES] = val.astype(o_ref.dtype)

    is_qkv = j == 0
    cos, sin = cos_ref[...], sin_ref[...]
    tiles_per_group = GROUP_WIDTH // V7X_LANES
    for part, scale in ((0, 1.0 / math.sqrt(HEAD_DIM)), (1, 1.0)):
        cos_p = jnp.where(is_qkv, cos * scale, 1.0)
        sin_p = jnp.where(is_qkv, sin * scale, 0.0)
        for s in range(tiles_per_group // 2):
            lo_tile = part * tiles_per_group + s
            hi_tile = lo_tile + tiles_per_group // 2
            lo = acc[:, lo_tile * V7X_LANES:(lo_tile + 1) * V7X_LANES]
            hi = acc[:, hi_tile * V7X_LANES:(hi_tile + 1) * V7X_LANES]
            put(lo_tile, lo * cos_p - hi * sin_p)
            put(hi_tile, hi * cos_p + lo * sin_p)
    for tile in range(2 * tiles_per_group, acc.shape[1] // V7X_LANES):
        put(tile, acc[:, tile * V7X_LANES:(tile + 1) * V7X_LANES])
    if dilation > 1:
        n_tiles = acc.shape[1] // V7X_LANES
        first = min(dilation, MAX_SUBLANE_STRIDE)
        second = dilation // first
        if second > 1:
            for c1 in range(first):
                for tile in range(n_tiles):
                    stage2_ref[tile, c1 * (tm // first):(c1 + 1) * (tm // first), :] = (
                        stage_ref[tile, pl.ds(c1, tm // first, stride=first), :])
        for c in range(dilation):
            c1, c2 = c % first, c // first
            for tile in range(n_tiles):
                if second > 1:
                    rows = stage2_ref[tile, pl.ds(c1 * (tm // first) + c2, tm // dilation, stride=second), :]
                else:
                    rows = stage_ref[tile, pl.ds(c, tm // dilation, stride=dilation), :]
                o_ref[0, c, :, tile * V7X_LANES:(tile + 1) * V7X_LANES] = rows.astype(o_ref.dtype)


def _norm_proj(x2d, gain, w, layer, n_blocks, cos, sin, *, seq, tm=1024):
    rows, d = x2d.shape
    table_spec = pl.BlockSpec((tm, V7X_LANES), lambda i, j: (i % (seq // tm), 0))
    return pl.pallas_call(
        functools.partial(_proj_kernel, normalise=True, dilation=1),
        grid=(rows // tm, n_blocks),
        in_specs=[
            pl.BlockSpec((tm, d), lambda i, j: (i, 0)),
            pl.BlockSpec((1, d), lambda i, j: (0, 0)),
            pl.BlockSpec((None, d, QKV_WIDTH), lambda i, j: (layer, 0, j)),
            table_spec,
            table_spec,
        ],
        out_specs=[pl.BlockSpec((tm, QKV_WIDTH), lambda i, j: (i, j)), pl.BlockSpec((tm, d), lambda i, j: (i, 0))],
        out_shape=[jax.ShapeDtypeStruct((rows, n_blocks * QKV_WIDTH), BF16), jax.ShapeDtypeStruct((rows, d), BF16)],
        compiler_params=_params("parallel", "arbitrary"),
        name="norm_proj",
    )(x2d, gain.reshape(1, d), w, cos, sin)


def _dilated_proj(h2d, w, layer, block, dilation, cos, sin, *, batch, seq, tm=1024):
    rows, d = h2d.shape
    blocks_per_seq = seq // tm
    table_spec = pl.BlockSpec((tm, V7X_LANES), lambda i, j: (i % blocks_per_seq, 0))
    return pl.pallas_call(
        functools.partial(_proj_kernel, normalise=False, dilation=dilation),
        grid=(rows // tm, 1),
        in_specs=[
            pl.BlockSpec((tm, d), lambda i, j: (i, 0)),
            pl.BlockSpec((None, d, QKV_WIDTH), lambda i, j: (layer, 0, block), pipeline_mode=pl.Buffered(1)),
            table_spec,
            table_spec,
        ],
        out_specs=pl.BlockSpec((1, dilation, tm // dilation, QKV_WIDTH),
                               lambda i, j: (i // blocks_per_seq, 0, i % blocks_per_seq, 0)),
        out_shape=jax.ShapeDtypeStruct((batch, dilation, seq // dilation, QKV_WIDTH), BF16),
        scratch_shapes=[pltpu.VMEM((QKV_WIDTH // V7X_LANES, tm, V7X_LANES), F32)]
        * (2 if dilation > MAX_SUBLANE_STRIDE else 1),
        compiler_params=_params("parallel", "arbitrary"),
        name=f"dilated_proj{dilation}",
    )(h2d, w, cos, sin)


HEADS_PER_STEP = 4
HALF_DIM = HEAD_DIM // 2
QUAD_WIDTH = HEADS_PER_STEP * HEAD_DIM


def _attn_group(q_lo_ref, q_hi_ref, k_lo_ref, k_hi_ref, v_ref, bias_ref, acc_ref, m_ref, lsw_ref, g, r, t_len, nside):
    qb = 2 * nside
    kw = min(t_len, qb + 2 * nside)
    nb = t_len // qb
    lane = lax.broadcasted_iota(jnp.int32, (1, V7X_LANES), 1)
    lane_lo = lane < HEAD_DIM
    head_lanes = [jnp.logical_and(lane >= h * HALF_DIM, lane < (h + 1) * HALF_DIM) for h in range(HEADS_PER_STEP)]

    def body(idx, carry):
        c = idx // nb
        q0 = pl.multiple_of((idx % nb) * qb, qb)
        ks = pl.multiple_of(jnp.clip(q0 - nside, 0, t_len - kw), nside)
        q_lo, q_hi = q_lo_ref[0, c, pl.ds(q0, qb), :], q_hi_ref[0, c, pl.ds(q0, qb), :]
        k2 = jnp.concatenate([k_lo_ref[0, c, pl.ds(ks, kw), :], k_hi_ref[0, c, pl.ds(ks, kw), :]], axis=1)
        zero = jnp.zeros_like(q_lo)
        qs = jnp.concatenate([jnp.concatenate([jnp.where(hl, q_lo, zero), jnp.where(hl, q_hi, zero)], axis=1)
                              for hl in head_lanes], axis=0)
        s = lax.dot_general(qs, k2, (((1,), (1,)), ((), ())), preferred_element_type=F32)
        bias = bias_ref[(q0 - ks) // nside, :, :kw]
        s = (s.reshape(HEADS_PER_STEP, qb, kw) + bias[None]).reshape(HEADS_PER_STEP * qb, kw)
        m = jnp.max(s, axis=-1, keepdims=True)
        p = jnp.exp(s - m).astype(BF16)
        rows = pl.ds(q0 * r + c, qb, stride=r) if r > 1 else pl.ds(q0, qb)
        for pair in range(HEADS_PER_STEP // 2):
            cols = slice(pair * V7X_LANES, (pair + 1) * V7X_LANES)
            v = v_ref[0, c, pl.ds(ks, kw), cols]
            one = jnp.ones_like(v)
            pv_a = jnp.dot(p[(2 * pair) * qb:(2 * pair + 1) * qb], jnp.where(lane_lo, v, one), preferred_element_type=F32)
            pv_b = jnp.dot(p[(2 * pair + 1) * qb:(2 * pair + 2) * qb], jnp.where(lane_lo, one, v), preferred_element_type=F32)
            slot = g * (HEADS_PER_STEP // 2) + pair
            acc_ref[slot, rows, :] = jnp.where(lane_lo, pv_a, pv_b)
            lsw_ref[slot, rows, :] = jnp.where(lane_lo, pv_b, pv_a)
            m_ref[slot, rows, :] = jnp.where(lane_lo, m[(2 * pair) * qb:(2 * pair + 1) * qb],
                                             m[(2 * pair + 1) * qb:(2 * pair + 2) * qb])
        return carry

    lax.fori_loop(0, r * nb, body, 0, unroll=8)


def _attention_kernel(*refs, seq, chunk):
    qkv = refs[:5 * N_GROUPS]
    bias_ref, o_ref, acc_ref, m_ref, lsw_ref = refs[5 * N_GROUPS:]
    for g, (window, r) in enumerate(DILATED_GROUPS):
        _attn_group(*qkv[5 * g:5 * g + 5], bias_ref, acc_ref, m_ref, lsw_ref, g, r, seq // r, window // (2 * r))

    def combine(i, carry):
        rows = pl.ds(pl.multiple_of(i * chunk, chunk), chunk)
        for pair in range(HEADS_PER_STEP // 2):
            cols = slice(pair * V7X_LANES, (pair + 1) * V7X_LANES)
            slots = [g * (HEADS_PER_STEP // 2) + pair for g in range(N_GROUPS)]
            ms = [m_ref[slot, rows, :] for slot in slots]
            top = functools.reduce(jnp.maximum, ms)
            ws = [jnp.exp(m - top) for m in ms]
            num = sum(w * acc_ref[slot, rows, :] for slot, w in zip(slots, ws))
            den = sum(w * pltpu.roll(lsw_ref[slot, rows, :], HEAD_DIM, 1) for slot, w in zip(slots, ws))
            o_ref[0, rows, cols] = (num / den).astype(o_ref.dtype)
        return carry

    lax.fori_loop(0, seq // chunk, combine, 0)


def _band_bias(nside):
    u = jnp.arange(2 * nside)[None, :, None]
    j = jnp.arange(4 * nside)[None, None, :]
    off = (jnp.arange(3) * nside)[:, None, None]
    return jnp.where(jnp.abs(j - u - off) <= nside, 0.0, MASK_VALUE).astype(F32)


def _attention(group_qkv, seq):
    b = group_qkv[0].shape[0]
    n_quads = GROUP_WIDTH // QUAD_WIDTH
    tiles_per_half = GROUP_WIDTH // (2 * V7X_LANES)
    nside = DILATED_GROUPS[0][0] // (2 * DILATED_GROUPS[0][1])
    assert all(w // (2 * r) == nside for w, r in DILATED_GROUPS)
    args, specs = [], []
    for src, (_, r) in zip(group_qkv, DILATED_GROUPS):
        assert src.shape[:3] == (b, r, seq // r)
        for part in range(3):
            t0 = part * GROUP_WIDTH // V7X_LANES
            if part < 2:
                for half in range(2):
                    args.append(src)
                    specs.append(pl.BlockSpec((1, r, seq // r, V7X_LANES),
                                              lambda i, hq, t=t0 + half * tiles_per_half: (i, 0, 0, t + hq)))
            else:
                args.append(src)
                specs.append(pl.BlockSpec((1, r, seq // r, QUAD_WIDTH),
                                          lambda i, hq, t=t0 * V7X_LANES // QUAD_WIDTH: (i, 0, 0, t + hq)))
    args.append(_band_bias(nside))
    specs.append(pl.BlockSpec((3, 2 * nside, 4 * nside), lambda i, hq: (0, 0, 0)))
    return pl.pallas_call(
        functools.partial(_attention_kernel, seq=seq, chunk=256),
        grid=(b, n_quads),
        in_specs=specs,
        out_specs=pl.BlockSpec((1, seq, QUAD_WIDTH), lambda i, hq: (i, 0, hq)),
        out_shape=jax.ShapeDtypeStruct((b, seq, GROUP_WIDTH), BF16),
        scratch_shapes=[pltpu.VMEM((N_GROUPS * HEADS_PER_STEP // 2, seq, V7X_LANES), F32)] * 3,
        compiler_params=_params("parallel", "parallel"),
        name="attention",
    )(*args)


def _filter_kernel(feats_ref, t_ref, w1_ref, b1_ref, freq_ref, w2_ref, b2_ref, w3f_ref, w3b_ref, delta_ref, sgn_ref,
                   cos_hi_ref, cos_lo_ref, sin_hi_ref, sin_lo_ref, kr_ref, ki_ref, sum_ref, dif_ref, nyq_ref, hid_ref):
    fb = pl.program_id(2)
    tc = kr_ref.shape[2]

    dot = functools.partial(jnp.dot, precision=HIGHEST, preferred_element_type=F32)

    @pl.when(jnp.logical_and(fb == 0, jnp.logical_and(pl.program_id(0) == 0, pl.program_id(1) == 0)))
    def _():
        h = jnp.sin(freq_ref[0:1, :] * (dot(feats_ref[...], w1_ref[...]) + b1_ref[...]))
        hid_ref[...] = jnp.sin(freq_ref[1:2, :] * (dot(h, w2_ref[...]) + b2_ref[...]))

    @pl.when(fb == 0)
    def _():
        h = hid_ref[...]
        decay = jnp.exp(-t_ref[...] * jnp.abs(delta_ref[...]))
        fwd = dot(h, w3f_ref[...]) * decay
        bwd = dot(h, w3b_ref[...]) * decay
        row = lax.broadcasted_iota(jnp.int32, (fwd.shape[0], 1), 0)
        bwd = jnp.where(row == 0, 0.0, bwd)
        norm = jnp.sum(jnp.abs(fwd), axis=0, keepdims=True) + jnp.sum(jnp.abs(bwd), axis=0, keepdims=True)
        fwd = fwd / norm
        bwd = bwd / norm
        nyq_ref[...] = jnp.sum(sgn_ref[...] * (fwd + bwd), axis=0, keepdims=True)
        for ref, val in ((sum_ref, fwd + bwd), (dif_ref, fwd - bwd)):
            hi = val.astype(BF16)
            ref[:, :tc] = hi
            ref[:, tc:] = (val - hi.astype(F32)).astype(BF16)

    def dft(m_hi_ref, m_lo_ref, v_ref):
        hi_terms = jnp.dot(m_hi_ref[...], v_ref[...], preferred_element_type=F32)
        return hi_terms[:, :tc] + hi_terms[:, tc:] + jnp.dot(m_lo_ref[...], v_ref[:, :tc], preferred_element_type=F32)

    kr = dft(cos_hi_ref, cos_lo_ref, sum_ref)
    ki = dft(sin_hi_ref, sin_lo_ref, dif_ref)
    row0 = jnp.logical_and(lax.broadcasted_iota(jnp.int32, (kr.shape[0], 1), 0) == 0, fb == 0)
    kr_ref[0] = jnp.where(row0, 0.5 * kr, kr)
    ki_ref[0] = jnp.where(row0, nyq_ref[...], ki)


def _filter_spectrum(feats, t, w1, b1, freq, w2, b2, w3, deltas, sgn, dft_hi_lo, *, tc=256, tf=512):
    length = feats.shape[0]
    hid = w2.shape[0]
    chans = deltas.shape[1]
    ncb = chans // tc
    const = lambda o, cb, fb: (0, 0)
    out = jax.ShapeDtypeStruct((HYENA_ORDER, length, chans), F32)
    return pl.pallas_call(
        _filter_kernel,
        grid=(HYENA_ORDER, ncb, length // tf),
        in_specs=[
            pl.BlockSpec(feats.shape, const),
            pl.BlockSpec((length, 1), const),
            pl.BlockSpec(w1.shape, const),
            pl.BlockSpec((1, hid), const),
            pl.BlockSpec((2, hid), const),
            pl.BlockSpec((hid, hid), const),
            pl.BlockSpec((1, hid), const),
            pl.BlockSpec((hid, tc), lambda o, cb, fb: (0, (2 * o) * ncb + cb)),
            pl.BlockSpec((hid, tc), lambda o, cb, fb: (0, (2 * o + 1) * ncb + cb)),
            pl.BlockSpec((1, tc), lambda o, cb, fb: (0, cb)),
            pl.BlockSpec((length, 1), const),
        ] + [pl.BlockSpec((tf, length), lambda o, cb, fb: (fb, 0))] * 4,
        out_specs=[pl.BlockSpec((1, tf, tc), lambda o, cb, fb: (o, fb, cb))] * 2,
        out_shape=[out, out],
        scratch_shapes=[pltpu.VMEM((length, 2 * tc), BF16)] * 2 + [pltpu.VMEM((1, tc), F32), pltpu.VMEM((length, hid), F32)],
        compiler_params=_params("arbitrary", "arbitrary", "arbitrary"),
        name="hyena_filter",
    )(feats, t, w1, b1.reshape(1, hid), freq, w2, b2.reshape(1, hid), w3, w3, deltas, sgn, *dft_hi_lo)


DFT_ROWS = 512


def _reverse_rows(flip_ref, x):
    h = x.shape[0]
    return jnp.concatenate([jnp.dot(flip_ref[c * DFT_ROWS:(c + 1) * DFT_ROWS, :], x, preferred_element_type=F32)
                            for c in range(h // DFT_ROWS)], axis=0)


def _fold(u_ref, flip_ref):
    h = u_ref.shape[1] // 2
    return u_ref[0, 0:h, :].astype(F32), pltpu.roll(_reverse_rows(flip_ref, u_ref[0, h:, :]), 1, 0)


def _dwconv3_folded(a, b, w_ref, b_ref):
    h = a.shape[0]
    row = lax.broadcasted_iota(jnp.int32, (h, 1), 0)
    a_prev = jnp.where(row == 0, 0.0, pltpu.roll(a, 1, 0))
    a_next = jnp.where(row == h - 1, b[0:1, :], pltpu.roll(a, h - 1, 0))
    b_prev = jnp.where(row == 0, a[h - 1:h, :], pltpu.roll(b, h - 1, 0))
    b_next = jnp.where(row == 1, 0.0, pltpu.roll(b, 1, 0))
    w0, w1, w2, bias = w_ref[0:1, :], w_ref[1:2, :], w_ref[2:3, :], b_ref[...]
    return a_prev * w0 + a * w1 + a_next * w2 + bias, b_prev * w0 + b * w1 + b_next * w2 + bias


def _long_conv_kernel(*refs, z_from_u):
    if z_from_u:
        (uz_ref, cwz_ref, cbz_ref, ug_ref, cwg_ref, cbg_ref, kr_ref, ki_ref, skip_ref, sgn_ref, flip_ref,
         c1_ref, s1_ref, c2_ref, s2_ref, c2t_ref, s2t_ref, o_ref, yre_ref, yie_ref, yro_ref, yio_ref) = refs
        za, zb = _dwconv3_folded(*_fold(uz_ref, flip_ref), cwz_ref, cbz_ref)
    else:
        (z_ref, ug_ref, cwg_ref, cbg_ref, kr_ref, ki_ref, skip_ref, sgn_ref, flip_ref,
         c1_ref, s1_ref, c2_ref, s2_ref, c2t_ref, s2t_ref, o_ref, yre_ref, yie_ref, yro_ref, yio_ref, ob_ref) = refs
        za, zb = z_ref[0, 0], z_ref[0, 1]
    ga, gb = _dwconv3_folded(*_fold(ug_ref, flip_ref), cwg_ref, cbg_ref)
    h = za.shape[0]
    first = lax.broadcasted_iota(jnp.int32, (h, 1), 0) == 0
    sgn = sgn_ref[...]
    p = jnp.where(first, za, za + zb).astype(BF16)
    d = jnp.where(first, za, za - zb).astype(BF16)
    b0 = zb[0:1, :]
    xnyq = jnp.sum(sgn * (za + zb), axis=0, keepdims=True)
    ynyq_half = 0.5 * (xnyq * ki_ref[0, 0:1, :])
    skip = skip_ref[0]
    mid = jnp.zeros_like(b0)
    for c in range(h // DFT_ROWS):
        rows = slice(c * DFT_ROWS, (c + 1) * DFT_ROWS)
        odd_rows = slice(h + c * DFT_ROWS, h + (c + 1) * DFT_ROWS)
        edge = sgn[rows, :] * b0
        xr_e = jnp.dot(c1_ref[rows, :], p, preferred_element_type=F32) + edge
        xi_e = jnp.dot(s1_ref[rows, :], d, preferred_element_type=F32)
        xr_o = jnp.dot(c2_ref[rows, :], d, preferred_element_type=F32)
        xi_o = jnp.dot(s2_ref[rows, :], p, preferred_element_type=F32) - edge
        kr_e, ki_e, kr_o, ki_o = kr_ref[0, rows, :], ki_ref[0, rows, :], kr_ref[0, odd_rows, :], ki_ref[0, odd_rows, :]
        yr_e = xr_e * kr_e - xi_e * ki_e
        yi_o = xr_o * ki_o + xi_o * kr_o
        mid = mid + jnp.sum(sgn[rows, :] * (yr_e - yi_o), axis=0, keepdims=True)
        yre_ref[rows, :] = yr_e.astype(BF16)
        yie_ref[rows, :] = (xr_e * ki_e + xi_e * kr_e).astype(BF16)
        yro_ref[rows, :] = (xr_o * kr_o - xi_o * ki_o).astype(BF16)
        yio_ref[rows, :] = yi_o.astype(BF16)
    inv_n = 1.0 / (2 * h)
    for c in range(h // DFT_ROWS):
        rows = slice(c * DFT_ROWS, (c + 1) * DFT_ROWS)
        pp = (jnp.dot(c1_ref[rows, :], yre_ref[...], preferred_element_type=F32)
              + jnp.dot(s2t_ref[rows, :], yio_ref[...], preferred_element_type=F32)) + sgn[rows, :] * ynyq_half
        qq = (jnp.dot(s1_ref[rows, :], yie_ref[...], preferred_element_type=F32)
              + jnp.dot(c2t_ref[rows, :], yro_ref[...], preferred_element_type=F32))
        ya = (pp + qq) * inv_n
        yb = jnp.where(first[rows, :], mid + ynyq_half, pp - qq) * inv_n
        out_a = ga[rows, :] * (ya + za[rows, :] * skip)
        out_b = gb[rows, :] * (yb + zb[rows, :] * skip)
        if z_from_u:
            o_ref[0, 0, rows, :] = out_a
            o_ref[0, 1, rows, :] = out_b
        else:
            o_ref[0, rows, :] = out_a.astype(o_ref.dtype)
            ob_ref[rows, :] = out_b
    if not z_from_u:
        o_ref[0, h:, :] = _reverse_rows(flip_ref, pltpu.roll(ob_ref[...], h - 1, 0).astype(BF16)).astype(o_ref.dtype)


def _long_conv(z_src, u3d, conv_w, conv_b, kr, ki, skip, sgn, dft, *, order, z_part, gate_part, u_col0, tc=256):
    b, seq, _ = u3d.shape
    half = seq // 2
    chans = kr.shape[2]
    ncb = chans // tc
    u_cb0 = u_col0 // tc
    first_order = z_src is None

    def u_specs(part):
        return [
            pl.BlockSpec((1, seq, tc), lambda cb, i: (i, 0, u_cb0 + part * ncb + cb)),
            pl.BlockSpec((3, tc), lambda cb, i: (0, part * ncb + cb)),
            pl.BlockSpec((1, tc), lambda cb, i: (0, part * ncb + cb)),
        ]

    def dft_block(row_block):
        return pl.BlockSpec((half, half), lambda cb, i: (row_block, 0), pipeline_mode=pl.Buffered(1))

    folded_spec = pl.BlockSpec((1, 2, half, tc), lambda cb, i: (i, 0, 0, cb))
    cosm, sinm, cos_odd_t, sin_odd_t, flip = dft
    conv_b2 = conv_b.reshape(1, -1)
    if first_order:
        args, specs = [u3d, conv_w, conv_b2], u_specs(z_part)
    else:
        args, specs = [z_src], [folded_spec]
    args += [u3d, conv_w, conv_b2, kr, ki, skip.reshape(HYENA_ORDER, 1, chans), sgn, flip,
             cosm, sinm, cosm, sinm, cos_odd_t, sin_odd_t]
    specs += u_specs(gate_part) + [
        pl.BlockSpec((1, seq, tc), lambda cb, i: (order, 0, cb)),
        pl.BlockSpec((1, seq, tc), lambda cb, i: (order, 0, cb)),
        pl.BlockSpec((1, 1, tc), lambda cb, i: (order, 0, cb)),
        pl.BlockSpec((half, 1), lambda cb, i: (0, 0)),
        dft_block(0), dft_block(0), dft_block(0), dft_block(1), dft_block(1), dft_block(0), dft_block(0),
    ]
    return pl.pallas_call(
        functools.partial(_long_conv_kernel, z_from_u=first_order),
        grid=(ncb, b),
        in_specs=specs,
        out_specs=folded_spec if first_order else pl.BlockSpec((1, seq, tc), lambda cb, i: (i, 0, cb)),
        out_shape=jax.ShapeDtypeStruct((b, 2, half, chans), F32) if first_order
        else jax.ShapeDtypeStruct((b, seq, chans), BF16),
        scratch_shapes=[pltpu.VMEM((half, tc), BF16)] * 4 + ([] if first_order else [pltpu.VMEM((half, tc), F32)]),
        compiler_params=_params("parallel", "parallel"),
        name=f"long_conv{order}",
    )(*args)


def _merge_kernel(x_ref, h_ref, oa_ref, oh_ref, wga_ref, wgh_ref, bga_ref, bgh_ref, wpa_ref, wph_ref, wo_ref, o_ref):
    c = pl.program_id(1)

    @pl.when(c == 0)
    def _():
        o_ref[...] = x_ref[...]

    h = h_ref[...]
    gate_a = 1.0 / (1.0 + jnp.exp(-(jnp.dot(h, wga_ref[...], preferred_element_type=F32) + bga_ref[...])))
    gate_h = 1.0 / (1.0 + jnp.exp(-(jnp.dot(h, wgh_ref[...], preferred_element_type=F32) + bgh_ref[...])))
    pa = jnp.dot(oa_ref[...], wpa_ref[...], preferred_element_type=F32)
    ph = jnp.dot(oh_ref[...], wph_ref[...], preferred_element_type=F32)
    mixed = (gate_a * pa + gate_h * ph).astype(BF16)
    o_ref[...] += jnp.dot(mixed, wo_ref[...], preferred_element_type=F32)


def _merge(x2d, h2d, o_attn, o_hy, w_gate, b_gate, w_pa, w_ph, w_out, layer, *, tm=512, tc=512):
    rows, d = x2d.shape
    ncb = d // tc
    b_gate2 = b_gate.reshape(1, -1)
    return pl.pallas_call(
        _merge_kernel,
        grid=(rows // tm, ncb),
        in_specs=[
            pl.BlockSpec((tm, d), lambda i, c: (i, 0)),
            pl.BlockSpec((tm, d), lambda i, c: (i, 0)),
            pl.BlockSpec((tm, o_attn.shape[1]), lambda i, c: (i, 0)),
            pl.BlockSpec((tm, o_hy.shape[1]), lambda i, c: (i, 0)),
            pl.BlockSpec((None, d, tc), lambda i, c: (layer, 0, c)),
            pl.BlockSpec((None, d, tc), lambda i, c: (layer, 0, ncb + c)),
            pl.BlockSpec((1, tc), lambda i, c: (0, c)),
            pl.BlockSpec((1, tc), lambda i, c: (0, ncb + c)),
            pl.BlockSpec((None, w_pa.shape[1], tc), lambda i, c: (layer, 0, c)),
            pl.BlockSpec((None, w_ph.shape[1], tc), lambda i, c: (layer, 0, c)),
            pl.BlockSpec((None, tc, d), lambda i, c: (layer, c, 0)),
        ],
        out_specs=pl.BlockSpec((tm, d), lambda i, c: (i, 0)),
        out_shape=jax.ShapeDtypeStruct((rows, d), F32),
        compiler_params=_params("parallel", "arbitrary"),
        name="merge",
    )(x2d, h2d, o_attn, o_hy, w_gate, w_gate, b_gate2, b_gate2, w_pa, w_ph, w_out)


HALO = 16


def _conv_ffn_kernel(*refs, blocks_per_seq, final_norm):
    if final_norm:
        x_ref, xp_ref, xn_ref, g_ref, wa_ref, wb_ref, cw_ref, cb_ref, wd_ref, gf_ref, o_ref, h_ref = refs
    else:
        x_ref, xp_ref, xn_ref, g_ref, wa_ref, wb_ref, cw_ref, cb_ref, wd_ref, o_ref, h_ref = refs
    i, f = pl.program_id(0), pl.program_id(1)
    tm = x_ref.shape[0]

    @pl.when(f == 0)
    def _():
        x = x_ref[...]
        g = g_ref[...]
        h_ref[0:HALO, :] = _rmsnorm(xp_ref[...], g).astype(BF16)
        h_ref[HALO:HALO + tm, :] = _rmsnorm(x, g).astype(BF16)
        h_ref[HALO + tm:, :] = _rmsnorm(xn_ref[...], g).astype(BF16)
        o_ref[...] = x

    up_a = jnp.dot(h_ref[...], wa_ref[...], preferred_element_type=F32)
    up_b = jnp.dot(h_ref[HALO:HALO + tm, :], wb_ref[...], preferred_element_type=F32)
    n_ext = up_a.shape[0]
    row = lax.broadcasted_iota(jnp.int32, (tm, 1), 0)
    seq_pos = i % blocks_per_seq
    at_start = jnp.logical_and(row == 0, seq_pos == 0)
    at_end = jnp.logical_and(row == tm - 1, seq_pos == blocks_per_seq - 1)
    a_prev = jnp.where(at_start, 0.0, pltpu.roll(up_a, 1, 0)[HALO:HALO + tm])
    a_next = jnp.where(at_end, 0.0, pltpu.roll(up_a, n_ext - 1, 0)[HALO:HALO + tm])
    a = a_prev * cw_ref[0:1, :] + up_a[HALO:HALO + tm] * cw_ref[1:2, :] + a_next * cw_ref[2:3, :] + cb_ref[...]
    gelu = 0.5 * a * (1.0 + jnp.tanh(math.sqrt(2.0 / math.pi) * (a + 0.044715 * (a * a * a))))
    o_ref[...] += jnp.dot((gelu * up_b).astype(BF16), wd_ref[...], preferred_element_type=F32)

    if final_norm:
        @pl.when(f == pl.num_programs(1) - 1)
        def _():
            o_ref[...] = _rmsnorm(o_ref[...], gf_ref[...])


def _conv_ffn(x2d, gain, w_up, conv_w, conv_b, w_down, layer, final_gain, *, seq, tm=1024, tf=512):
    rows, d = x2d.shape
    d_ff = w_down.shape[1]
    nfb = d_ff // tf
    halo_per_block = tm // HALO
    n_halo_blocks = rows // HALO
    args = [x2d, x2d, x2d, gain.reshape(1, d), w_up, w_up, conv_w, conv_b.reshape(1, d_ff), w_down]
    specs = [
        pl.BlockSpec((tm, d), lambda i, f: (i, 0)),
        pl.BlockSpec((HALO, d), lambda i, f: (jnp.maximum(i * halo_per_block - 1, 0), 0)),
        pl.BlockSpec((HALO, d), lambda i, f: (jnp.minimum((i + 1) * halo_per_block, n_halo_blocks - 1), 0)),
        pl.BlockSpec((1, d), lambda i, f: (0, 0)),
        pl.BlockSpec((None, d, tf), lambda i, f: (layer, 0, f)),
        pl.BlockSpec((None, d, tf), lambda i, f: (layer, 0, nfb + f)),
        pl.BlockSpec((3, tf), lambda i, f: (0, f)),
        pl.BlockSpec((1, tf), lambda i, f: (0, f)),
        pl.BlockSpec((None, tf, d), lambda i, f: (layer, f, 0)),
    ]
    if final_gain is not None:
        args.append(final_gain.reshape(1, d))
        specs.append(pl.BlockSpec((1, d), lambda i, f: (0, 0)))
    return pl.pallas_call(
        functools.partial(_conv_ffn_kernel, blocks_per_seq=seq // tm, final_norm=final_gain is not None),
        grid=(rows // tm, nfb),
        in_specs=specs,
        out_specs=pl.BlockSpec((tm, d), lambda i, f: (i, 0)),
        out_shape=jax.ShapeDtypeStruct((rows, d), F32),
        scratch_shapes=[pltpu.VMEM((tm + 2 * HALO, d), BF16)],
        compiler_params=_params("parallel", "arbitrary"),
        name="conv_ffn",
    )(*args)


def _rope_tables(seq):
    pos = jnp.arange(seq, dtype=F32)
    inv = 1.0 / (ROPE_THETA ** (jnp.arange(0, HEAD_DIM, 2, dtype=F32) / HEAD_DIM))
    ang = pos[:, None] * inv[None, :]
    ang = jnp.concatenate([ang] * (V7X_LANES // HALF_DIM), axis=-1)
    return jnp.cos(ang), jnp.sin(ang)


def _arrange_w_in(w_in):
    layers, d, _ = w_in.shape
    qk = w_in[:, :, :2 * ATTN_WIDTH].reshape(layers, d, 2, N_GROUPS, HEADS_PER_GROUP, 2, HALF_DIM)
    qk = qk.transpose(0, 1, 2, 3, 5, 4, 6).reshape(layers, d, 2, N_GROUPS, GROUP_WIDTH)
    v = w_in[:, :, 2 * ATTN_WIDTH:3 * ATTN_WIDTH].reshape(layers, d, N_GROUPS, GROUP_WIDTH)
    groups = [jnp.concatenate([qk[:, :, 0, g], qk[:, :, 1, g], v[:, :, g]], axis=-1) for g in range(N_GROUPS)]
    return jnp.concatenate([groups[0], w_in[:, :, 3 * ATTN_WIDTH:]] + groups[1:], axis=-1)


def _hyena_tables(length, chans):
    t = jnp.linspace(0.0, 1.0, length, dtype=F32)[:, None]
    bands = (HYENA_EMB_DIM - 1) // 2
    w = 2.0 * math.pi * jnp.arange(length, dtype=F32)[:, None] / length
    f = jnp.linspace(1e-4, bands - 1, bands, dtype=F32)[None, :]
    feats = jnp.concatenate([t, jnp.cos(f * w), -jnp.sin(f * w)], axis=-1)
    feats = jnp.pad(feats, ((0, 0), (0, V7X_LANES - HYENA_EMB_DIM)))
    max_decay = math.log(HYENA_DECAY_TARGET) / HYENA_FAST_DECAY
    min_decay = math.log(HYENA_DECAY_TARGET) / HYENA_SLOW_DECAY
    deltas = jnp.linspace(min_decay, max_decay, chans, dtype=F32)[None, :]
    idx = jnp.arange(length, dtype=jnp.int32)
    freq = jnp.concatenate([idx[0::2], idx[1::2]])
    ang = ((freq[:, None] * idx[None, :]) % (2 * length)).astype(F32) * (math.pi / length)
    sgn = (1 - 2 * (idx % 2)).astype(F32)[:, None]
    return feats, t, deltas, sgn, jnp.cos(ang), -jnp.sin(ang)


def kernel(x, attn_norm, w_in, hy_conv_w, hy_conv_b, f_w1, f_b1, f_freq, f_w2, f_b2, f_w3, hy_skip, w_proj_attn, w_proj_hyena, w_gate, b_gate, w_out, ffn_norm, w_up, ffn_conv_w, ffn_conv_b, w_down, final_norm):
    b, seq, d = x.shape
    depth = w_in.shape[0]
    chans = hy_skip.shape[2]
    cos, sin = _rope_tables(seq)
    feats, t, deltas, sgn, cosm, sinm = _hyena_tables(seq, chans)
    cosm16, sinm16 = cosm.astype(BF16), sinm.astype(BF16)
    dft_hi_lo = (cosm16, (cosm - cosm16.astype(F32)).astype(BF16), sinm16, (sinm - sinm16.astype(F32)).astype(BF16))
    half = seq // 2
    flip = (jnp.arange(half)[:, None] + jnp.arange(half)[None, :] == half - 1).astype(BF16)
    dft16 = (cosm16, sinm16, cosm16[half:, :half].T, sinm16[half:, :half].T, flip)
    w1_pad = jnp.pad(f_w1, ((0, 0), (0, V7X_LANES - HYENA_EMB_DIM), (0, 0)))
    w_in16 = _arrange_w_in(w_in).astype(BF16)
    w_gate16, w_pa16, w_ph16, w_out16, w_up16, w_down16 = (
        w.astype(BF16) for w in (w_gate, w_proj_attn, w_proj_hyena, w_out, w_up, w_down))
    n_nat_blocks = (w_in.shape[2] - 2 * QKV_WIDTH) // QKV_WIDTH

    x2d = x.reshape(b * seq, d)
    for l in range(depth):
        nat, h2d = _norm_proj(x2d, attn_norm[l], w_in16, l, n_nat_blocks, cos, sin, seq=seq)
        nat3d = nat.reshape(b, seq, -1)
        group_qkv = [nat3d.reshape(b, 1, seq, -1)] + [
            _dilated_proj(h2d, w_in16, l, n_nat_blocks + g - 1, r, cos, sin, batch=b, seq=seq)
            for g, (_, r) in enumerate(DILATED_GROUPS) if g > 0]
        o_attn = _attention(group_qkv, seq)
        kr, ki = _filter_spectrum(feats, t, w1_pad[l], f_b1[l], f_freq[l], f_w2[l], f_b2[l], f_w3[l], deltas, sgn, dft_hi_lo)
        conv = functools.partial(_long_conv, u3d=nat3d, conv_w=hy_conv_w[l], conv_b=hy_conv_b[l], kr=kr, ki=ki,
                                 skip=hy_skip[l], sgn=sgn[:half], dft=dft16, u_col0=QKV_WIDTH)
        z1 = conv(None, order=0, z_part=0, gate_part=1)
        o_hy = conv(z1, order=1, z_part=None, gate_part=2)
        x2d = _merge(x2d, h2d, o_attn.reshape(b * seq, -1), o_hy.reshape(b * seq, -1),
                     w_gate16, b_gate[l], w_pa16, w_ph16, w_out16, l)
        x2d = _conv_ffn(x2d, ffn_norm[l], w_up16, ffn_conv_w[l], ffn_conv_b[l], w_down16, l,
                        final_norm if l == depth - 1 else None, seq=seq)
    return x2d.reshape(b, seq, d)
```

```python
import functools
import math

import jax
import jax.numpy as jnp
from jax import lax
from jax.experimental import pallas as pl
from jax.experimental.pallas import tpu as pltpu

HEAD_DIM = 64
HEADS_PER_GROUP = 8
DILATED_GROUPS = ((128, 1), (512, 4), (2048, 16))
N_GROUPS = len(DILATED_GROUPS)
GROUP_WIDTH = HEADS_PER_GROUP * HEAD_DIM
ATTN_WIDTH = N_GROUPS * GROUP_WIDTH
ROPE_THETA = 10000.0
HYENA_ORDER = 2
HYENA_EMB_DIM = 33
HYENA_DECAY_TARGET = 1e-2
HYENA_FAST_DECAY = 0.3
HYENA_SLOW_DECAY = 1.5
RMS_EPS = 1e-6
MASK_VALUE = -1e30

V7X_LANES = 128
V7X_VMEM_LIMIT_BYTES = 60 * 1024 * 1024

F32 = jnp.float32
BF16 = jnp.bfloat16
HIGHEST = lax.Precision.HIGHEST


def _params(*semantics):
    return pltpu.CompilerParams(dimension_semantics=semantics, vmem_limit_bytes=V7X_VMEM_LIMIT_BYTES)


def _rmsnorm(x, g):
    return x * lax.rsqrt(jnp.mean(x * x, axis=-1, keepdims=True) + RMS_EPS) * g


QKV_WIDTH = 3 * GROUP_WIDTH
Q_SCALE = math.log2(math.e) / math.sqrt(HEAD_DIM)
MAX_SUBLANE_STRIDE = 4


def _proj_kernel(*refs, normalise, dilation):
    if normalise:
        x_ref, g_ref, w_ref, cos_ref, sin_ref, o_ref, h_ref = refs
    elif dilation > MAX_SUBLANE_STRIDE:
        h_ref, w_ref, cos_ref, sin_ref, o_ref, stage_ref, stage2_ref = refs
    elif dilation > 1:
        h_ref, w_ref, cos_ref, sin_ref, o_ref, stage_ref = refs
    else:
        h_ref, w_ref, cos_ref, sin_ref, o_ref = refs
    j = pl.program_id(1)

    if normalise:
        @pl.when(j == 0)
        def _():
            h_ref[...] = _rmsnorm(x_ref[...], g_ref[...]).astype(BF16)

    acc = jnp.dot(h_ref[...], w_ref[...], preferred_element_type=F32)
    tm = acc.shape[0]

    def put(tile, val):
        if dilation > 1:
            stage_ref[tile] = val
        else:
            o_ref[:, tile * V7X_LANES:(tile + 1) * V7X_LANES] = val.astype(o_ref.dtype)

    is_qkv = j == 0
    cos, sin = cos_ref[...], sin_ref[...]
    tiles_per_group = GROUP_WIDTH // V7X_LANES
    for part, scale in ((0, Q_SCALE), (1, 1.0)):
        cos_p = jnp.where(is_qkv, cos * scale, 1.0)
        sin_p = jnp.where(is_qkv, sin * scale, 0.0)
        for s in range(tiles_per_group // 2):
            lo_tile = part * tiles_per_group + s
            hi_tile = lo_tile + tiles_per_group // 2
            lo = acc[:, lo_tile * V7X_LANES:(lo_tile + 1) * V7X_LANES]
            hi = acc[:, hi_tile * V7X_LANES:(hi_tile + 1) * V7X_LANES]
            put(lo_tile, lo * cos_p - hi * sin_p)
            put(hi_tile, hi * cos_p + lo * sin_p)
    for tile in range(2 * tiles_per_group, acc.shape[1] // V7X_LANES):
        put(tile, acc[:, tile * V7X_LANES:(tile + 1) * V7X_LANES])
    if dilation > 1:
        n_tiles = acc.shape[1] // V7X_LANES
        first = min(dilation, MAX_SUBLANE_STRIDE)
        second = dilation // first
        if second > 1:
            for c1 in range(first):
                for tile in range(n_tiles):
                    stage2_ref[tile, c1 * (tm // first):(c1 + 1) * (tm // first), :] = (
                        stage_ref[tile, pl.ds(c1, tm // first, stride=first), :])
        for c in range(dilation):
            c1, c2 = c % first, c // first
            for tile in range(n_tiles):
                if second > 1:
                    rows = stage2_ref[tile, pl.ds(c1 * (tm // first) + c2, tm // dilation, stride=second), :]
                else:
                    rows = stage_ref[tile, pl.ds(c, tm // dilation, stride=dilation), :]
                o_ref[0, c, :, tile * V7X_LANES:(tile + 1) * V7X_LANES] = rows.astype(o_ref.dtype)


def _norm_proj(x2d, gain, w, layer, n_blocks, cos, sin, *, seq, tm=1024):
    rows, d = x2d.shape
    table_spec = pl.BlockSpec((tm, V7X_LANES), lambda i, j: (i % (seq // tm), 0))
    return pl.pallas_call(
        functools.partial(_proj_kernel, normalise=True, dilation=1),
        grid=(rows // tm, n_blocks),
        in_specs=[
            pl.BlockSpec((tm, d), lambda i, j: (i, 0)),
            pl.BlockSpec((1, d), lambda i, j: (0, 0)),
            pl.BlockSpec((None, d, QKV_WIDTH), lambda i, j: (layer, 0, j)),
            table_spec,
            table_spec,
        ],
        out_specs=[pl.BlockSpec((tm, QKV_WIDTH), lambda i, j: (i, j)), pl.BlockSpec((tm, d), lambda i, j: (i, 0))],
        out_shape=[jax.ShapeDtypeStruct((rows, n_blocks * QKV_WIDTH), BF16), jax.ShapeDtypeStruct((rows, d), BF16)],
        compiler_params=_params("parallel", "arbitrary"),
        name="norm_proj",
    )(x2d, gain.reshape(1, d), w, cos, sin)


def _dilated_proj(h2d, w, layer, block, dilation, cos, sin, *, batch, seq, tm=1024):
    rows, d = h2d.shape
    blocks_per_seq = seq // tm
    table_spec = pl.BlockSpec((tm, V7X_LANES), lambda i, j: (i % blocks_per_seq, 0))
    return pl.pallas_call(
        functools.partial(_proj_kernel, normalise=False, dilation=dilation),
        grid=(rows // tm, 1),
        in_specs=[
            pl.BlockSpec((tm, d), lambda i, j: (i, 0)),
            pl.BlockSpec((None, d, QKV_WIDTH), lambda i, j: (layer, 0, block), pipeline_mode=pl.Buffered(1)),
            table_spec,
            table_spec,
        ],
        out_specs=pl.BlockSpec((1, dilation, tm // dilation, QKV_WIDTH),
                               lambda i, j: (i // blocks_per_seq, 0, i % blocks_per_seq, 0)),
        out_shape=jax.ShapeDtypeStruct((batch, dilation, seq // dilation, QKV_WIDTH), BF16),
        scratch_shapes=[pltpu.VMEM((QKV_WIDTH // V7X_LANES, tm, V7X_LANES), F32)]
        * (2 if dilation > MAX_SUBLANE_STRIDE else 1),
        compiler_params=_params("parallel", "arbitrary"),
        name=f"dilated_proj{dilation}",
    )(h2d, w, cos, sin)


HEADS_PER_STEP = 4
HALF_DIM = HEAD_DIM // 2
QUAD_WIDTH = HEADS_PER_STEP * HEAD_DIM


def _attn_group(q_lo_ref, q_hi_ref, k_lo_ref, k_hi_ref, v_ref, bias_ref, acc_ref, m_ref, lsw_ref, g, r, t_len, nside):
    qb = 2 * nside
    kw = min(t_len, qb + 2 * nside)
    nb = t_len // qb
    lane = lax.broadcasted_iota(jnp.int32, (1, V7X_LANES), 1)
    lane_lo = lane < HEAD_DIM
    head_lanes = [jnp.logical_and(lane >= h * HALF_DIM, lane < (h + 1) * HALF_DIM) for h in range(HEADS_PER_STEP)]

    def body(idx, carry):
        c = idx // nb
        q0 = pl.multiple_of((idx % nb) * qb, qb)
        ks = pl.multiple_of(jnp.clip(q0 - nside, 0, t_len - kw), nside)
        q_lo, q_hi = q_lo_ref[0, c, pl.ds(q0, qb), :], q_hi_ref[0, c, pl.ds(q0, qb), :]
        k2 = jnp.concatenate([k_lo_ref[0, c, pl.ds(ks, kw), :], k_hi_ref[0, c, pl.ds(ks, kw), :]], axis=1)
        zero = jnp.zeros_like(q_lo)
        qs = jnp.concatenate([jnp.concatenate([jnp.where(hl, q_lo, zero), jnp.where(hl, q_hi, zero)], axis=1)
                              for hl in head_lanes], axis=0)
        s = lax.dot_general(qs, k2, (((1,), (1,)), ((), ())), preferred_element_type=F32)
        bias = bias_ref[(q0 - ks) // nside, :, :kw]
        s = (s.reshape(HEADS_PER_STEP, qb, kw) + bias[None]).reshape(HEADS_PER_STEP * qb, kw)
        m = jnp.max(s, axis=-1, keepdims=True)
        p = jnp.exp2(s - m).astype(BF16)
        rows = pl.ds(q0 * r + c, qb, stride=r) if r > 1 else pl.ds(q0, qb)
        for pair in range(HEADS_PER_STEP // 2):
            cols = slice(pair * V7X_LANES, (pair + 1) * V7X_LANES)
            v = v_ref[0, c, pl.ds(ks, kw), cols]
            one = jnp.ones_like(v)
            pv_a = jnp.dot(p[(2 * pair) * qb:(2 * pair + 1) * qb], jnp.where(lane_lo, v, one), preferred_element_type=F32)
            pv_b = jnp.dot(p[(2 * pair + 1) * qb:(2 * pair + 2) * qb], jnp.where(lane_lo, one, v), preferred_element_type=F32)
            slot = g * (HEADS_PER_STEP // 2) + pair
            acc_ref[slot, rows, :] = jnp.where(lane_lo, pv_a, pv_b)
            lsw_ref[slot, rows, :] = jnp.where(lane_lo, pv_b, pv_a)
            m_ref[slot, rows, :] = jnp.where(lane_lo, m[(2 * pair) * qb:(2 * pair + 1) * qb],
                                             m[(2 * pair + 1) * qb:(2 * pair + 2) * qb])
        return carry

    lax.fori_loop(0, r * nb, body, 0, unroll=8)


def _attention_kernel(*refs, seq, chunk):
    qkv = refs[:5 * N_GROUPS]
    bias_ref, o_ref, acc_ref, m_ref, lsw_ref = refs[5 * N_GROUPS:]
    for g, (window, r) in enumerate(DILATED_GROUPS):
        _attn_group(*qkv[5 * g:5 * g + 5], bias_ref, acc_ref, m_ref, lsw_ref, g, r, seq // r, window // (2 * r))

    def combine(i, carry):
        rows = pl.ds(pl.multiple_of(i * chunk, chunk), chunk)
        for pair in range(HEADS_PER_STEP // 2):
            cols = slice(pair * V7X_LANES, (pair + 1) * V7X_LANES)
            slots = [g * (HEADS_PER_STEP // 2) + pair for g in range(N_GROUPS)]
            ms = [m_ref[slot, rows, :] for slot in slots]
            top = functools.reduce(jnp.maximum, ms)
            ws = [jnp.exp2(m - top) for m in ms]
            num = sum(w * acc_ref[slot, rows, :] for slot, w in zip(slots, ws))
            den = sum(w * pltpu.roll(lsw_ref[slot, rows, :], HEAD_DIM, 1) for slot, w in zip(slots, ws))
            o_ref[0, rows, cols] = (num / den).astype(o_ref.dtype)
        return carry

    lax.fori_loop(0, seq // chunk, combine, 0)


def _band_bias(nside):
    u = jnp.arange(2 * nside)[None, :, None]
    j = jnp.arange(4 * nside)[None, None, :]
    off = (jnp.arange(3) * nside)[:, None, None]
    return jnp.where(jnp.abs(j - u - off) <= nside, 0.0, MASK_VALUE).astype(F32)


def _attention(group_qkv, seq):
    b = group_qkv[0].shape[0]
    n_quads = GROUP_WIDTH // QUAD_WIDTH
    tiles_per_half = GROUP_WIDTH // (2 * V7X_LANES)
    nside = DILATED_GROUPS[0][0] // (2 * DILATED_GROUPS[0][1])
    assert all(w // (2 * r) == nside for w, r in DILATED_GROUPS)
    args, specs = [], []
    for src, (_, r) in zip(group_qkv, DILATED_GROUPS):
        assert src.shape[:3] == (b, r, seq // r)
        for part in range(3):
            t0 = part * GROUP_WIDTH // V7X_LANES
            if part < 2:
                for half in range(2):
                    args.append(src)
                    specs.append(pl.BlockSpec((1, r, seq // r, V7X_LANES),
                                              lambda i, hq, t=t0 + half * tiles_per_half: (i, 0, 0, t + hq)))
            else:
                args.append(src)
                specs.append(pl.BlockSpec((1, r, seq // r, QUAD_WIDTH),
                                          lambda i, hq, t=t0 * V7X_LANES // QUAD_WIDTH: (i, 0, 0, t + hq)))
    args.append(_band_bias(nside))
    specs.append(pl.BlockSpec((3, 2 * nside, 4 * nside), lambda i, hq: (0, 0, 0)))
    return pl.pallas_call(
        functools.partial(_attention_kernel, seq=seq, chunk=256),
        grid=(b, n_quads),
        in_specs=specs,
        out_specs=pl.BlockSpec((1, seq, QUAD_WIDTH), lambda i, hq: (i, 0, hq)),
        out_shape=jax.ShapeDtypeStruct((b, seq, GROUP_WIDTH), BF16),
        scratch_shapes=[pltpu.VMEM((N_GROUPS * HEADS_PER_STEP // 2, seq, V7X_LANES), F32)] * 3,
        compiler_params=_params("parallel", "parallel"),
        name="attention",
    )(*args)


def _filter_kernel(feats_ref, t_ref, w1_ref, b1_ref, freq_ref, w2_ref, b2_ref, w3f_ref, w3b_ref, delta_ref, sgn_ref,
                   cos_hi_ref, cos_lo_ref, sin_hi_ref, sin_lo_ref, kr_ref, ki_ref, sum_ref, dif_ref, nyq_ref, hid_ref):
    fb = pl.program_id(2)
    tc = kr_ref.shape[2]

    dot = functools.partial(jnp.dot, precision=HIGHEST, preferred_element_type=F32)

    @pl.when(jnp.logical_and(fb == 0, jnp.logical_and(pl.program_id(0) == 0, pl.program_id(1) == 0)))
    def _():
        h = jnp.sin(freq_ref[0:1, :] * (dot(feats_ref[...], w1_ref[...]) + b1_ref[...]))
        hid_ref[...] = jnp.sin(freq_ref[1:2, :] * (dot(h, w2_ref[...]) + b2_ref[...]))

    @pl.when(fb == 0)
    def _():
        h = hid_ref[...]
        decay = jnp.exp(-t_ref[...] * jnp.abs(delta_ref[...]))
        fwd = dot(h, w3f_ref[...]) * decay
        bwd = dot(h, w3b_ref[...]) * decay
        row = lax.broadcasted_iota(jnp.int32, (fwd.shape[0], 1), 0)
        bwd = jnp.where(row == 0, 0.0, bwd)
        norm = jnp.sum(jnp.abs(fwd), axis=0, keepdims=True) + jnp.sum(jnp.abs(bwd), axis=0, keepdims=True)
        fwd = fwd / norm
        bwd = bwd / norm
        nyq_ref[...] = jnp.sum(sgn_ref[...] * (fwd + bwd), axis=0, keepdims=True)
        for ref, val in ((sum_ref, fwd + bwd), (dif_ref, fwd - bwd)):
            hi = val.astype(BF16)
            ref[:, :tc] = hi
            ref[:, tc:] = (val - hi.astype(F32)).astype(BF16)

    def dft(m_hi_ref, m_lo_ref, v_ref):
        hi_terms = jnp.dot(m_hi_ref[...], v_ref[...], preferred_element_type=F32)
        return hi_terms[:, :tc] + hi_terms[:, tc:] + jnp.dot(m_lo_ref[...], v_ref[:, :tc], preferred_element_type=F32)

    kr = dft(cos_hi_ref, cos_lo_ref, sum_ref)
    ki = dft(sin_hi_ref, sin_lo_ref, dif_ref)
    row0 = jnp.logical_and(lax.broadcasted_iota(jnp.int32, (kr.shape[0], 1), 0) == 0, fb == 0)
    kr_ref[0] = jnp.where(row0, 0.5 * kr, kr)
    ki_ref[0] = jnp.where(row0, nyq_ref[...], ki)


def _filter_spectrum(feats, t, w1, b1, freq, w2, b2, w3, deltas, sgn, dft_hi_lo, *, tc=256, tf=512):
    length = feats.shape[0]
    hid = w2.shape[0]
    chans = deltas.shape[1]
    ncb = chans // tc
    const = lambda o, cb, fb: (0, 0)
    out = jax.ShapeDtypeStruct((HYENA_ORDER, length, chans), F32)
    return pl.pallas_call(
        _filter_kernel,
        grid=(HYENA_ORDER, ncb, length // tf),
        in_specs=[
            pl.BlockSpec(feats.shape, const),
            pl.BlockSpec((length, 1), const),
            pl.BlockSpec(w1.shape, const),
            pl.BlockSpec((1, hid), const),
            pl.BlockSpec((2, hid), const),
            pl.BlockSpec((hid, hid), const),
            pl.BlockSpec((1, hid), const),
            pl.BlockSpec((hid, tc), lambda o, cb, fb: (0, (2 * o) * ncb + cb)),
            pl.BlockSpec((hid, tc), lambda o, cb, fb: (0, (2 * o + 1) * ncb + cb)),
            pl.BlockSpec((1, tc), lambda o, cb, fb: (0, cb)),
            pl.BlockSpec((length, 1), const),
        ] + [pl.BlockSpec((tf, length), lambda o, cb, fb: (fb, 0))] * 4,
        out_specs=[pl.BlockSpec((1, tf, tc), lambda o, cb, fb: (o, fb, cb))] * 2,
        out_shape=[out, out],
        scratch_shapes=[pltpu.VMEM((length, 2 * tc), BF16)] * 2 + [pltpu.VMEM((1, tc), F32), pltpu.VMEM((length, hid), F32)],
        compiler_params=_params("arbitrary", "arbitrary", "arbitrary"),
        name="hyena_filter",
    )(feats, t, w1, b1.reshape(1, hid), freq, w2, b2.reshape(1, hid), w3, w3, deltas, sgn, *dft_hi_lo)


DFT_ROWS = 512


def _reverse_rows(flip_ref, x):
    h = x.shape[0]
    return jnp.concatenate([jnp.dot(flip_ref[c * DFT_ROWS:(c + 1) * DFT_ROWS, :], x, preferred_element_type=F32)
                            for c in range(h // DFT_ROWS)], axis=0)


def _fold(u_ref, flip_ref):
    h = u_ref.shape[1] // 2
    return u_ref[0, 0:h, :].astype(F32), pltpu.roll(_reverse_rows(flip_ref, u_ref[0, h:, :]), 1, 0)


def _dwconv3_folded(a, b, w_ref, b_ref):
    h = a.shape[0]
    row = lax.broadcasted_iota(jnp.int32, (h, 1), 0)
    a_prev = jnp.where(row == 0, 0.0, pltpu.roll(a, 1, 0))
    a_next = jnp.where(row == h - 1, b[0:1, :], pltpu.roll(a, h - 1, 0))
    b_prev = jnp.where(row == 0, a[h - 1:h, :], pltpu.roll(b, h - 1, 0))
    b_next = jnp.where(row == 1, 0.0, pltpu.roll(b, 1, 0))
    w0, w1, w2, bias = w_ref[0:1, :], w_ref[1:2, :], w_ref[2:3, :], b_ref[...]
    return a_prev * w0 + a * w1 + a_next * w2 + bias, b_prev * w0 + b * w1 + b_next * w2 + bias


def _long_conv_kernel(*refs, z_from_u):
    if z_from_u:
        (uz_ref, cwz_ref, cbz_ref, ug_ref, cwg_ref, cbg_ref, kr_ref, ki_ref, skip_ref, sgn_ref, flip_ref,
         c1_ref, s1_ref, c2_ref, s2_ref, c2t_ref, s2t_ref, o_ref, yre_ref, yie_ref, yro_ref, yio_ref) = refs
        za, zb = _dwconv3_folded(*_fold(uz_ref, flip_ref), cwz_ref, cbz_ref)
    else:
        (z_ref, ug_ref, cwg_ref, cbg_ref, kr_ref, ki_ref, skip_ref, sgn_ref, flip_ref,
         c1_ref, s1_ref, c2_ref, s2_ref, c2t_ref, s2t_ref, o_ref, yre_ref, yie_ref, yro_ref, yio_ref, ob_ref) = refs
        za, zb = z_ref[0, 0], z_ref[0, 1]
    ga, gb = _dwconv3_folded(*_fold(ug_ref, flip_ref), cwg_ref, cbg_ref)
    h = za.shape[0]
    first = lax.broadcasted_iota(jnp.int32, (h, 1), 0) == 0
    sgn = sgn_ref[...]
    p = jnp.where(first, za, za + zb).astype(BF16)
    d = jnp.where(first, za, za - zb).astype(BF16)
    b0 = zb[0:1, :]
    xnyq = jnp.sum(sgn * (za + zb), axis=0, keepdims=True)
    ynyq_half = 0.5 * (xnyq * ki_ref[0, 0:1, :])
    skip = skip_ref[0]
    mid = jnp.zeros_like(b0)
    for c in range(h // DFT_ROWS):
        rows = slice(c * DFT_ROWS, (c + 1) * DFT_ROWS)
        odd_rows = slice(h + c * DFT_ROWS, h + (c + 1) * DFT_ROWS)
        edge = sgn[rows, :] * b0
        xr_e = jnp.dot(c1_ref[rows, :], p, preferred_element_type=F32) + edge
        xi_e = jnp.dot(s1_ref[rows, :], d, preferred_element_type=F32)
        xr_o = jnp.dot(c2_ref[rows, :], d, preferred_element_type=F32)
        xi_o = jnp.dot(s2_ref[rows, :], p, preferred_element_type=F32) - edge
        kr_e, ki_e, kr_o, ki_o = kr_ref[0, rows, :], ki_ref[0, rows, :], kr_ref[0, odd_rows, :], ki_ref[0, odd_rows, :]
        yr_e = xr_e * kr_e - xi_e * ki_e
        yi_o = xr_o * ki_o + xi_o * kr_o
        mid = mid + jnp.sum(sgn[rows, :] * (yr_e - yi_o), axis=0, keepdims=True)
        yre_ref[rows, :] = yr_e.astype(BF16)
        yie_ref[rows, :] = (xr_e * ki_e + xi_e * kr_e).astype(BF16)
        yro_ref[rows, :] = (xr_o * kr_o - xi_o * ki_o).astype(BF16)
        yio_ref[rows, :] = yi_o.astype(BF16)
    inv_n = 1.0 / (2 * h)
    for c in range(h // DFT_ROWS):
        rows = slice(c * DFT_ROWS, (c + 1) * DFT_ROWS)
        pp = (jnp.dot(c1_ref[rows, :], yre_ref[...], preferred_element_type=F32)
              + jnp.dot(s2t_ref[rows, :], yio_ref[...], preferred_element_type=F32)) + sgn[rows, :] * ynyq_half
        qq = (jnp.dot(s1_ref[rows, :], yie_ref[...], preferred_element_type=F32)
              + jnp.dot(c2t_ref[rows, :], yro_ref[...], preferred_element_type=F32))
        ya = (pp + qq) * inv_n
        yb = jnp.where(first[rows, :], mid + ynyq_half, pp - qq) * inv_n
        out_a = ga[rows, :] * (ya + za[rows, :] * skip)
        out_b = gb[rows, :] * (yb + zb[rows, :] * skip)
        if z_from_u:
            o_ref[0, 0, rows, :] = out_a
            o_ref[0, 1, rows, :] = out_b
        else:
            o_ref[0, rows, :] = out_a.astype(o_ref.dtype)
            ob_ref[rows, :] = out_b
    if not z_from_u:
        o_ref[0, h:, :] = _reverse_rows(flip_ref, pltpu.roll(ob_ref[...], h - 1, 0).astype(BF16)).astype(o_ref.dtype)


def _long_conv(z_src, u3d, conv_w, conv_b, kr, ki, skip, sgn, dft, *, order, z_part, gate_part, u_col0, tc=256):
    b, seq, _ = u3d.shape
    half = seq // 2
    chans = kr.shape[2]
    ncb = chans // tc
    u_cb0 = u_col0 // tc
    first_order = z_src is None

    def u_specs(part):
        return [
            pl.BlockSpec((1, seq, tc), lambda cb, i: (i, 0, u_cb0 + part * ncb + cb)),
            pl.BlockSpec((3, tc), lambda cb, i: (0, part * ncb + cb)),
            pl.BlockSpec((1, tc), lambda cb, i: (0, part * ncb + cb)),
        ]

    def dft_block(row_block):
        return pl.BlockSpec((half, half), lambda cb, i: (row_block, 0), pipeline_mode=pl.Buffered(1))

    folded_spec = pl.BlockSpec((1, 2, half, tc), lambda cb, i: (i, 0, 0, cb))
    cosm, sinm, cos_odd_t, sin_odd_t, flip = dft
    conv_b2 = conv_b.reshape(1, -1)
    if first_order:
        args, specs = [u3d, conv_w, conv_b2], u_specs(z_part)
    else:
        args, specs = [z_src], [folded_spec]
    args += [u3d, conv_w, conv_b2, kr, ki, skip.reshape(HYENA_ORDER, 1, chans), sgn, flip,
             cosm, sinm, cosm, sinm, cos_odd_t, sin_odd_t]
    specs += u_specs(gate_part) + [
        pl.BlockSpec((1, seq, tc), lambda cb, i: (order, 0, cb)),
        pl.BlockSpec((1, seq, tc), lambda cb, i: (order, 0, cb)),
        pl.BlockSpec((1, 1, tc), lambda cb, i: (order, 0, cb)),
        pl.BlockSpec((half, 1), lambda cb, i: (0, 0)),
        dft_block(0), dft_block(0), dft_block(0), dft_block(1), dft_block(1), dft_block(0), dft_block(0),
    ]
    return pl.pallas_call(
        functools.partial(_long_conv_kernel, z_from_u=first_order),
        grid=(ncb, b),
        in_specs=specs,
        out_specs=folded_spec if first_order else pl.BlockSpec((1, seq, tc), lambda cb, i: (i, 0, cb)),
        out_shape=jax.ShapeDtypeStruct((b, 2, half, chans), F32) if first_order
        else jax.ShapeDtypeStruct((b, seq, chans), BF16),
        scratch_shapes=[pltpu.VMEM((half, tc), BF16)] * 4 + ([] if first_order else [pltpu.VMEM((half, tc), F32)]),
        compiler_params=_params("parallel", "parallel"),
        name=f"long_conv{order}",
    )(*args)


def _merge_kernel(x_ref, g_ref, oa_ref, oh_ref, wga_ref, wgh_ref, bga_ref, bgh_ref, wpa_ref, wph_ref, wo_ref, o_ref, h_ref):
    c = pl.program_id(1)

    @pl.when(c == 0)
    def _():
        x = x_ref[...]
        h_ref[...] = _rmsnorm(x, g_ref[...]).astype(BF16)
        o_ref[...] = x

    h = h_ref[...]
    gate_a = 1.0 / (1.0 + jnp.exp(-(jnp.dot(h, wga_ref[...], preferred_element_type=F32) + bga_ref[...])))
    gate_h = 1.0 / (1.0 + jnp.exp(-(jnp.dot(h, wgh_ref[...], preferred_element_type=F32) + bgh_ref[...])))
    pa = jnp.dot(oa_ref[...], wpa_ref[...], preferred_element_type=F32)
    ph = jnp.dot(oh_ref[...], wph_ref[...], preferred_element_type=F32)
    mixed = (gate_a * pa + gate_h * ph).astype(BF16)
    o_ref[...] += jnp.dot(mixed, wo_ref[...], preferred_element_type=F32)


def _merge(x2d, gain, o_attn, o_hy, w_gate, b_gate, w_pa, w_ph, w_out, layer, *, tm=512, tc=512):
    rows, d = x2d.shape
    ncb = d // tc
    b_gate2 = b_gate.reshape(1, -1)
    return pl.pallas_call(
        _merge_kernel,
        grid=(rows // tm, ncb),
        in_specs=[
            pl.BlockSpec((tm, d), lambda i, c: (i, 0)),
            pl.BlockSpec((1, d), lambda i, c: (0, 0)),
            pl.BlockSpec((tm, o_attn.shape[1]), lambda i, c: (i, 0)),
            pl.BlockSpec((tm, o_hy.shape[1]), lambda i, c: (i, 0)),
            pl.BlockSpec((None, d, tc), lambda i, c: (layer, 0, c)),
            pl.BlockSpec((None, d, tc), lambda i, c: (layer, 0, ncb + c)),
            pl.BlockSpec((1, tc), lambda i, c: (0, c)),
            pl.BlockSpec((1, tc), lambda i, c: (0, ncb + c)),
            pl.BlockSpec((None, w_pa.shape[1], tc), lambda i, c: (layer, 0, c)),
            pl.BlockSpec((None, w_ph.shape[1], tc), lambda i, c: (layer, 0, c)),
            pl.BlockSpec((None, tc, d), lambda i, c: (layer, c, 0)),
        ],
        out_specs=pl.BlockSpec((tm, d), lambda i, c: (i, 0)),
        out_shape=jax.ShapeDtypeStruct((rows, d), F32),
        scratch_shapes=[pltpu.VMEM((tm, d), BF16)],
        compiler_params=_params("parallel", "arbitrary"),
        name="merge",
    )(x2d, gain.reshape(1, d), o_attn, o_hy, w_gate, w_gate, b_gate2, b_gate2, w_pa, w_ph, w_out)


HALO = 16


def _conv_ffn_kernel(*refs, blocks_per_seq, final_norm):
    if final_norm:
        x_ref, xp_ref, xn_ref, g_ref, wa_ref, wb_ref, cw_ref, cb_ref, wd_ref, gf_ref, o_ref, h_ref = refs
    else:
        x_ref, xp_ref, xn_ref, g_ref, wa_ref, wb_ref, cw_ref, cb_ref, wd_ref, o_ref, h_ref = refs
    i, f = pl.program_id(0), pl.program_id(1)
    tm = x_ref.shape[0]

    @pl.when(f == 0)
    def _():
        x = x_ref[...]
        g = g_ref[...]
        h_ref[0:HALO, :] = _rmsnorm(xp_ref[...], g).astype(BF16)
        h_ref[HALO:HALO + tm, :] = _rmsnorm(x, g).astype(BF16)
        h_ref[HALO + tm:, :] = _rmsnorm(xn_ref[...], g).astype(BF16)
        o_ref[...] = x

    up_a = jnp.dot(h_ref[...], wa_ref[...], preferred_element_type=F32)
    up_b = jnp.dot(h_ref[HALO:HALO + tm, :], wb_ref[...], preferred_element_type=F32)
    n_ext = up_a.shape[0]
    row = lax.broadcasted_iota(jnp.int32, (tm, 1), 0)
    seq_pos = i % blocks_per_seq
    at_start = jnp.logical_and(row == 0, seq_pos == 0)
    at_end = jnp.logical_and(row == tm - 1, seq_pos == blocks_per_seq - 1)
    a_prev = jnp.where(at_start, 0.0, pltpu.roll(up_a, 1, 0)[HALO:HALO + tm])
    a_next = jnp.where(at_end, 0.0, pltpu.roll(up_a, n_ext - 1, 0)[HALO:HALO + tm])
    a = a_prev * cw_ref[0:1, :] + up_a[HALO:HALO + tm] * cw_ref[1:2, :] + a_next * cw_ref[2:3, :] + cb_ref[...]
    gelu = 0.5 * a * (1.0 + jnp.tanh(math.sqrt(2.0 / math.pi) * (a + 0.044715 * (a * a * a))))
    o_ref[...] += jnp.dot((gelu * up_b).astype(BF16), wd_ref[...], preferred_element_type=F32)

    if final_norm:
        @pl.when(f == pl.num_programs(1) - 1)
        def _():
            o_ref[...] = _rmsnorm(o_ref[...], gf_ref[...])


def _conv_ffn(x2d, gain, w_up, conv_w, conv_b, w_down, layer, final_gain, *, seq, tm=1024, tf=512):
    rows, d = x2d.shape
    d_ff = w_down.shape[1]
    nfb = d_ff // tf
    halo_per_block = tm // HALO
    n_halo_blocks = rows // HALO
    args = [x2d, x2d, x2d, gain.reshape(1, d), w_up, w_up, conv_w, conv_b.reshape(1, d_ff), w_down]
    specs = [
        pl.BlockSpec((tm, d), lambda i, f: (i, 0)),
        pl.BlockSpec((HALO, d), lambda i, f: (jnp.maximum(i * halo_per_block - 1, 0), 0)),
        pl.BlockSpec((HALO, d), lambda i, f: (jnp.minimum((i + 1) * halo_per_block, n_halo_blocks - 1), 0)),
        pl.BlockSpec((1, d), lambda i, f: (0, 0)),
        pl.BlockSpec((None, d, tf), lambda i, f: (layer, 0, f)),
        pl.BlockSpec((None, d, tf), lambda i, f: (layer, 0, nfb + f)),
        pl.BlockSpec((3, tf), lambda i, f: (0, f)),
        pl.BlockSpec((1, tf), lambda i, f: (0, f)),
        pl.BlockSpec((None, tf, d), lambda i, f: (layer, f, 0)),
    ]
    if final_gain is not None:
        args.append(final_gain.reshape(1, d))
        specs.append(pl.BlockSpec((1, d), lambda i, f: (0, 0)))
    return pl.pallas_call(
        functools.partial(_conv_ffn_kernel, blocks_per_seq=seq // tm, final_norm=final_gain is not None),
        grid=(rows // tm, nfb),
        in_specs=specs,
        out_specs=pl.BlockSpec((tm, d), lambda i, f: (i, 0)),
        out_shape=jax.ShapeDtypeStruct((rows, d), F32),
        scratch_shapes=[pltpu.VMEM((tm + 2 * HALO, d), BF16)],
        compiler_params=_params("parallel", "arbitrary"),
        name="conv_ffn",
    )(*args)


def _rope_tables(seq):
    pos = jnp.arange(seq, dtype=F32)
    inv = 1.0 / (ROPE_THETA ** (jnp.arange(0, HEAD_DIM, 2, dtype=F32) / HEAD_DIM))
    ang = pos[:, None] * inv[None, :]
    ang = jnp.concatenate([ang] * (V7X_LANES // HALF_DIM), axis=-1)
    return jnp.cos(ang), jnp.sin(ang)


def _arrange_w_in(w_in):
    layers, d, _ = w_in.shape
    qk = w_in[:, :, :2 * ATTN_WIDTH].reshape(layers, d, 2, N_GROUPS, HEADS_PER_GROUP, 2, HALF_DIM)
    qk = qk.transpose(0, 1, 2, 3, 5, 4, 6).reshape(layers, d, 2, N_GROUPS, GROUP_WIDTH)
    v = w_in[:, :, 2 * ATTN_WIDTH:3 * ATTN_WIDTH].reshape(layers, d, N_GROUPS, GROUP_WIDTH)
    groups = [jnp.concatenate([qk[:, :, 0, g], qk[:, :, 1, g], v[:, :, g]], axis=-1) for g in range(N_GROUPS)]
    return jnp.concatenate([groups[0], w_in[:, :, 3 * ATTN_WIDTH:]] + groups[1:], axis=-1)


def _hyena_tables(length, chans):
    t = jnp.linspace(0.0, 1.0, length, dtype=F32)[:, None]
    bands = (HYENA_EMB_DIM - 1) // 2
    w = 2.0 * math.pi * jnp.arange(length, dtype=F32)[:, None] / length
    f = jnp.linspace(1e-4, bands - 1, bands, dtype=F32)[None, :]
    feats = jnp.concatenate([t, jnp.cos(f * w), -jnp.sin(f * w)], axis=-1)
    feats = jnp.pad(feats, ((0, 0), (0, V7X_LANES - HYENA_EMB_DIM)))
    max_decay = math.log(HYENA_DECAY_TARGET) / HYENA_FAST_DECAY
    min_decay = math.log(HYENA_DECAY_TARGET) / HYENA_SLOW_DECAY
    deltas = jnp.linspace(min_decay, max_decay, chans, dtype=F32)[None, :]
    idx = jnp.arange(length, dtype=jnp.int32)
    freq = jnp.concatenate([idx[0::2], idx[1::2]])
    ang = ((freq[:, None] * idx[None, :]) % (2 * length)).astype(F32) * (math.pi / length)
    sgn = (1 - 2 * (idx % 2)).astype(F32)[:, None]
    return feats, t, deltas, sgn, jnp.cos(ang), -jnp.sin(ang)


def kernel(x, attn_norm, w_in, hy_conv_w, hy_conv_b, f_w1, f_b1, f_freq, f_w2, f_b2, f_w3, hy_skip, w_proj_attn, w_proj_hyena, w_gate, b_gate, w_out, ffn_norm, w_up, ffn_conv_w, ffn_conv_b, w_down, final_norm):
    b, seq, d = x.shape
    depth = w_in.shape[0]
    chans = hy_skip.shape[2]
    cos, sin = _rope_tables(seq)
    feats, t, deltas, sgn, cosm, sinm = _hyena_tables(seq, chans)
    cosm16, sinm16 = cosm.astype(BF16), sinm.astype(BF16)
    dft_hi_lo = (cosm16, (cosm - cosm16.astype(F32)).astype(BF16), sinm16, (sinm - sinm16.astype(F32)).astype(BF16))
    half = seq // 2
    flip = (jnp.arange(half)[:, None] + jnp.arange(half)[None, :] == half - 1).astype(BF16)
    dft16 = (cosm16, sinm16, cosm16[half:, :half].T, sinm16[half:, :half].T, flip)
    w1_pad = jnp.pad(f_w1, ((0, 0), (0, V7X_LANES - HYENA_EMB_DIM), (0, 0)))
    w_in16 = _arrange_w_in(w_in).astype(BF16)
    w_gate16, w_pa16, w_ph16, w_out16, w_up16, w_down16 = (
        w.astype(BF16) for w in (w_gate, w_proj_attn, w_proj_hyena, w_out, w_up, w_down))
    n_nat_blocks = (w_in.shape[2] - 2 * QKV_WIDTH) // QKV_WIDTH

    x2d = x.reshape(b * seq, d)
    for l in range(depth):
        nat, h2d = _norm_proj(x2d, attn_norm[l], w_in16, l, n_nat_blocks, cos, sin, seq=seq)
        nat3d = nat.reshape(b, seq, -1)
        group_qkv = [nat3d.reshape(b, 1, seq, -1)] + [
            _dilated_proj(h2d, w_in16, l, n_nat_blocks + g - 1, r, cos, sin, batch=b, seq=seq)
            for g, (_, r) in enumerate(DILATED_GROUPS) if g > 0]
        o_attn = _attention(group_qkv, seq)
        kr, ki = _filter_spectrum(feats, t, w1_pad[l], f_b1[l], f_freq[l], f_w2[l], f_b2[l], f_w3[l], deltas, sgn, dft_hi_lo)
        conv = functools.partial(_long_conv, u3d=nat3d, conv_w=hy_conv_w[l], conv_b=hy_conv_b[l], kr=kr, ki=ki,
                                 skip=hy_skip[l], sgn=sgn[:half], dft=dft16, u_col0=QKV_WIDTH)
        z1 = conv(None, order=0, z_part=0, gate_part=1)
        o_hy = conv(z1, order=1, z_part=None, gate_part=2)
        x2d = _merge(x2d, attn_norm[l], o_attn.reshape(b * seq, -1), o_hy.reshape(b * seq, -1),
                     w_gate16, b_gate[l], w_pa16, w_ph16, w_out16, l)
        x2d = _conv_ffn(x2d, ffn_norm[l], w_up16, ffn_conv_w[l], ffn_conv_b[l], w_down16, l,
                        final_norm if l == depth - 1 else None, seq=seq)
    return x2d.reshape(b, seq, d)
```

```python
import functools
import math

import jax
import jax.numpy as jnp
from jax import lax
from jax.experimental import pallas as pl
from jax.experimental.pallas import tpu as pltpu

HEAD_DIM = 64
HEADS_PER_GROUP = 8
DILATED_GROUPS = ((128, 1), (512, 4), (2048, 16))
N_GROUPS = len(DILATED_GROUPS)
GROUP_WIDTH = HEADS_PER_GROUP * HEAD_DIM
ATTN_WIDTH = N_GROUPS * GROUP_WIDTH
ROPE_THETA = 10000.0
HYENA_ORDER = 2
HYENA_EMB_DIM = 33
HYENA_DECAY_TARGET = 1e-2
HYENA_FAST_DECAY = 0.3
HYENA_SLOW_DECAY = 1.5
RMS_EPS = 1e-6
MASK_VALUE = -1e30

V7X_LANES = 128
V7X_VMEM_LIMIT_BYTES = 60 * 1024 * 1024

PROJ_ROWS = 1024
MERGE_ROWS, MERGE_COLS = 512, 512
FFN_ROWS, FFN_COLS = 1024, 512
HYENA_CHANS = 256
FILTER_FREQ_ROWS = 512
COMBINE_ROWS = 256

F32 = jnp.float32
BF16 = jnp.bfloat16
HIGHEST = lax.Precision.HIGHEST


def _params(*semantics):
    return pltpu.CompilerParams(dimension_semantics=semantics, vmem_limit_bytes=V7X_VMEM_LIMIT_BYTES)


def _rmsnorm(x, g):
    return x * lax.rsqrt(jnp.mean(x * x, axis=-1, keepdims=True) + RMS_EPS) * g


QKV_WIDTH = 3 * GROUP_WIDTH
Q_SCALE = math.log2(math.e) / math.sqrt(HEAD_DIM)
MAX_SUBLANE_STRIDE = 4


def _proj_kernel(*refs, normalise, dilation):
    if normalise:
        x_ref, g_ref, w_ref, cos_ref, sin_ref, o_ref, h_ref = refs
    elif dilation > MAX_SUBLANE_STRIDE:
        h_ref, w_ref, cos_ref, sin_ref, o_ref, stage_ref, stage2_ref = refs
    elif dilation > 1:
        h_ref, w_ref, cos_ref, sin_ref, o_ref, stage_ref = refs
    else:
        h_ref, w_ref, cos_ref, sin_ref, o_ref = refs
    j = pl.program_id(1)

    if normalise:
        @pl.when(j == 0)
        def _():
            h_ref[...] = _rmsnorm(x_ref[...], g_ref[...]).astype(BF16)

    acc = jnp.dot(h_ref[...], w_ref[...], preferred_element_type=F32)
    tm = acc.shape[0]

    def put(tile, val):
        if dilation > 1:
            stage_ref[tile] = val
        else:
            o_ref[:, tile * V7X_LANES:(tile + 1) * V7X_LANES] = val.astype(o_ref.dtype)

    is_qkv = j == 0
    cos, sin = cos_ref[...], sin_ref[...]
    tiles_per_group = GROUP_WIDTH // V7X_LANES
    for part, scale in ((0, Q_SCALE), (1, 1.0)):
        cos_p = jnp.where(is_qkv, cos * scale, 1.0)
        sin_p = jnp.where(is_qkv, sin * scale, 0.0)
        for s in range(tiles_per_group // 2):
            lo_tile = part * tiles_per_group + 2 * s
            hi_tile = lo_tile + 1
            lo = acc[:, lo_tile * V7X_LANES:(lo_tile + 1) * V7X_LANES]
            hi = acc[:, hi_tile * V7X_LANES:(hi_tile + 1) * V7X_LANES]
            put(lo_tile, lo * cos_p - hi * sin_p)
            put(hi_tile, hi * cos_p + lo * sin_p)
    for tile in range(2 * tiles_per_group, acc.shape[1] // V7X_LANES):
        put(tile, acc[:, tile * V7X_LANES:(tile + 1) * V7X_LANES])
    if dilation > 1:
        n_tiles = acc.shape[1] // V7X_LANES
        first = min(dilation, MAX_SUBLANE_STRIDE)
        second = dilation // first
        if second > 1:
            for c1 in range(first):
                for tile in range(n_tiles):
                    stage2_ref[tile, c1 * (tm // first):(c1 + 1) * (tm // first), :] = (
                        stage_ref[tile, pl.ds(c1, tm // first, stride=first), :])
        for c in range(dilation):
            c1, c2 = c % first, c // first
            for tile in range(n_tiles):
                if second > 1:
                    rows = stage2_ref[tile, pl.ds(c1 * (tm // first) + c2, tm // dilation, stride=second), :]
                else:
                    rows = stage_ref[tile, pl.ds(c, tm // dilation, stride=dilation), :]
                o_ref[0, c, :, tile * V7X_LANES:(tile + 1) * V7X_LANES] = rows.astype(o_ref.dtype)


def _norm_proj(x2d, gain, w, layer, n_blocks, cos, sin, *, seq, tm=PROJ_ROWS):
    rows, d = x2d.shape
    table_spec = pl.BlockSpec((tm, V7X_LANES), lambda i, j: (i % (seq // tm), 0))
    return pl.pallas_call(
        functools.partial(_proj_kernel, normalise=True, dilation=1),
        grid=(rows // tm, n_blocks),
        in_specs=[
            pl.BlockSpec((tm, d), lambda i, j: (i, 0)),
            pl.BlockSpec((1, d), lambda i, j: (0, 0)),
            pl.BlockSpec((None, d, QKV_WIDTH), lambda i, j: (layer, 0, j)),
            table_spec,
            table_spec,
        ],
        out_specs=[pl.BlockSpec((tm, QKV_WIDTH), lambda i, j: (i, j)), pl.BlockSpec((tm, d), lambda i, j: (i, 0))],
        out_shape=[jax.ShapeDtypeStruct((rows, n_blocks * QKV_WIDTH), BF16), jax.ShapeDtypeStruct((rows, d), BF16)],
        compiler_params=_params("parallel", "arbitrary"),
        name="norm_proj",
    )(x2d, gain.reshape(1, d), w, cos, sin)


def _dilated_proj(h2d, w, layer, block, dilation, cos, sin, *, batch, seq, tm=PROJ_ROWS):
    rows, d = h2d.shape
    blocks_per_seq = seq // tm
    table_spec = pl.BlockSpec((tm, V7X_LANES), lambda i, j: (i % blocks_per_seq, 0))
    return pl.pallas_call(
        functools.partial(_proj_kernel, normalise=False, dilation=dilation),
        grid=(rows // tm, 1),
        in_specs=[
            pl.BlockSpec((tm, d), lambda i, j: (i, 0)),
            pl.BlockSpec((None, d, QKV_WIDTH), lambda i, j: (layer, 0, block), pipeline_mode=pl.Buffered(1)),
            table_spec,
            table_spec,
        ],
        out_specs=pl.BlockSpec((1, dilation, tm // dilation, QKV_WIDTH),
                               lambda i, j: (i // blocks_per_seq, 0, i % blocks_per_seq, 0)),
        out_shape=jax.ShapeDtypeStruct((batch, dilation, seq // dilation, QKV_WIDTH), BF16),
        scratch_shapes=[pltpu.VMEM((QKV_WIDTH // V7X_LANES, tm, V7X_LANES), F32)]
        * (2 if dilation > MAX_SUBLANE_STRIDE else 1),
        compiler_params=_params("parallel", "arbitrary"),
        name=f"dilated_proj{dilation}",
    )(h2d, w, cos, sin)


HEADS_PER_STEP = 4
HALF_DIM = HEAD_DIM // 2
QUAD_WIDTH = HEADS_PER_STEP * HEAD_DIM


def _attn_group(q_ref, k_ref, v_ref, bias_ref, acc_ref, m_ref, lsw_ref, g, r, t_len, nside):
    qb = 2 * nside
    kw = min(t_len, qb + 2 * nside)
    nb = t_len // qb
    lane_lo = lax.broadcasted_iota(jnp.int32, (1, V7X_LANES), 1) < HEAD_DIM
    head_of_lane = (lax.broadcasted_iota(jnp.int32, (1, QUAD_WIDTH), 1) % V7X_LANES) // HALF_DIM

    def body(idx, carry):
        c = idx // nb
        q0 = pl.multiple_of((idx % nb) * qb, qb)
        ks = pl.multiple_of(jnp.clip(q0 - nside, 0, t_len - kw), nside)
        q2 = q_ref[0, c, pl.ds(q0, qb), :]
        k2 = k_ref[0, c, pl.ds(ks, kw), :]
        zero = jnp.zeros_like(q2)
        qs = jnp.concatenate([jnp.where(head_of_lane == h, q2, zero) for h in range(HEADS_PER_STEP)], axis=0)
        s = lax.dot_general(qs, k2, (((1,), (1,)), ((), ())), preferred_element_type=F32)
        bias = bias_ref[(q0 - ks) // nside, :, :kw]
        s = (s.reshape(HEADS_PER_STEP, qb, kw) + bias[None]).reshape(HEADS_PER_STEP * qb, kw)
        m = jnp.max(s, axis=-1, keepdims=True)
        p = jnp.exp2(s - m).astype(BF16)
        rows = pl.ds(q0 * r + c, qb, stride=r) if r > 1 else pl.ds(q0, qb)
        for pair in range(HEADS_PER_STEP // 2):
            cols = slice(pair * V7X_LANES, (pair + 1) * V7X_LANES)
            v = v_ref[0, c, pl.ds(ks, kw), cols]
            one = jnp.ones_like(v)
            pv_a = jnp.dot(p[(2 * pair) * qb:(2 * pair + 1) * qb], jnp.where(lane_lo, v, one), preferred_element_type=F32)
            pv_b = jnp.dot(p[(2 * pair + 1) * qb:(2 * pair + 2) * qb], jnp.where(lane_lo, one, v), preferred_element_type=F32)
            slot = g * (HEADS_PER_STEP // 2) + pair
            acc_ref[slot, rows, :] = jnp.where(lane_lo, pv_a, pv_b)
            lsw_ref[slot, rows, :] = jnp.where(lane_lo, pv_b, pv_a)
            m_ref[slot, rows, :] = jnp.where(lane_lo, m[(2 * pair) * qb:(2 * pair + 1) * qb],
                                             m[(2 * pair + 1) * qb:(2 * pair + 2) * qb])
        return carry

    lax.fori_loop(0, r * nb, body, 0, unroll=16)


def _attention_kernel(*refs, seq, chunk):
    qkv = refs[:3 * N_GROUPS]
    bias_ref, o_ref, acc_ref, m_ref, lsw_ref = refs[3 * N_GROUPS:]
    for g, (window, r) in enumerate(DILATED_GROUPS):
        _attn_group(*qkv[3 * g:3 * g + 3], bias_ref, acc_ref, m_ref, lsw_ref, g, r, seq // r, window // (2 * r))

    def combine(i, carry):
        rows = pl.ds(pl.multiple_of(i * chunk, chunk), chunk)
        for pair in range(HEADS_PER_STEP // 2):
            cols = slice(pair * V7X_LANES, (pair + 1) * V7X_LANES)
            slots = [g * (HEADS_PER_STEP // 2) + pair for g in range(N_GROUPS)]
            ms = [m_ref[slot, rows, :] for slot in slots]
            top = functools.reduce(jnp.maximum, ms)
            ws = [jnp.exp2(m - top) for m in ms]
            num = sum(w * acc_ref[slot, rows, :] for slot, w in zip(slots, ws))
            den = sum(w * pltpu.roll(lsw_ref[slot, rows, :], HEAD_DIM, 1) for slot, w in zip(slots, ws))
            o_ref[0, rows, cols] = (num / den).astype(o_ref.dtype)
        return carry

    lax.fori_loop(0, seq // chunk, combine, 0)


def _band_bias(nside):
    u = jnp.arange(2 * nside)[None, :, None]
    j = jnp.arange(4 * nside)[None, None, :]
    off = (jnp.arange(3) * nside)[:, None, None]
    return jnp.where(jnp.abs(j - u - off) <= nside, 0.0, MASK_VALUE).astype(F32)


def _attention(group_qkv, seq):
    b = group_qkv[0].shape[0]
    n_quads = GROUP_WIDTH // QUAD_WIDTH
    nside = DILATED_GROUPS[0][0] // (2 * DILATED_GROUPS[0][1])
    assert all(w // (2 * r) == nside for w, r in DILATED_GROUPS)
    args, specs = [], []
    for src, (_, r) in zip(group_qkv, DILATED_GROUPS):
        assert src.shape[:3] == (b, r, seq // r)
        for part in range(3):
            args.append(src)
            specs.append(pl.BlockSpec((1, r, seq // r, QUAD_WIDTH),
                                      lambda i, hq, t=part * n_quads: (i, 0, 0, t + hq)))
    args.append(_band_bias(nside))
    specs.append(pl.BlockSpec((3, 2 * nside, 4 * nside), lambda i, hq: (0, 0, 0)))
    return pl.pallas_call(
        functools.partial(_attention_kernel, seq=seq, chunk=COMBINE_ROWS),
        grid=(b, n_quads),
        in_specs=specs,
        out_specs=pl.BlockSpec((1, seq, QUAD_WIDTH), lambda i, hq: (i, 0, hq)),
        out_shape=jax.ShapeDtypeStruct((b, seq, GROUP_WIDTH), BF16),
        scratch_shapes=[pltpu.VMEM((N_GROUPS * HEADS_PER_STEP // 2, seq, V7X_LANES), F32)] * 3,
        compiler_params=_params("parallel", "parallel"),
        name="attention",
    )(*args)


def _filter_kernel(feats_ref, t_ref, w1_ref, b1_ref, freq_ref, w2_ref, b2_ref, w3f_ref, w3b_ref, delta_ref, sgn_ref,
                   cos_hi_ref, cos_lo_ref, sin_hi_ref, sin_lo_ref, kr_ref, ki_ref, sum_ref, dif_ref, nyq_ref, hid_ref):
    fb = pl.program_id(2)
    tc = kr_ref.shape[2]

    dot = functools.partial(jnp.dot, precision=HIGHEST, preferred_element_type=F32)

    @pl.when(jnp.logical_and(fb == 0, jnp.logical_and(pl.program_id(0) == 0, pl.program_id(1) == 0)))
    def _():
        h = jnp.sin(freq_ref[0:1, :] * (dot(feats_ref[...], w1_ref[...]) + b1_ref[...]))
        hid_ref[...] = jnp.sin(freq_ref[1:2, :] * (dot(h, w2_ref[...]) + b2_ref[...]))

    @pl.when(fb == 0)
    def _():
        h = hid_ref[...]
        decay = jnp.exp(-t_ref[...] * jnp.abs(delta_ref[...]))
        fwd = dot(h, w3f_ref[...]) * decay
        bwd = dot(h, w3b_ref[...]) * decay
        row = lax.broadcasted_iota(jnp.int32, (fwd.shape[0], 1), 0)
        bwd = jnp.where(row == 0, 0.0, bwd)
        norm = jnp.sum(jnp.abs(fwd), axis=0, keepdims=True) + jnp.sum(jnp.abs(bwd), axis=0, keepdims=True)
        fwd = fwd / norm
        bwd = bwd / norm
        nyq_ref[...] = jnp.sum(sgn_ref[...] * (fwd + bwd), axis=0, keepdims=True)
        for ref, val in ((sum_ref, fwd + bwd), (dif_ref, fwd - bwd)):
            hi = val.astype(BF16)
            ref[:, :tc] = hi
            ref[:, tc:] = (val - hi.astype(F32)).astype(BF16)

    def dft(m_hi_ref, m_lo_ref, v_ref):
        hi_terms = jnp.dot(m_hi_ref[...], v_ref[...], preferred_element_type=F32)
        return hi_terms[:, :tc] + hi_terms[:, tc:] + jnp.dot(m_lo_ref[...], v_ref[:, :tc], preferred_element_type=F32)

    kr = dft(cos_hi_ref, cos_lo_ref, sum_ref)
    ki = dft(sin_hi_ref, sin_lo_ref, dif_ref)
    row0 = jnp.logical_and(lax.broadcasted_iota(jnp.int32, (kr.shape[0], 1), 0) == 0, fb == 0)
    kr_ref[0] = jnp.where(row0, 0.5 * kr, kr)
    ki_ref[0] = jnp.where(row0, nyq_ref[...], ki)


def _filter_spectrum(feats, t, w1, b1, freq, w2, b2, w3, deltas, sgn, dft_hi_lo, *, tc=HYENA_CHANS, tf=FILTER_FREQ_ROWS):
    length = feats.shape[0]
    hid = w2.shape[0]
    chans = deltas.shape[1]
    ncb = chans // tc
    const = lambda o, cb, fb: (0, 0)
    out = jax.ShapeDtypeStruct((HYENA_ORDER, length, chans), F32)
    return pl.pallas_call(
        _filter_kernel,
        grid=(HYENA_ORDER, ncb, length // tf),
        in_specs=[
            pl.BlockSpec(feats.shape, const),
            pl.BlockSpec((length, 1), const),
            pl.BlockSpec(w1.shape, const),
            pl.BlockSpec((1, hid), const),
            pl.BlockSpec((2, hid), const),
            pl.BlockSpec((hid, hid), const),
            pl.BlockSpec((1, hid), const),
            pl.BlockSpec((hid, tc), lambda o, cb, fb: (0, (2 * o) * ncb + cb)),
            pl.BlockSpec((hid, tc), lambda o, cb, fb: (0, (2 * o + 1) * ncb + cb)),
            pl.BlockSpec((1, tc), lambda o, cb, fb: (0, cb)),
            pl.BlockSpec((length, 1), const),
        ] + [pl.BlockSpec((tf, length), lambda o, cb, fb: (fb, 0))] * 4,
        out_specs=[pl.BlockSpec((1, tf, tc), lambda o, cb, fb: (o, fb, cb))] * 2,
        out_shape=[out, out],
        scratch_shapes=[pltpu.VMEM((length, 2 * tc), BF16)] * 2 + [pltpu.VMEM((1, tc), F32), pltpu.VMEM((length, hid), F32)],
        compiler_params=_params("arbitrary", "arbitrary", "arbitrary"),
        name="hyena_filter",
    )(feats, t, w1, b1.reshape(1, hid), freq, w2, b2.reshape(1, hid), w3, w3, deltas, sgn, *dft_hi_lo)


DFT_ROWS = 512


def _reverse_rows(flip_ref, x):
    h = x.shape[0]
    return jnp.concatenate([jnp.dot(flip_ref[c * DFT_ROWS:(c + 1) * DFT_ROWS, :], x, preferred_element_type=F32)
                            for c in range(h // DFT_ROWS)], axis=0)


def _fold(u_ref, flip_ref):
    h = u_ref.shape[1] // 2
    return u_ref[0, 0:h, :].astype(F32), pltpu.roll(_reverse_rows(flip_ref, u_ref[0, h:, :]), 1, 0)


def _dwconv3_folded(a, b, w_ref, b_ref):
    h = a.shape[0]
    row = lax.broadcasted_iota(jnp.int32, (h, 1), 0)
    a_prev = jnp.where(row == 0, 0.0, pltpu.roll(a, 1, 0))
    a_next = jnp.where(row == h - 1, b[0:1, :], pltpu.roll(a, h - 1, 0))
    b_prev = jnp.where(row == 0, a[h - 1:h, :], pltpu.roll(b, h - 1, 0))
    b_next = jnp.where(row == 1, 0.0, pltpu.roll(b, 1, 0))
    w0, w1, w2, bias = w_ref[0:1, :], w_ref[1:2, :], w_ref[2:3, :], b_ref[...]
    return a_prev * w0 + a * w1 + a_next * w2 + bias, b_prev * w0 + b * w1 + b_next * w2 + bias


def _long_conv_kernel(*refs, z_from_u):
    if z_from_u:
        (uz_ref, cwz_ref, cbz_ref, ug_ref, cwg_ref, cbg_ref, kr_ref, ki_ref, skip_ref, sgn_ref, flip_ref,
         c1_ref, s1_ref, c2_ref, s2_ref, c2t_ref, s2t_ref, o_ref, yre_ref, yie_ref, yro_ref, yio_ref) = refs
        za, zb = _dwconv3_folded(*_fold(uz_ref, flip_ref), cwz_ref, cbz_ref)
    else:
        (z_ref, ug_ref, cwg_ref, cbg_ref, kr_ref, ki_ref, skip_ref, sgn_ref, flip_ref,
         c1_ref, s1_ref, c2_ref, s2_ref, c2t_ref, s2t_ref, o_ref, yre_ref, yie_ref, yro_ref, yio_ref, ob_ref) = refs
        za, zb = z_ref[0, 0], z_ref[0, 1]
    ga, gb = _dwconv3_folded(*_fold(ug_ref, flip_ref), cwg_ref, cbg_ref)
    h = za.shape[0]
    first = lax.broadcasted_iota(jnp.int32, (h, 1), 0) == 0
    sgn = sgn_ref[...]
    p = jnp.where(first, za, za + zb).astype(BF16)
    d = jnp.where(first, za, za - zb).astype(BF16)
    b0 = zb[0:1, :]
    xnyq = jnp.sum(sgn * (za + zb), axis=0, keepdims=True)
    ynyq_half = 0.5 * (xnyq * ki_ref[0, 0:1, :])
    skip = skip_ref[0]
    mid = jnp.zeros_like(b0)
    for c in range(h // DFT_ROWS):
        rows = slice(c * DFT_ROWS, (c + 1) * DFT_ROWS)
        odd_rows = slice(h + c * DFT_ROWS, h + (c + 1) * DFT_ROWS)
        edge = sgn[rows, :] * b0
        xr_e = jnp.dot(c1_ref[rows, :], p, preferred_element_type=F32) + edge
        xi_e = jnp.dot(s1_ref[rows, :], d, preferred_element_type=F32)
        xr_o = jnp.dot(c2_ref[rows, :], d, preferred_element_type=F32)
        xi_o = jnp.dot(s2_ref[rows, :], p, preferred_element_type=F32) - edge
        kr_e, ki_e, kr_o, ki_o = kr_ref[0, rows, :], ki_ref[0, rows, :], kr_ref[0, odd_rows, :], ki_ref[0, odd_rows, :]
        yr_e = xr_e * kr_e - xi_e * ki_e
        yi_o = xr_o * ki_o + xi_o * kr_o
        mid = mid + jnp.sum(sgn[rows, :] * (yr_e - yi_o), axis=0, keepdims=True)
        yre_ref[rows, :] = yr_e.astype(BF16)
        yie_ref[rows, :] = (xr_e * ki_e + xi_e * kr_e).astype(BF16)
        yro_ref[rows, :] = (xr_o * kr_o - xi_o * ki_o).astype(BF16)
        yio_ref[rows, :] = yi_o.astype(BF16)
    inv_n = 1.0 / (2 * h)
    for c in range(h // DFT_ROWS):
        rows = slice(c * DFT_ROWS, (c + 1) * DFT_ROWS)
        pp = (jnp.dot(c1_ref[rows, :], yre_ref[...], preferred_element_type=F32)
              + jnp.dot(s2t_ref[rows, :], yio_ref[...], preferred_element_type=F32)) + sgn[rows, :] * ynyq_half
        qq = (jnp.dot(s1_ref[rows, :], yie_ref[...], preferred_element_type=F32)
              + jnp.dot(c2t_ref[rows, :], yro_ref[...], preferred_element_type=F32))
        ya = (pp + qq) * inv_n
        yb = jnp.where(first[rows, :], mid + ynyq_half, pp - qq) * inv_n
        out_a = ga[rows, :] * (ya + za[rows, :] * skip)
        out_b = gb[rows, :] * (yb + zb[rows, :] * skip)
        if z_from_u:
            o_ref[0, 0, rows, :] = out_a
            o_ref[0, 1, rows, :] = out_b
        else:
            o_ref[0, rows, :] = out_a.astype(o_ref.dtype)
            ob_ref[rows, :] = out_b
    if not z_from_u:
        o_ref[0, h:, :] = _reverse_rows(flip_ref, pltpu.roll(ob_ref[...], h - 1, 0).astype(BF16)).astype(o_ref.dtype)


def _long_conv(z_src, u3d, conv_w, conv_b, kr, ki, skip, sgn, dft, *, order, z_part, gate_part, u_col0, tc=HYENA_CHANS):
    b, seq, _ = u3d.shape
    half = seq // 2
    chans = kr.shape[2]
    ncb = chans // tc
    u_cb0 = u_col0 // tc
    first_order = z_src is None

    def u_specs(part):
        return [
            pl.BlockSpec((1, seq, tc), lambda cb, i: (i, 0, u_cb0 + part * ncb + cb)),
            pl.BlockSpec((3, tc), lambda cb, i: (0, part * ncb + cb)),
            pl.BlockSpec((1, tc), lambda cb, i: (0, part * ncb + cb)),
        ]

    def dft_block(row_block):
        return pl.BlockSpec((half, half), lambda cb, i: (row_block, 0), pipeline_mode=pl.Buffered(1))

    folded_spec = pl.BlockSpec((1, 2, half, tc), lambda cb, i: (i, 0, 0, cb))
    cosm, sinm, cos_odd_t, sin_odd_t, flip = dft
    conv_b2 = conv_b.reshape(1, -1)
    if first_order:
        args, specs = [u3d, conv_w, conv_b2], u_specs(z_part)
    else:
        args, specs = [z_src], [folded_spec]
    args += [u3d, conv_w, conv_b2, kr, ki, skip.reshape(HYENA_ORDER, 1, chans), sgn, flip,
             cosm, sinm, cosm, sinm, cos_odd_t, sin_odd_t]
    specs += u_specs(gate_part) + [
        pl.BlockSpec((1, seq, tc), lambda cb, i: (order, 0, cb)),
        pl.BlockSpec((1, seq, tc), lambda cb, i: (order, 0, cb)),
        pl.BlockSpec((1, 1, tc), lambda cb, i: (order, 0, cb)),
        pl.BlockSpec((half, 1), lambda cb, i: (0, 0)),
        dft_block(0), dft_block(0), dft_block(0), dft_block(1), dft_block(1), dft_block(0), dft_block(0),
    ]
    return pl.pallas_call(
        functools.partial(_long_conv_kernel, z_from_u=first_order),
        grid=(ncb, b),
        in_specs=specs,
        out_specs=folded_spec if first_order else pl.BlockSpec((1, seq, tc), lambda cb, i: (i, 0, cb)),
        out_shape=jax.ShapeDtypeStruct((b, 2, half, chans), F32) if first_order
        else jax.ShapeDtypeStruct((b, seq, chans), BF16),
        scratch_shapes=[pltpu.VMEM((half, tc), BF16)] * 4 + ([] if first_order else [pltpu.VMEM((half, tc), F32)]),
        compiler_params=_params("parallel", "parallel"),
        name=f"long_conv{order}",
    )(*args)


def _merge_kernel(x_ref, g_ref, oa_ref, oh_ref, wga_ref, wgh_ref, bg_ref, wpa_ref, wph_ref, wo_ref, o_ref, h_ref):
    c = pl.program_id(1)
    tc = wga_ref.shape[1]
    d = x_ref.shape[1]

    @pl.when(c == 0)
    def _():
        x = x_ref[...]
        h_ref[...] = _rmsnorm(x, g_ref[...]).astype(BF16)
        o_ref[...] = x

    h = h_ref[...]
    bias_a = bg_ref[:, pl.ds(pl.multiple_of(c * tc, tc), tc)]
    bias_h = bg_ref[:, pl.ds(pl.multiple_of(d + c * tc, tc), tc)]
    gate_a = 1.0 / (1.0 + jnp.exp(-(jnp.dot(h, wga_ref[...], preferred_element_type=F32) + bias_a)))
    gate_h = 1.0 / (1.0 + jnp.exp(-(jnp.dot(h, wgh_ref[...], preferred_element_type=F32) + bias_h)))
    pa = jnp.dot(oa_ref[...], wpa_ref[...], preferred_element_type=F32)
    ph = jnp.dot(oh_ref[...], wph_ref[...], preferred_element_type=F32)
    mixed = (gate_a * pa + gate_h * ph).astype(BF16)
    o_ref[...] += jnp.dot(mixed, wo_ref[...], preferred_element_type=F32)


def _merge(x2d, gain, o_attn, o_hy, w_gate, b_gate, w_pa, w_ph, w_out, layer, *, tm=MERGE_ROWS, tc=MERGE_COLS):
    rows, d = x2d.shape
    ncb = d // tc
    b_gate2 = b_gate.reshape(1, -1)
    return pl.pallas_call(
        _merge_kernel,
        grid=(rows // tm, ncb),
        in_specs=[
            pl.BlockSpec((tm, d), lambda i, c: (i, 0)),
            pl.BlockSpec((1, d), lambda i, c: (0, 0)),
            pl.BlockSpec((tm, o_attn.shape[1]), lambda i, c: (i, 0)),
            pl.BlockSpec((tm, o_hy.shape[1]), lambda i, c: (i, 0)),
            pl.BlockSpec((None, d, tc), lambda i, c: (layer, 0, c)),
            pl.BlockSpec((None, d, tc), lambda i, c: (layer, 0, ncb + c)),
            pl.BlockSpec((1, 2 * d), lambda i, c: (0, 0)),
            pl.BlockSpec((None, w_pa.shape[1], tc), lambda i, c: (layer, 0, c)),
            pl.BlockSpec((None, w_ph.shape[1], tc), lambda i, c: (layer, 0, c)),
            pl.BlockSpec((None, tc, d), lambda i, c: (layer, c, 0)),
        ],
        out_specs=pl.BlockSpec((tm, d), lambda i, c: (i, 0)),
        out_shape=jax.ShapeDtypeStruct((rows, d), F32),
        scratch_shapes=[pltpu.VMEM((tm, d), BF16)],
        compiler_params=_params("parallel", "arbitrary"),
        name="merge",
    )(x2d, gain.reshape(1, d), o_attn, o_hy, w_gate, w_gate, b_gate2, w_pa, w_ph, w_out)


HALO = 16


def _conv_ffn_kernel(*refs, blocks_per_seq, final_norm):
    if final_norm:
        x_ref, xp_ref, xn_ref, g_ref, wa_ref, wb_ref, cw_ref, cb_ref, wd_ref, gf_ref, o_ref, h_ref = refs
    else:
        x_ref, xp_ref, xn_ref, g_ref, wa_ref, wb_ref, cw_ref, cb_ref, wd_ref, o_ref, h_ref = refs
    i, f = pl.program_id(0), pl.program_id(1)
    tm = x_ref.shape[0]

    @pl.when(f == 0)
    def _():
        x = x_ref[...]
        g = g_ref[...]
        h_ref[0:HALO, :] = _rmsnorm(xp_ref[...], g).astype(BF16)
        h_ref[HALO:HALO + tm, :] = _rmsnorm(x, g).astype(BF16)
        h_ref[HALO + tm:, :] = _rmsnorm(xn_ref[...], g).astype(BF16)
        o_ref[...] = x

    up_a = jnp.dot(h_ref[...], wa_ref[...], preferred_element_type=F32)
    up_b = jnp.dot(h_ref[HALO:HALO + tm, :], wb_ref[...], preferred_element_type=F32)
    n_ext = up_a.shape[0]
    row = lax.broadcasted_iota(jnp.int32, (tm, 1), 0)
    seq_pos = i % blocks_per_seq
    at_start = jnp.logical_and(row == 0, seq_pos == 0)
    at_end = jnp.logical_and(row == tm - 1, seq_pos == blocks_per_seq - 1)
    a_prev = jnp.where(at_start, 0.0, pltpu.roll(up_a, 1, 0)[HALO:HALO + tm])
    a_next = jnp.where(at_end, 0.0, pltpu.roll(up_a, n_ext - 1, 0)[HALO:HALO + tm])
    tf = up_b.shape[1]
    cols = pl.ds(pl.multiple_of(f * tf, tf), tf)
    a = (a_prev * cw_ref[0:1, cols] + up_a[HALO:HALO + tm] * cw_ref[1:2, cols] + a_next * cw_ref[2:3, cols]
         + cb_ref[:, cols])
    gelu = 0.5 * a * (1.0 + jnp.tanh(math.sqrt(2.0 / math.pi) * (a + 0.044715 * (a * a * a))))
    o_ref[...] += jnp.dot((gelu * up_b).astype(BF16), wd_ref[...], preferred_element_type=F32)

    if final_norm:
        @pl.when(f == pl.num_programs(1) - 1)
        def _():
            o_ref[...] = _rmsnorm(o_ref[...], gf_ref[...])


def _conv_ffn(x2d, gain, w_up, conv_w, conv_b, w_down, layer, final_gain, *, seq, tm=FFN_ROWS, tf=FFN_COLS):
    rows, d = x2d.shape
    d_ff = w_down.shape[1]
    nfb = d_ff // tf
    halo_per_block = tm // HALO
    n_halo_blocks = rows // HALO
    args = [x2d, x2d, x2d, gain.reshape(1, d), w_up, w_up, conv_w, conv_b.reshape(1, d_ff), w_down]
    specs = [
        pl.BlockSpec((tm, d), lambda i, f: (i, 0)),
        pl.BlockSpec((HALO, d), lambda i, f: (jnp.maximum(i * halo_per_block - 1, 0), 0)),
        pl.BlockSpec((HALO, d), lambda i, f: (jnp.minimum((i + 1) * halo_per_block, n_halo_blocks - 1), 0)),
        pl.BlockSpec((1, d), lambda i, f: (0, 0)),
        pl.BlockSpec((None, d, tf), lambda i, f: (layer, 0, f)),
        pl.BlockSpec((None, d, tf), lambda i, f: (layer, 0, nfb + f)),
        pl.BlockSpec((3, d_ff), lambda i, f: (0, 0)),
        pl.BlockSpec((1, d_ff), lambda i, f: (0, 0)),
        pl.BlockSpec((None, tf, d), lambda i, f: (layer, f, 0)),
    ]
    if final_gain is not None:
        args.append(final_gain.reshape(1, d))
        specs.append(pl.BlockSpec((1, d), lambda i, f: (0, 0)))
    return pl.pallas_call(
        functools.partial(_conv_ffn_kernel, blocks_per_seq=seq // tm, final_norm=final_gain is not None),
        grid=(rows // tm, nfb),
        in_specs=specs,
        out_specs=pl.BlockSpec((tm, d), lambda i, f: (i, 0)),
        out_shape=jax.ShapeDtypeStruct((rows, d), F32),
        scratch_shapes=[pltpu.VMEM((tm + 2 * HALO, d), BF16)],
        compiler_params=_params("parallel", "arbitrary"),
        name="conv_ffn",
    )(*args)


def _rope_tables(seq):
    pos = jnp.arange(seq, dtype=F32)
    inv = 1.0 / (ROPE_THETA ** (jnp.arange(0, HEAD_DIM, 2, dtype=F32) / HEAD_DIM))
    ang = pos[:, None] * inv[None, :]
    ang = jnp.concatenate([ang] * (V7X_LANES // HALF_DIM), axis=-1)
    return jnp.cos(ang), jnp.sin(ang)


def _arrange_w_in(w_in):
    layers, d, _ = w_in.shape
    quads = HEADS_PER_GROUP // HEADS_PER_STEP
    qk = w_in[:, :, :2 * ATTN_WIDTH].reshape(layers, d, 2, N_GROUPS, quads, HEADS_PER_STEP, 2, HALF_DIM)
    qk = qk.transpose(0, 1, 2, 3, 4, 6, 5, 7).reshape(layers, d, 2, N_GROUPS, GROUP_WIDTH)
    v = w_in[:, :, 2 * ATTN_WIDTH:3 * ATTN_WIDTH].reshape(layers, d, N_GROUPS, GROUP_WIDTH)
    groups = [jnp.concatenate([qk[:, :, 0, g], qk[:, :, 1, g], v[:, :, g]], axis=-1) for g in range(N_GROUPS)]
    return jnp.concatenate([groups[0], w_in[:, :, 3 * ATTN_WIDTH:]] + groups[1:], axis=-1)


DFT_COARSE = 64


def _hyena_tables(length, chans):
    t = jnp.linspace(0.0, 1.0, length, dtype=F32)[:, None]
    bands = (HYENA_EMB_DIM - 1) // 2
    w = 2.0 * math.pi * jnp.arange(length, dtype=F32)[:, None] / length
    f = jnp.linspace(1e-4, bands - 1, bands, dtype=F32)[None, :]
    feats = jnp.concatenate([t, jnp.cos(f * w), -jnp.sin(f * w)], axis=-1)
    feats = jnp.pad(feats, ((0, 0), (0, V7X_LANES - HYENA_EMB_DIM)))
    max_decay = math.log(HYENA_DECAY_TARGET) / HYENA_FAST_DECAY
    min_decay = math.log(HYENA_DECAY_TARGET) / HYENA_SLOW_DECAY
    deltas = jnp.linspace(min_decay, max_decay, chans, dtype=F32)[None, :]
    idx = jnp.arange(length, dtype=jnp.int32)

    def trig(k):
        a = (k % (2 * length)).astype(F32) * (math.pi / length)
        return jnp.cos(a), jnp.sin(a)

    fine = jnp.arange(DFT_COARSE, dtype=jnp.int32)
    ca, sa = trig((DFT_COARSE * jnp.arange(length // DFT_COARSE, dtype=jnp.int32))[None, :, None, None] * idx)
    cb, sb = trig(jnp.stack([fine[0::2], fine[1::2]])[:, None, :, None] * idx)
    cosm = (ca * cb - sa * sb).reshape(length, length)
    sinm = (sa * cb + ca * sb).reshape(length, length)
    sgn = (1 - 2 * (idx % 2)).astype(F32)[:, None]
    return feats, t, deltas, sgn, cosm, -sinm


def kernel(x, attn_norm, w_in, hy_conv_w, hy_conv_b, f_w1, f_b1, f_freq, f_w2, f_b2, f_w3, hy_skip, w_proj_attn, w_proj_hyena, w_gate, b_gate, w_out, ffn_norm, w_up, ffn_conv_w, ffn_conv_b, w_down, final_norm):
    b, seq, d = x.shape
    depth = w_in.shape[0]
    chans = hy_skip.shape[2]
    cos, sin = _rope_tables(seq)
    feats, t, deltas, sgn, cosm, sinm = _hyena_tables(seq, chans)
    cosm16, sinm16 = cosm.astype(BF16), sinm.astype(BF16)
    dft_hi_lo = (cosm16, (cosm - cosm16.astype(F32)).astype(BF16), sinm16, (sinm - sinm16.astype(F32)).astype(BF16))
    half = seq // 2
    flip = (jnp.arange(half)[:, None] + jnp.arange(half)[None, :] == half - 1).astype(BF16)
    dft16 = (cosm16, sinm16, cosm16[half:, :half].T, sinm16[half:, :half].T, flip)
    w1_pad = jnp.pad(f_w1, ((0, 0), (0, V7X_LANES - HYENA_EMB_DIM), (0, 0)))
    w_in16 = _arrange_w_in(w_in).astype(BF16)
    w_gate16, w_pa16, w_ph16, w_out16, w_up16, w_down16 = (
        w.astype(BF16) for w in (w_gate, w_proj_attn, w_proj_hyena, w_out, w_up, w_down))
    n_nat_blocks = (w_in.shape[2] - 2 * QKV_WIDTH) // QKV_WIDTH

    x2d = x.reshape(b * seq, d)
    for l in range(depth):
        nat, h2d = _norm_proj(x2d, attn_norm[l], w_in16, l, n_nat_blocks, cos, sin, seq=seq)
        nat3d = nat.reshape(b, seq, -1)
        group_qkv = [nat3d.reshape(b, 1, seq, -1)] + [
            _dilated_proj(h2d, w_in16, l, n_nat_blocks + g - 1, r, cos, sin, batch=b, seq=seq)
            for g, (_, r) in enumerate(DILATED_GROUPS) if g > 0]
        o_attn = _attention(group_qkv, seq)
        kr, ki = _filter_spectrum(feats, t, w1_pad[l], f_b1[l], f_freq[l], f_w2[l], f_b2[l], f_w3[l], deltas, sgn, dft_hi_lo)
        conv = functools.partial(_long_conv, u3d=nat3d, conv_w=hy_conv_w[l], conv_b=hy_conv_b[l], kr=kr, ki=ki,
                                 skip=hy_skip[l], sgn=sgn[:half], dft=dft16, u_col0=QKV_WIDTH)
        z1 = conv(None, order=0, z_part=0, gate_part=1)
        o_hy = conv(z1, order=1, z_part=None, gate_part=2)
        x2d = _merge(x2d, attn_norm[l], o_attn.reshape(b * seq, -1), o_hy.reshape(b * seq, -1),
                     w_gate16, b_gate[l], w_pa16, w_ph16, w_out16, l)
        x2d = _conv_ffn(x2d, ffn_norm[l], w_up16, ffn_conv_w[l], ffn_conv_b[l], w_down16, l,
                        final_norm if l == depth - 1 else None, seq=seq)
    return x2d.reshape(b, seq, d)
```

```python
import functools
import math

import jax
import jax.numpy as jnp
from jax import lax
from jax.experimental import pallas as pl
from jax.experimental.pallas import tpu as pltpu

HEAD_DIM = 64
HEADS_PER_GROUP = 8
DILATED_GROUPS = ((128, 1), (512, 4), (2048, 16))
N_GROUPS = len(DILATED_GROUPS)
GROUP_WIDTH = HEADS_PER_GROUP * HEAD_DIM
ATTN_WIDTH = N_GROUPS * GROUP_WIDTH
ROPE_THETA = 10000.0
HYENA_ORDER = 2
HYENA_EMB_DIM = 33
HYENA_DECAY_TARGET = 1e-2
HYENA_FAST_DECAY = 0.3
HYENA_SLOW_DECAY = 1.5
RMS_EPS = 1e-6
MASK_VALUE = -1e30

V7X_LANES = 128
V7X_VMEM_LIMIT_BYTES = 60 * 1024 * 1024

PROJ_ROWS = 1024
MERGE_ROWS, MERGE_COLS = 512, 512
FFN_ROWS, FFN_COLS = 1024, 512
HYENA_CHANS = 256
FILTER_FREQ_ROWS = 512
COMBINE_ROWS = 256

F32 = jnp.float32
BF16 = jnp.bfloat16
HIGHEST = lax.Precision.HIGHEST


def _params(*semantics):
    return pltpu.CompilerParams(dimension_semantics=semantics, vmem_limit_bytes=V7X_VMEM_LIMIT_BYTES)


def _rmsnorm(x, g):
    return x * lax.rsqrt(jnp.mean(x * x, axis=-1, keepdims=True) + RMS_EPS) * g


QKV_WIDTH = 3 * GROUP_WIDTH
Q_SCALE = math.log2(math.e) / math.sqrt(HEAD_DIM)
MAX_SUBLANE_STRIDE = 4


def _proj_kernel(*refs, normalise, dilation):
    if normalise:
        x_ref, g_ref, w_ref, cos_ref, sin_ref, o_ref, h_ref = refs
    elif dilation > MAX_SUBLANE_STRIDE:
        h_ref, w_ref, cos_ref, sin_ref, o_ref, stage_ref, stage2_ref = refs
    elif dilation > 1:
        h_ref, w_ref, cos_ref, sin_ref, o_ref, stage_ref = refs
    else:
        h_ref, w_ref, cos_ref, sin_ref, o_ref = refs
    j = pl.program_id(1)

    if normalise:
        @pl.when(j == 0)
        def _():
            h_ref[...] = _rmsnorm(x_ref[...], g_ref[...]).astype(BF16)

    acc = jnp.dot(h_ref[...], w_ref[...], preferred_element_type=F32)
    tm = acc.shape[0]

    def put(tile, val):
        if dilation > 1:
            stage_ref[tile] = val
        else:
            o_ref[:, tile * V7X_LANES:(tile + 1) * V7X_LANES] = val.astype(o_ref.dtype)

    is_qkv = j == 0
    cos, sin = cos_ref[...], sin_ref[...]
    tiles_per_group = GROUP_WIDTH // V7X_LANES
    for part, scale in ((0, Q_SCALE), (1, 1.0)):
        cos_p = jnp.where(is_qkv, cos * scale, 1.0)
        sin_p = jnp.where(is_qkv, sin * scale, 0.0)
        for s in range(tiles_per_group // 2):
            lo_tile = part * tiles_per_group + 2 * s
            hi_tile = lo_tile + 1
            lo = acc[:, lo_tile * V7X_LANES:(lo_tile + 1) * V7X_LANES]
            hi = acc[:, hi_tile * V7X_LANES:(hi_tile + 1) * V7X_LANES]
            put(lo_tile, lo * cos_p - hi * sin_p)
            put(hi_tile, hi * cos_p + lo * sin_p)
    for tile in range(2 * tiles_per_group, acc.shape[1] // V7X_LANES):
        put(tile, acc[:, tile * V7X_LANES:(tile + 1) * V7X_LANES])
    if dilation > 1:
        n_tiles = acc.shape[1] // V7X_LANES
        first = min(dilation, MAX_SUBLANE_STRIDE)
        second = dilation // first
        if second > 1:
            for c1 in range(first):
                for tile in range(n_tiles):
                    stage2_ref[tile, c1 * (tm // first):(c1 + 1) * (tm // first), :] = (
                        stage_ref[tile, pl.ds(c1, tm // first, stride=first), :])
        for c in range(dilation):
            c1, c2 = c % first, c // first
            for tile in range(n_tiles):
                if second > 1:
                    rows = stage2_ref[tile, pl.ds(c1 * (tm // first) + c2, tm // dilation, stride=second), :]
                else:
                    rows = stage_ref[tile, pl.ds(c, tm // dilation, stride=dilation), :]
                o_ref[0, c, :, tile * V7X_LANES:(tile + 1) * V7X_LANES] = rows.astype(o_ref.dtype)


def _norm_proj(x2d, gain, w, layer, n_blocks, cos, sin, *, seq, tm=PROJ_ROWS):
    rows, d = x2d.shape
    table_spec = pl.BlockSpec((tm, V7X_LANES), lambda i, j: (i % (seq // tm), 0))
    return pl.pallas_call(
        functools.partial(_proj_kernel, normalise=True, dilation=1),
        grid=(rows // tm, n_blocks),
        in_specs=[
            pl.BlockSpec((tm, d), lambda i, j: (i, 0)),
            pl.BlockSpec((1, d), lambda i, j: (0, 0)),
            pl.BlockSpec((None, d, QKV_WIDTH), lambda i, j: (layer, 0, j)),
            table_spec,
            table_spec,
        ],
        out_specs=[pl.BlockSpec((tm, QKV_WIDTH), lambda i, j: (i, j)), pl.BlockSpec((tm, d), lambda i, j: (i, 0))],
        out_shape=[jax.ShapeDtypeStruct((rows, n_blocks * QKV_WIDTH), BF16), jax.ShapeDtypeStruct((rows, d), BF16)],
        compiler_params=_params("parallel", "arbitrary"),
        name="norm_proj",
    )(x2d, gain.reshape(1, d), w, cos, sin)


def _dilated_proj(h2d, w, layer, block, dilation, cos, sin, *, batch, seq, tm=PROJ_ROWS):
    rows, d = h2d.shape
    blocks_per_seq = seq // tm
    table_spec = pl.BlockSpec((tm, V7X_LANES), lambda i, j: (i % blocks_per_seq, 0))
    return pl.pallas_call(
        functools.partial(_proj_kernel, normalise=False, dilation=dilation),
        grid=(rows // tm, 1),
        in_specs=[
            pl.BlockSpec((tm, d), lambda i, j: (i, 0)),
            pl.BlockSpec((None, d, QKV_WIDTH), lambda i, j: (layer, 0, block), pipeline_mode=pl.Buffered(1)),
            table_spec,
            table_spec,
        ],
        out_specs=pl.BlockSpec((1, dilation, tm // dilation, QKV_WIDTH),
                               lambda i, j: (i // blocks_per_seq, 0, i % blocks_per_seq, 0)),
        out_shape=jax.ShapeDtypeStruct((batch, dilation, seq // dilation, QKV_WIDTH), BF16),
        scratch_shapes=[pltpu.VMEM((QKV_WIDTH // V7X_LANES, tm, V7X_LANES), F32)]
        * (2 if dilation > MAX_SUBLANE_STRIDE else 1),
        compiler_params=_params("parallel", "arbitrary"),
        name=f"dilated_proj{dilation}",
    )(h2d, w, cos, sin)


HEADS_PER_STEP = 4
HALF_DIM = HEAD_DIM // 2
QUAD_WIDTH = HEADS_PER_STEP * HEAD_DIM


def _attn_group(q_ref, k_ref, v_ref, bias_ref, acc_ref, m_ref, lsw_ref, g, r, t_len, nside):
    qb = 2 * nside
    kw = min(t_len, qb + 2 * nside)
    nb = t_len // qb
    lane_lo = lax.broadcasted_iota(jnp.int32, (1, V7X_LANES), 1) < HEAD_DIM
    head_of_lane = (lax.broadcasted_iota(jnp.int32, (1, QUAD_WIDTH), 1) % V7X_LANES) // HALF_DIM

    def body(idx, carry):
        c = idx // nb
        q0 = pl.multiple_of((idx % nb) * qb, qb)
        ks = pl.multiple_of(jnp.clip(q0 - nside, 0, t_len - kw), nside)
        q2 = q_ref[0, c, pl.ds(q0, qb), :]
        k2 = k_ref[0, c, pl.ds(ks, kw), :]
        zero = jnp.zeros_like(q2)
        qs = jnp.concatenate([jnp.where(head_of_lane == h, q2, zero) for h in range(HEADS_PER_STEP)], axis=0)
        s = lax.dot_general(qs, k2, (((1,), (1,)), ((), ())), preferred_element_type=F32)
        bias = bias_ref[(q0 - ks) // nside, :, :kw]
        s = (s.reshape(HEADS_PER_STEP, qb, kw) + bias[None]).reshape(HEADS_PER_STEP * qb, kw)
        m = jnp.max(s, axis=-1, keepdims=True)
        p = jnp.exp2(s - m).astype(BF16)
        rows = pl.ds(q0 * r + c, qb, stride=r) if r > 1 else pl.ds(q0, qb)
        for pair in range(HEADS_PER_STEP // 2):
            cols = slice(pair * V7X_LANES, (pair + 1) * V7X_LANES)
            v = v_ref[0, c, pl.ds(ks, kw), cols]
            one = jnp.ones_like(v)
            pv_a = jnp.dot(p[(2 * pair) * qb:(2 * pair + 1) * qb], jnp.where(lane_lo, v, one), preferred_element_type=F32)
            pv_b = jnp.dot(p[(2 * pair + 1) * qb:(2 * pair + 2) * qb], jnp.where(lane_lo, one, v), preferred_element_type=F32)
            slot = g * (HEADS_PER_STEP // 2) + pair
            acc_ref[slot, rows, :] = jnp.where(lane_lo, pv_a, pv_b)
            lsw_ref[slot, rows, :] = jnp.where(lane_lo, pv_b, pv_a)
            m_ref[slot, rows, :] = jnp.where(lane_lo, m[(2 * pair) * qb:(2 * pair + 1) * qb],
                                             m[(2 * pair + 1) * qb:(2 * pair + 2) * qb])
        return carry

    lax.fori_loop(0, r * nb, body, 0, unroll=16)


def _attention_kernel(*refs, seq, chunk):
    qkv = refs[:3 * N_GROUPS]
    bias_ref, o_ref, acc_ref, m_ref, lsw_ref = refs[3 * N_GROUPS:]
    for g, (window, r) in enumerate(DILATED_GROUPS):
        _attn_group(*qkv[3 * g:3 * g + 3], bias_ref, acc_ref, m_ref, lsw_ref, g, r, seq // r, window // (2 * r))

    def combine(i, carry):
        rows = pl.ds(pl.multiple_of(i * chunk, chunk), chunk)
        for pair in range(HEADS_PER_STEP // 2):
            cols = slice(pair * V7X_LANES, (pair + 1) * V7X_LANES)
            slots = [g * (HEADS_PER_STEP // 2) + pair for g in range(N_GROUPS)]
            ms = [m_ref[slot, rows, :] for slot in slots]
            top = functools.reduce(jnp.maximum, ms)
            ws = [jnp.exp2(m - top) for m in ms]
            num = sum(w * acc_ref[slot, rows, :] for slot, w in zip(slots, ws))
            den = sum(w * pltpu.roll(lsw_ref[slot, rows, :], HEAD_DIM, 1) for slot, w in zip(slots, ws))
            o_ref[0, rows, cols] = (num / den).astype(o_ref.dtype)
        return carry

    lax.fori_loop(0, seq // chunk, combine, 0)


def _band_bias(nside):
    u = jnp.arange(2 * nside)[None, :, None]
    j = jnp.arange(4 * nside)[None, None, :]
    off = (jnp.arange(3) * nside)[:, None, None]
    return jnp.where(jnp.abs(j - u - off) <= nside, 0.0, MASK_VALUE).astype(F32)


def _attention(group_qkv, seq):
    b = group_qkv[0].shape[0]
    n_quads = GROUP_WIDTH // QUAD_WIDTH
    nside = DILATED_GROUPS[0][0] // (2 * DILATED_GROUPS[0][1])
    assert all(w // (2 * r) == nside for w, r in DILATED_GROUPS)
    args, specs = [], []
    for src, (_, r) in zip(group_qkv, DILATED_GROUPS):
        assert src.shape[:3] == (b, r, seq // r)
        for part in range(3):
            args.append(src)
            specs.append(pl.BlockSpec((1, r, seq // r, QUAD_WIDTH),
                                      lambda i, hq, t=part * n_quads: (i, 0, 0, t + hq)))
    args.append(_band_bias(nside))
    specs.append(pl.BlockSpec((3, 2 * nside, 4 * nside), lambda i, hq: (0, 0, 0)))
    return pl.pallas_call(
        functools.partial(_attention_kernel, seq=seq, chunk=COMBINE_ROWS),
        grid=(b, n_quads),
        in_specs=specs,
        out_specs=pl.BlockSpec((1, seq, QUAD_WIDTH), lambda i, hq: (i, 0, hq)),
        out_shape=jax.ShapeDtypeStruct((b, seq, GROUP_WIDTH), BF16),
        scratch_shapes=[pltpu.VMEM((N_GROUPS * HEADS_PER_STEP // 2, seq, V7X_LANES), F32)] * 3,
        compiler_params=_params("parallel", "parallel"),
        name="attention",
    )(*args)


def _filter_kernel(feats_ref, t_ref, w1_ref, b1_ref, freq_ref, w2_ref, b2_ref, w3f_ref, w3b_ref, delta_ref, sgn_ref,
                   cos_hi_ref, cos_lo_ref, sin_hi_ref, sin_lo_ref, kr_ref, ki_ref, sum_ref, dif_ref, nyq_ref, hid_ref):
    fb = pl.program_id(2)
    tc = kr_ref.shape[2]

    dot = functools.partial(jnp.dot, precision=HIGHEST, preferred_element_type=F32)

    @pl.when(jnp.logical_and(fb == 0, jnp.logical_and(pl.program_id(0) == 0, pl.program_id(1) == 0)))
    def _():
        h = jnp.sin(freq_ref[0:1, :] * (dot(feats_ref[...], w1_ref[...]) + b1_ref[...]))
        hid_ref[...] = jnp.sin(freq_ref[1:2, :] * (dot(h, w2_ref[...]) + b2_ref[...]))

    @pl.when(fb == 0)
    def _():
        h = hid_ref[...]
        decay = jnp.exp(-t_ref[...] * jnp.abs(delta_ref[...]))
        fwd = dot(h, w3f_ref[...]) * decay
        bwd = dot(h, w3b_ref[...]) * decay
        row = lax.broadcasted_iota(jnp.int32, (fwd.shape[0], 1), 0)
        bwd = jnp.where(row == 0, 0.0, bwd)
        norm = jnp.sum(jnp.abs(fwd), axis=0, keepdims=True) + jnp.sum(jnp.abs(bwd), axis=0, keepdims=True)
        fwd = fwd / norm
        bwd = bwd / norm
        nyq_ref[...] = jnp.sum(sgn_ref[...] * (fwd + bwd), axis=0, keepdims=True)
        for ref, val in ((sum_ref, fwd + bwd), (dif_ref, fwd - bwd)):
            hi = val.astype(BF16)
            ref[:, :tc] = hi
            ref[:, tc:] = (val - hi.astype(F32)).astype(BF16)

    def dft(m_hi_ref, m_lo_ref, v_ref):
        hi_terms = jnp.dot(m_hi_ref[...], v_ref[...], preferred_element_type=F32)
        return hi_terms[:, :tc] + hi_terms[:, tc:] + jnp.dot(m_lo_ref[...], v_ref[:, :tc], preferred_element_type=F32)

    kr = dft(cos_hi_ref, cos_lo_ref, sum_ref)
    ki = dft(sin_hi_ref, sin_lo_ref, dif_ref)
    row0 = jnp.logical_and(lax.broadcasted_iota(jnp.int32, (kr.shape[0], 1), 0) == 0, fb == 0)
    kr_ref[0] = jnp.where(row0, 0.5 * kr, kr)
    ki_ref[0] = jnp.where(row0, nyq_ref[...], ki)


def _filter_spectrum(feats, t, w1, b1, freq, w2, b2, w3, deltas, sgn, dft_hi_lo, *, tc=HYENA_CHANS, tf=FILTER_FREQ_ROWS):
    length = feats.shape[0]
    hid = w2.shape[0]
    chans = deltas.shape[1]
    ncb = chans // tc
    const = lambda o, cb, fb: (0, 0)
    out = jax.ShapeDtypeStruct((HYENA_ORDER, length, chans), F32)
    return pl.pallas_call(
        _filter_kernel,
        grid=(HYENA_ORDER, ncb, length // tf),
        in_specs=[
            pl.BlockSpec(feats.shape, const),
            pl.BlockSpec((length, 1), const),
            pl.BlockSpec(w1.shape, const),
            pl.BlockSpec((1, hid), const),
            pl.BlockSpec((2, hid), const),
            pl.BlockSpec((hid, hid), const),
            pl.BlockSpec((1, hid), const),
            pl.BlockSpec((hid, tc), lambda o, cb, fb: (0, (2 * o) * ncb + cb)),
            pl.BlockSpec((hid, tc), lambda o, cb, fb: (0, (2 * o + 1) * ncb + cb)),
            pl.BlockSpec((1, tc), lambda o, cb, fb: (0, cb)),
            pl.BlockSpec((length, 1), const),
        ] + [pl.BlockSpec((tf, length), lambda o, cb, fb: (fb, 0))] * 4,
        out_specs=[pl.BlockSpec((1, tf, tc), lambda o, cb, fb: (o, fb, cb))] * 2,
        out_shape=[out, out],
        scratch_shapes=[pltpu.VMEM((length, 2 * tc), BF16)] * 2 + [pltpu.VMEM((1, tc), F32), pltpu.VMEM((length, hid), F32)],
        compiler_params=_params("arbitrary", "arbitrary", "arbitrary"),
        name="hyena_filter",
    )(feats, t, w1, b1.reshape(1, hid), freq, w2, b2.reshape(1, hid), w3, w3, deltas, sgn, *dft_hi_lo)


DFT_ROWS = 512


def _reverse_rows(flip_ref, x):
    h = x.shape[0]
    return jnp.concatenate([jnp.dot(flip_ref[c * DFT_ROWS:(c + 1) * DFT_ROWS, :], x, preferred_element_type=F32)
                            for c in range(h // DFT_ROWS)], axis=0)


def _fold(u_ref, flip_ref):
    h = u_ref.shape[1] // 2
    return u_ref[0, 0:h, :].astype(F32), pltpu.roll(_reverse_rows(flip_ref, u_ref[0, h:, :]), 1, 0)


def _dwconv3_folded(a, b, w_ref, b_ref):
    h = a.shape[0]
    row = lax.broadcasted_iota(jnp.int32, (h, 1), 0)
    a_prev = jnp.where(row == 0, 0.0, pltpu.roll(a, 1, 0))
    a_next = jnp.where(row == h - 1, b[0:1, :], pltpu.roll(a, h - 1, 0))
    b_prev = jnp.where(row == 0, a[h - 1:h, :], pltpu.roll(b, h - 1, 0))
    b_next = jnp.where(row == 1, 0.0, pltpu.roll(b, 1, 0))
    w0, w1, w2, bias = w_ref[0:1, :], w_ref[1:2, :], w_ref[2:3, :], b_ref[...]
    return a_prev * w0 + a * w1 + a_next * w2 + bias, b_prev * w0 + b * w1 + b_next * w2 + bias


def _long_conv_kernel(*refs, z_from_u):
    if z_from_u:
        (uz_ref, cwz_ref, cbz_ref, ug_ref, cwg_ref, cbg_ref, kr_ref, ki_ref, skip_ref, sgn_ref, flip_ref,
         c1_ref, s1_ref, c2_ref, s2_ref, c2t_ref, s2t_ref, o_ref, yre_ref, yie_ref, yro_ref, yio_ref) = refs
        za, zb = _dwconv3_folded(*_fold(uz_ref, flip_ref), cwz_ref, cbz_ref)
    else:
        (z_ref, ug_ref, cwg_ref, cbg_ref, kr_ref, ki_ref, skip_ref, sgn_ref, flip_ref,
         c1_ref, s1_ref, c2_ref, s2_ref, c2t_ref, s2t_ref, o_ref, yre_ref, yie_ref, yro_ref, yio_ref, ob_ref) = refs
        za, zb = z_ref[0, 0], z_ref[0, 1]
    ga, gb = _dwconv3_folded(*_fold(ug_ref, flip_ref), cwg_ref, cbg_ref)
    h = za.shape[0]
    first = lax.broadcasted_iota(jnp.int32, (h, 1), 0) == 0
    sgn = sgn_ref[...]
    p = jnp.where(first, za, za + zb).astype(BF16)
    d = jnp.where(first, za, za - zb).astype(BF16)
    b0 = zb[0:1, :]
    xnyq = jnp.sum(sgn * (za + zb), axis=0, keepdims=True)
    ynyq_half = 0.5 * (xnyq * ki_ref[0, 0:1, :])
    skip = skip_ref[0]
    mid = jnp.zeros_like(b0)
    for c in range(h // DFT_ROWS):
        rows = slice(c * DFT_ROWS, (c + 1) * DFT_ROWS)
        odd_rows = slice(h + c * DFT_ROWS, h + (c + 1) * DFT_ROWS)
        edge = sgn[rows, :] * b0
        xr_e = jnp.dot(c1_ref[rows, :], p, preferred_element_type=F32) + edge
        xi_e = jnp.dot(s1_ref[rows, :], d, preferred_element_type=F32)
        xr_o = jnp.dot(c2_ref[rows, :], d, preferred_element_type=F32)
        xi_o = jnp.dot(s2_ref[rows, :], p, preferred_element_type=F32) - edge
        kr_e, ki_e, kr_o, ki_o = kr_ref[0, rows, :], ki_ref[0, rows, :], kr_ref[0, odd_rows, :], ki_ref[0, odd_rows, :]
        yr_e = xr_e * kr_e - xi_e * ki_e
        yi_o = xr_o * ki_o + xi_o * kr_o
        mid = mid + jnp.sum(sgn[rows, :] * (yr_e - yi_o), axis=0, keepdims=True)
        yre_ref[rows, :] = yr_e.astype(BF16)
        yie_ref[rows, :] = (xr_e * ki_e + xi_e * kr_e).astype(BF16)
        yro_ref[rows, :] = (xr_o * kr_o - xi_o * ki_o).astype(BF16)
        yio_ref[rows, :] = yi_o.astype(BF16)
    inv_n = 1.0 / (2 * h)
    for c in range(h // DFT_ROWS):
        rows = slice(c * DFT_ROWS, (c + 1) * DFT_ROWS)
        pp = (jnp.dot(c1_ref[rows, :], yre_ref[...], preferred_element_type=F32)
              + jnp.dot(s2t_ref[rows, :], yio_ref[...], preferred_element_type=F32)) + sgn[rows, :] * ynyq_half
        qq = (jnp.dot(s1_ref[rows, :], yie_ref[...], preferred_element_type=F32)
              + jnp.dot(c2t_ref[rows, :], yro_ref[...], preferred_element_type=F32))
        ya = (pp + qq) * inv_n
        yb = jnp.where(first[rows, :], mid + ynyq_half, pp - qq) * inv_n
        out_a = ga[rows, :] * (ya + za[rows, :] * skip)
        out_b = gb[rows, :] * (yb + zb[rows, :] * skip)
        if z_from_u:
            o_ref[0, 0, rows, :] = out_a
            o_ref[0, 1, rows, :] = out_b
        else:
            o_ref[0, rows, :] = out_a.astype(o_ref.dtype)
            ob_ref[rows, :] = out_b
    if not z_from_u:
        o_ref[0, h:, :] = _reverse_rows(flip_ref, pltpu.roll(ob_ref[...], h - 1, 0).astype(BF16)).astype(o_ref.dtype)


def _long_conv(z_src, u3d, conv_w, conv_b, kr, ki, skip, sgn, dft, *, order, z_part, gate_part, u_col0, tc=HYENA_CHANS):
    b, seq, _ = u3d.shape
    half = seq // 2
    chans = kr.shape[2]
    ncb = chans // tc
    u_cb0 = u_col0 // tc
    first_order = z_src is None

    def u_specs(part):
        return [
            pl.BlockSpec((1, seq, tc), lambda cb, i: (i, 0, u_cb0 + part * ncb + cb)),
            pl.BlockSpec((3, tc), lambda cb, i: (0, part * ncb + cb)),
            pl.BlockSpec((1, tc), lambda cb, i: (0, part * ncb + cb)),
        ]

    def dft_block(row_block):
        return pl.BlockSpec((half, half), lambda cb, i: (row_block, 0), pipeline_mode=pl.Buffered(1))

    folded_spec = pl.BlockSpec((1, 2, half, tc), lambda cb, i: (i, 0, 0, cb))
    cosm, sinm, cos_odd_t, sin_odd_t, flip = dft
    conv_b2 = conv_b.reshape(1, -1)
    if first_order:
        args, specs = [u3d, conv_w, conv_b2], u_specs(z_part)
    else:
        args, specs = [z_src], [folded_spec]
    args += [u3d, conv_w, conv_b2, kr, ki, skip.reshape(HYENA_ORDER, 1, chans), sgn, flip,
             cosm, sinm, cosm, sinm, cos_odd_t, sin_odd_t]
    specs += u_specs(gate_part) + [
        pl.BlockSpec((1, seq, tc), lambda cb, i: (order, 0, cb)),
        pl.BlockSpec((1, seq, tc), lambda cb, i: (order, 0, cb)),
        pl.BlockSpec((1, 1, tc), lambda cb, i: (order, 0, cb)),
        pl.BlockSpec((half, 1), lambda cb, i: (0, 0)),
        dft_block(0), dft_block(0), dft_block(0), dft_block(1), dft_block(1), dft_block(0), dft_block(0),
    ]
    return pl.pallas_call(
        functools.partial(_long_conv_kernel, z_from_u=first_order),
        grid=(ncb, b),
        in_specs=specs,
        out_specs=folded_spec if first_order else pl.BlockSpec((1, seq, tc), lambda cb, i: (i, 0, cb)),
        out_shape=jax.ShapeDtypeStruct((b, 2, half, chans), F32) if first_order
        else jax.ShapeDtypeStruct((b, seq, chans), BF16),
        scratch_shapes=[pltpu.VMEM((half, tc), BF16)] * 4 + ([] if first_order else [pltpu.VMEM((half, tc), F32)]),
        compiler_params=_params("parallel", "parallel"),
        name=f"long_conv{order}",
    )(*args)


N_WEIGHT_STREAMS = 5


def _merge_kernel(x_ref, g_ref, oa_ref, oh_ref, bg_ref, wg_hbm, wpa_hbm, wph_hbm, wo_hbm, o_ref,
                  h_ref, wga_buf, wgh_buf, wpa_buf, wph_buf, wo_buf, sem, *, layer, tc):
    i = pl.program_id(0)
    d = x_ref.shape[1]
    n_chunks = d // tc

    def copies(c, slot):
        cols = slice(c * tc, (c + 1) * tc)
        gate_h_cols = slice(d + c * tc, d + (c + 1) * tc)
        return (
            pltpu.make_async_copy(wg_hbm.at[layer, :, cols], wga_buf.at[slot], sem.at[slot, 0]),
            pltpu.make_async_copy(wg_hbm.at[layer, :, gate_h_cols], wgh_buf.at[slot], sem.at[slot, 1]),
            pltpu.make_async_copy(wpa_hbm.at[layer, :, cols], wpa_buf.at[slot], sem.at[slot, 2]),
            pltpu.make_async_copy(wph_hbm.at[layer, :, cols], wph_buf.at[slot], sem.at[slot, 3]),
            pltpu.make_async_copy(wo_hbm.at[layer, cols, :], wo_buf.at[slot], sem.at[slot, 4]),
        )

    @pl.when(i == 0)
    def _():
        for cp in copies(0, 0):
            cp.start()

    x = x_ref[...]
    h_ref[...] = _rmsnorm(x, g_ref[...]).astype(BF16)
    o_ref[...] = x
    for c in range(n_chunks):
        slot = c % 2
        if c + 1 < n_chunks:
            for cp in copies(c + 1, 1 - slot):
                cp.start()
        else:
            @pl.when(i < pl.num_programs(0) - 1)
            def _():
                for cp in copies(0, 1 - slot):
                    cp.start()
        for cp in copies(c, slot):
            cp.wait()
        h = h_ref[...]
        bias_a = bg_ref[:, c * tc:(c + 1) * tc]
        bias_h = bg_ref[:, d + c * tc:d + (c + 1) * tc]
        gate_a = 1.0 / (1.0 + jnp.exp(-(jnp.dot(h, wga_buf[slot], preferred_element_type=F32) + bias_a)))
        gate_h = 1.0 / (1.0 + jnp.exp(-(jnp.dot(h, wgh_buf[slot], preferred_element_type=F32) + bias_h)))
        pa = jnp.dot(oa_ref[...], wpa_buf[slot], preferred_element_type=F32)
        ph = jnp.dot(oh_ref[...], wph_buf[slot], preferred_element_type=F32)
        mixed = (gate_a * pa + gate_h * ph).astype(BF16)
        o_ref[...] += jnp.dot(mixed, wo_buf[slot], preferred_element_type=F32)


def _merge(x2d, gain, o_attn, o_hy, w_gate, b_gate, w_pa, w_ph, w_out, layer, *, tm=MERGE_ROWS, tc=MERGE_COLS):
    rows, d = x2d.shape
    assert (d // tc) % 2 == 0, "chunk c must land in slot c % 2 again on the next row block"
    hbm = pl.BlockSpec(memory_space=pl.ANY)
    return pl.pallas_call(
        functools.partial(_merge_kernel, layer=layer, tc=tc),
        grid=(rows // tm,),
        in_specs=[
            pl.BlockSpec((tm, d), lambda i: (i, 0)),
            pl.BlockSpec((1, d), lambda i: (0, 0)),
            pl.BlockSpec((tm, o_attn.shape[1]), lambda i: (i, 0)),
            pl.BlockSpec((tm, o_hy.shape[1]), lambda i: (i, 0)),
            pl.BlockSpec((1, 2 * d), lambda i: (0, 0)),
            hbm, hbm, hbm, hbm,
        ],
        out_specs=pl.BlockSpec((tm, d), lambda i: (i, 0)),
        out_shape=jax.ShapeDtypeStruct((rows, d), F32),
        scratch_shapes=[
            pltpu.VMEM((tm, d), BF16),
            pltpu.VMEM((2, d, tc), BF16),
            pltpu.VMEM((2, d, tc), BF16),
            pltpu.VMEM((2, w_pa.shape[1], tc), BF16),
            pltpu.VMEM((2, w_ph.shape[1], tc), BF16),
            pltpu.VMEM((2, tc, d), BF16),
            pltpu.SemaphoreType.DMA((2, N_WEIGHT_STREAMS)),
        ],
        compiler_params=_params("arbitrary"),
        name="merge",
    )(x2d, gain.reshape(1, d), o_attn, o_hy, b_gate.reshape(1, -1), w_gate, w_pa, w_ph, w_out)


HALO = 16


def _conv_ffn_kernel(*refs, blocks_per_seq, final_norm):
    if final_norm:
        x_ref, xp_ref, xn_ref, g_ref, wa_ref, wb_ref, cw_ref, cb_ref, wd_ref, gf_ref, o_ref, h_ref = refs
    else:
        x_ref, xp_ref, xn_ref, g_ref, wa_ref, wb_ref, cw_ref, cb_ref, wd_ref, o_ref, h_ref = refs
    i, f = pl.program_id(0), pl.program_id(1)
    tm = x_ref.shape[0]

    @pl.when(f == 0)
    def _():
        x = x_ref[...]
        g = g_ref[...]
        h_ref[0:HALO, :] = _rmsnorm(xp_ref[...], g).astype(BF16)
        h_ref[HALO:HALO + tm, :] = _rmsnorm(x, g).astype(BF16)
        h_ref[HALO + tm:, :] = _rmsnorm(xn_ref[...], g).astype(BF16)
        o_ref[...] = x

    up_a = jnp.dot(h_ref[...], wa_ref[...], preferred_element_type=F32)
    up_b = jnp.dot(h_ref[HALO:HALO + tm, :], wb_ref[...], preferred_element_type=F32)
    n_ext = up_a.shape[0]
    row = lax.broadcasted_iota(jnp.int32, (tm, 1), 0)
    seq_pos = i % blocks_per_seq
    at_start = jnp.logical_and(row == 0, seq_pos == 0)
    at_end = jnp.logical_and(row == tm - 1, seq_pos == blocks_per_seq - 1)
    a_prev = jnp.where(at_start, 0.0, pltpu.roll(up_a, 1, 0)[HALO:HALO + tm])
    a_next = jnp.where(at_end, 0.0, pltpu.roll(up_a, n_ext - 1, 0)[HALO:HALO + tm])
    a = a_prev * cw_ref[0:1, :] + up_a[HALO:HALO + tm] * cw_ref[1:2, :] + a_next * cw_ref[2:3, :] + cb_ref[...]
    gelu = 0.5 * a * (1.0 + jnp.tanh(math.sqrt(2.0 / math.pi) * (a + 0.044715 * (a * a * a))))
    o_ref[...] += jnp.dot((gelu * up_b).astype(BF16), wd_ref[...], preferred_element_type=F32)

    if final_norm:
        @pl.when(f == pl.num_programs(1) - 1)
        def _():
            o_ref[...] = _rmsnorm(o_ref[...], gf_ref[...])


def _conv_ffn(x2d, gain, w_up, conv_w, conv_b, w_down, layer, final_gain, *, seq, tm=FFN_ROWS, tf=FFN_COLS):
    rows, d = x2d.shape
    d_ff = w_down.shape[1]
    nfb = d_ff // tf
    halo_per_block = tm // HALO
    n_halo_blocks = rows // HALO
    args = [x2d, x2d, x2d, gain.reshape(1, d), w_up, w_up, conv_w, conv_b.reshape(1, d_ff), w_down]
    specs = [
        pl.BlockSpec((tm, d), lambda i, f: (i, 0)),
        pl.BlockSpec((HALO, d), lambda i, f: (jnp.maximum(i * halo_per_block - 1, 0), 0)),
        pl.BlockSpec((HALO, d), lambda i, f: (jnp.minimum((i + 1) * halo_per_block, n_halo_blocks - 1), 0)),
        pl.BlockSpec((1, d), lambda i, f: (0, 0)),
        pl.BlockSpec((None, d, tf), lambda i, f: (layer, 0, f)),
        pl.BlockSpec((None, d, tf), lambda i, f: (layer, 0, nfb + f)),
        pl.BlockSpec((3, tf), lambda i, f: (0, f)),
        pl.BlockSpec((1, tf), lambda i, f: (0, f)),
        pl.BlockSpec((None, tf, d), lambda i, f: (layer, f, 0)),
    ]
    if final_gain is not None:
        args.append(final_gain.reshape(1, d))
        specs.append(pl.BlockSpec((1, d), lambda i, f: (0, 0)))
    return pl.pallas_call(
        functools.partial(_conv_ffn_kernel, blocks_per_seq=seq // tm, final_norm=final_gain is not None),
        grid=(rows // tm, nfb),
        in_specs=specs,
        out_specs=pl.BlockSpec((tm, d), lambda i, f: (i, 0)),
        out_shape=jax.ShapeDtypeStruct((rows, d), F32),
        scratch_shapes=[pltpu.VMEM((tm + 2 * HALO, d), BF16)],
        compiler_params=_params("parallel", "arbitrary"),
        name="conv_ffn",
    )(*args)


def _rope_tables(seq):
    pos = jnp.arange(seq, dtype=F32)
    inv = 1.0 / (ROPE_THETA ** (jnp.arange(0, HEAD_DIM, 2, dtype=F32) / HEAD_DIM))
    ang = pos[:, None] * inv[None, :]
    ang = jnp.concatenate([ang] * (V7X_LANES // HALF_DIM), axis=-1)
    return jnp.cos(ang), jnp.sin(ang)


def _arrange_w_in(w_in):
    layers, d, _ = w_in.shape
    quads = HEADS_PER_GROUP // HEADS_PER_STEP
    qk = w_in[:, :, :2 * ATTN_WIDTH].reshape(layers, d, 2, N_GROUPS, quads, HEADS_PER_STEP, 2, HALF_DIM)
    qk = qk.transpose(0, 1, 2, 3, 4, 6, 5, 7).reshape(layers, d, 2, N_GROUPS, GROUP_WIDTH)
    v = w_in[:, :, 2 * ATTN_WIDTH:3 * ATTN_WIDTH].reshape(layers, d, N_GROUPS, GROUP_WIDTH)
    groups = [jnp.concatenate([qk[:, :, 0, g], qk[:, :, 1, g], v[:, :, g]], axis=-1) for g in range(N_GROUPS)]
    return jnp.concatenate([groups[0], w_in[:, :, 3 * ATTN_WIDTH:]] + groups[1:], axis=-1)


DFT_COARSE = 64


def _hyena_tables(length, chans):
    t = jnp.linspace(0.0, 1.0, length, dtype=F32)[:, None]
    bands = (HYENA_EMB_DIM - 1) // 2
    w = 2.0 * math.pi * jnp.arange(length, dtype=F32)[:, None] / length
    f = jnp.linspace(1e-4, bands - 1, bands, dtype=F32)[None, :]
    feats = jnp.concatenate([t, jnp.cos(f * w), -jnp.sin(f * w)], axis=-1)
    feats = jnp.pad(feats, ((0, 0), (0, V7X_LANES - HYENA_EMB_DIM)))
    max_decay = math.log(HYENA_DECAY_TARGET) / HYENA_FAST_DECAY
    min_decay = math.log(HYENA_DECAY_TARGET) / HYENA_SLOW_DECAY
    deltas = jnp.linspace(min_decay, max_decay, chans, dtype=F32)[None, :]
    idx = jnp.arange(length, dtype=jnp.int32)

    def trig(k):
        a = (k % (2 * length)).astype(F32) * (math.pi / length)
        return jnp.cos(a), jnp.sin(a)

    fine = jnp.arange(DFT_COARSE, dtype=jnp.int32)
    ca, sa = trig((DFT_COARSE * jnp.arange(length // DFT_COARSE, dtype=jnp.int32))[None, :, None, None] * idx)
    cb, sb = trig(jnp.stack([fine[0::2], fine[1::2]])[:, None, :, None] * idx)
    cosm = (ca * cb - sa * sb).reshape(length, length)
    sinm = (sa * cb + ca * sb).reshape(length, length)
    sgn = (1 - 2 * (idx % 2)).astype(F32)[:, None]
    return feats, t, deltas, sgn, cosm, -sinm


def kernel(x, attn_norm, w_in, hy_conv_w, hy_conv_b, f_w1, f_b1, f_freq, f_w2, f_b2, f_w3, hy_skip, w_proj_attn, w_proj_hyena, w_gate, b_gate, w_out, ffn_norm, w_up, ffn_conv_w, ffn_conv_b, w_down, final_norm):
    b, seq, d = x.shape
    depth = w_in.shape[0]
    chans = hy_skip.shape[2]
    cos, sin = _rope_tables(seq)
    feats, t, deltas, sgn, cosm, sinm = _hyena_tables(seq, chans)
    cosm16, sinm16 = cosm.astype(BF16), sinm.astype(BF16)
    dft_hi_lo = (cosm16, (cosm - cosm16.astype(F32)).astype(BF16), sinm16, (sinm - sinm16.astype(F32)).astype(BF16))
    half = seq // 2
    flip = (jnp.arange(half)[:, None] + jnp.arange(half)[None, :] == half - 1).astype(BF16)
    dft16 = (cosm16, sinm16, cosm16[half:, :half].T, sinm16[half:, :half].T, flip)
    w1_pad = jnp.pad(f_w1, ((0, 0), (0, V7X_LANES - HYENA_EMB_DIM), (0, 0)))
    w_in16 = _arrange_w_in(w_in).astype(BF16)
    w_gate16, w_pa16, w_ph16, w_out16, w_up16, w_down16 = (
        w.astype(BF16) for w in (w_gate, w_proj_attn, w_proj_hyena, w_out, w_up, w_down))
    n_nat_blocks = (w_in.shape[2] - 2 * QKV_WIDTH) // QKV_WIDTH

    x2d = x.reshape(b * seq, d)
    for l in range(depth):
        nat, h2d = _norm_proj(x2d, attn_norm[l], w_in16, l, n_nat_blocks, cos, sin, seq=seq)
        nat3d = nat.reshape(b, seq, -1)
        group_qkv = [nat3d.reshape(b, 1, seq, -1)] + [
            _dilated_proj(h2d, w_in16, l, n_nat_blocks + g - 1, r, cos, sin, batch=b, seq=seq)
            for g, (_, r) in enumerate(DILATED_GROUPS) if g > 0]
        o_attn = _attention(group_qkv, seq)
        kr, ki = _filter_spectrum(feats, t, w1_pad[l], f_b1[l], f_freq[l], f_w2[l], f_b2[l], f_w3[l], deltas, sgn, dft_hi_lo)
        conv = functools.partial(_long_conv, u3d=nat3d, conv_w=hy_conv_w[l], conv_b=hy_conv_b[l], kr=kr, ki=ki,
                                 skip=hy_skip[l], sgn=sgn[:half], dft=dft16, u_col0=QKV_WIDTH)
        z1 = conv(None, order=0, z_part=0, gate_part=1)
        o_hy = conv(z1, order=1, z_part=None, gate_part=2)
        x2d = _merge(x2d, attn_norm[l], o_attn.reshape(b * seq, -1), o_hy.reshape(b * seq, -1),
                     w_gate16, b_gate[l], w_pa16, w_ph16, w_out16, l)
        x2d = _conv_ffn(x2d, ffn_norm[l], w_up16, ffn_conv_w[l], ffn_conv_b[l], w_down16, l,
                        final_norm if l == depth - 1 else None, seq=seq)
    return x2d.reshape(b, seq, d)
```

```python
import functools
import math

import jax
import jax.numpy as jnp
from jax import lax
from jax.experimental import pallas as pl
from jax.experimental.pallas import tpu as pltpu

HEAD_DIM = 64
HEADS_PER_GROUP = 8
DILATED_GROUPS = ((128, 1), (512, 4), (2048, 16))
N_GROUPS = len(DILATED_GROUPS)
GROUP_WIDTH = HEADS_PER_GROUP * HEAD_DIM
ATTN_WIDTH = N_GROUPS * GROUP_WIDTH
ROPE_THETA = 10000.0
HYENA_ORDER = 2
HYENA_EMB_DIM = 33
HYENA_DECAY_TARGET = 1e-2
HYENA_FAST_DECAY = 0.3
HYENA_SLOW_DECAY = 1.5
RMS_EPS = 1e-6
MASK_VALUE = -1e30

V7X_LANES = 128
V7X_VMEM_LIMIT_BYTES = 60 * 1024 * 1024

PROJ_ROWS = 1024
MERGE_ROWS, MERGE_COLS = 512, 512
FFN_ROWS, FFN_COLS = 1024, 512
HYENA_CHANS = 256
FILTER_FREQ_ROWS = 512
COMBINE_ROWS = 256

F32 = jnp.float32
BF16 = jnp.bfloat16
HIGHEST = lax.Precision.HIGHEST


def _params(*semantics):
    return pltpu.CompilerParams(dimension_semantics=semantics, vmem_limit_bytes=V7X_VMEM_LIMIT_BYTES)


def _rmsnorm(x, g):
    return x * lax.rsqrt(jnp.mean(x * x, axis=-1, keepdims=True) + RMS_EPS) * g


QKV_WIDTH = 3 * GROUP_WIDTH
Q_SCALE = math.log2(math.e) / math.sqrt(HEAD_DIM)
MAX_SUBLANE_STRIDE = 4


def _proj_kernel(*refs, normalise, dilation):
    if normalise:
        x_ref, g_ref, w_ref, cos_ref, sin_ref, o_ref, h_ref = refs
    elif dilation > MAX_SUBLANE_STRIDE:
        h_ref, w_ref, cos_ref, sin_ref, o_ref, stage_ref, stage2_ref = refs
    elif dilation > 1:
        h_ref, w_ref, cos_ref, sin_ref, o_ref, stage_ref = refs
    else:
        h_ref, w_ref, cos_ref, sin_ref, o_ref = refs
    j = pl.program_id(1)

    if normalise:
        @pl.when(j == 0)
        def _():
            h_ref[...] = _rmsnorm(x_ref[...], g_ref[...]).astype(BF16)

    acc = jnp.dot(h_ref[...], w_ref[...], preferred_element_type=F32)
    tm = acc.shape[0]

    def put(tile, val):
        if dilation > 1:
            stage_ref[tile] = val
        else:
            o_ref[:, tile * V7X_LANES:(tile + 1) * V7X_LANES] = val.astype(o_ref.dtype)

    is_qkv = j == 0
    cos, sin = cos_ref[...], sin_ref[...]
    tiles_per_group = GROUP_WIDTH // V7X_LANES
    for part, scale in ((0, Q_SCALE), (1, 1.0)):
        cos_p = jnp.where(is_qkv, cos * scale, 1.0)
        sin_p = jnp.where(is_qkv, sin * scale, 0.0)
        for s in range(tiles_per_group // 2):
            lo_tile = part * tiles_per_group + 2 * s
            hi_tile = lo_tile + 1
            lo = acc[:, lo_tile * V7X_LANES:(lo_tile + 1) * V7X_LANES]
            hi = acc[:, hi_tile * V7X_LANES:(hi_tile + 1) * V7X_LANES]
            put(lo_tile, lo * cos_p - hi * sin_p)
            put(hi_tile, hi * cos_p + lo * sin_p)
    for tile in range(2 * tiles_per_group, acc.shape[1] // V7X_LANES):
        put(tile, acc[:, tile * V7X_LANES:(tile + 1) * V7X_LANES])
    if dilation > 1:
        n_tiles = acc.shape[1] // V7X_LANES
        first = min(dilation, MAX_SUBLANE_STRIDE)
        second = dilation // first
        if second > 1:
            for c1 in range(first):
                for tile in range(n_tiles):
                    stage2_ref[tile, c1 * (tm // first):(c1 + 1) * (tm // first), :] = (
                        stage_ref[tile, pl.ds(c1, tm // first, stride=first), :])
        for c in range(dilation):
            c1, c2 = c % first, c // first
            for tile in range(n_tiles):
                if second > 1:
                    rows = stage2_ref[tile, pl.ds(c1 * (tm // first) + c2, tm // dilation, stride=second), :]
                else:
                    rows = stage_ref[tile, pl.ds(c, tm // dilation, stride=dilation), :]
                o_ref[0, c, :, tile * V7X_LANES:(tile + 1) * V7X_LANES] = rows.astype(o_ref.dtype)


def _norm_proj(x2d, gain, w, layer, n_blocks, cos, sin, *, seq, tm=PROJ_ROWS):
    rows, d = x2d.shape
    table_spec = pl.BlockSpec((tm, V7X_LANES), lambda i, j: (i % (seq // tm), 0))
    return pl.pallas_call(
        functools.partial(_proj_kernel, normalise=True, dilation=1),
        grid=(rows // tm, n_blocks),
        in_specs=[
            pl.BlockSpec((tm, d), lambda i, j: (i, 0)),
            pl.BlockSpec((1, d), lambda i, j: (0, 0)),
            pl.BlockSpec((None, d, QKV_WIDTH), lambda i, j: (layer, 0, j)),
            table_spec,
            table_spec,
        ],
        out_specs=[pl.BlockSpec((tm, QKV_WIDTH), lambda i, j: (i, j)), pl.BlockSpec((tm, d), lambda i, j: (i, 0))],
        out_shape=[jax.ShapeDtypeStruct((rows, n_blocks * QKV_WIDTH), BF16), jax.ShapeDtypeStruct((rows, d), BF16)],
        compiler_params=_params("parallel", "arbitrary"),
        name="norm_proj",
    )(x2d, gain.reshape(1, d), w, cos, sin)


def _dilated_proj(h2d, w, layer, block, dilation, cos, sin, *, batch, seq, tm=PROJ_ROWS):
    rows, d = h2d.shape
    blocks_per_seq = seq // tm
    table_spec = pl.BlockSpec((tm, V7X_LANES), lambda i, j: (i % blocks_per_seq, 0))
    return pl.pallas_call(
        functools.partial(_proj_kernel, normalise=False, dilation=dilation),
        grid=(rows // tm, 1),
        in_specs=[
            pl.BlockSpec((tm, d), lambda i, j: (i, 0)),
            pl.BlockSpec((None, d, QKV_WIDTH), lambda i, j: (layer, 0, block), pipeline_mode=pl.Buffered(1)),
            table_spec,
            table_spec,
        ],
        out_specs=pl.BlockSpec((1, dilation, tm // dilation, QKV_WIDTH),
                               lambda i, j: (i // blocks_per_seq, 0, i % blocks_per_seq, 0)),
        out_shape=jax.ShapeDtypeStruct((batch, dilation, seq // dilation, QKV_WIDTH), BF16),
        scratch_shapes=[pltpu.VMEM((QKV_WIDTH // V7X_LANES, tm, V7X_LANES), F32)]
        * (2 if dilation > MAX_SUBLANE_STRIDE else 1),
        compiler_params=_params("parallel", "arbitrary"),
        name=f"dilated_proj{dilation}",
    )(h2d, w, cos, sin)


HEADS_PER_STEP = 4
HALF_DIM = HEAD_DIM // 2
QUAD_WIDTH = HEADS_PER_STEP * HEAD_DIM


def _attn_group(q_ref, k_ref, v_ref, bias_ref, acc_ref, m_ref, lsw_ref, g, r, t_len, nside):
    qb = 2 * nside
    kw = min(t_len, qb + 2 * nside)
    nb = t_len // qb
    lane_lo = lax.broadcasted_iota(jnp.int32, (1, V7X_LANES), 1) < HEAD_DIM
    head_of_lane = (lax.broadcasted_iota(jnp.int32, (1, QUAD_WIDTH), 1) % V7X_LANES) // HALF_DIM

    def body(idx, carry):
        c = idx // nb
        q0 = pl.multiple_of((idx % nb) * qb, qb)
        ks = pl.multiple_of(jnp.clip(q0 - nside, 0, t_len - kw), nside)
        q2 = q_ref[0, c, pl.ds(q0, qb), :]
        k2 = k_ref[0, c, pl.ds(ks, kw), :]
        zero = jnp.zeros_like(q2)
        qs = jnp.concatenate([jnp.where(head_of_lane == h, q2, zero) for h in range(HEADS_PER_STEP)], axis=0)
        s = lax.dot_general(qs, k2, (((1,), (1,)), ((), ())), preferred_element_type=F32)
        bias = bias_ref[(q0 - ks) // nside, :, :kw]
        s = (s.reshape(HEADS_PER_STEP, qb, kw) + bias[None]).reshape(HEADS_PER_STEP * qb, kw)
        m = jnp.max(s, axis=-1, keepdims=True)
        p = jnp.exp2(s - m).astype(BF16)
        rows = pl.ds(q0 * r + c, qb, stride=r) if r > 1 else pl.ds(q0, qb)
        for pair in range(HEADS_PER_STEP // 2):
            cols = slice(pair * V7X_LANES, (pair + 1) * V7X_LANES)
            v = v_ref[0, c, pl.ds(ks, kw), cols]
            one = jnp.ones_like(v)
            pv_a = jnp.dot(p[(2 * pair) * qb:(2 * pair + 1) * qb], jnp.where(lane_lo, v, one), preferred_element_type=F32)
            pv_b = jnp.dot(p[(2 * pair + 1) * qb:(2 * pair + 2) * qb], jnp.where(lane_lo, one, v), preferred_element_type=F32)
            slot = g * (HEADS_PER_STEP // 2) + pair
            acc_ref[slot, rows, :] = jnp.where(lane_lo, pv_a, pv_b)
            lsw_ref[slot, rows, :] = pltpu.roll(jnp.where(lane_lo, pv_b, pv_a), HEAD_DIM, 1)
            m_ref[slot, rows, :] = jnp.where(lane_lo, m[(2 * pair) * qb:(2 * pair + 1) * qb],
                                             m[(2 * pair + 1) * qb:(2 * pair + 2) * qb])
        return carry

    lax.fori_loop(0, r * nb, body, 0, unroll=16)


def _attention_kernel(*refs, seq, chunk):
    qkv = refs[:3 * N_GROUPS]
    bias_ref, o_ref, acc_ref, m_ref, lsw_ref = refs[3 * N_GROUPS:]
    for g, (window, r) in enumerate(DILATED_GROUPS):
        _attn_group(*qkv[3 * g:3 * g + 3], bias_ref, acc_ref, m_ref, lsw_ref, g, r, seq // r, window // (2 * r))

    def combine(i, carry):
        rows = pl.ds(pl.multiple_of(i * chunk, chunk), chunk)
        for pair in range(HEADS_PER_STEP // 2):
            cols = slice(pair * V7X_LANES, (pair + 1) * V7X_LANES)
            slots = [g * (HEADS_PER_STEP // 2) + pair for g in range(N_GROUPS)]
            ms = [m_ref[slot, rows, :] for slot in slots]
            top = functools.reduce(jnp.maximum, ms)
            ws = [jnp.exp2(m - top) for m in ms]
            num = sum(w * acc_ref[slot, rows, :] for slot, w in zip(slots, ws))
            den = sum(w * lsw_ref[slot, rows, :] for slot, w in zip(slots, ws))
            o_ref[0, rows, cols] = (num / den).astype(o_ref.dtype)
        return carry

    lax.fori_loop(0, seq // chunk, combine, 0)


def _band_bias(nside):
    u = jnp.arange(2 * nside)[None, :, None]
    j = jnp.arange(4 * nside)[None, None, :]
    off = (jnp.arange(3) * nside)[:, None, None]
    return jnp.where(jnp.abs(j - u - off) <= nside, 0.0, MASK_VALUE).astype(F32)


def _attention(group_qkv, seq):
    b = group_qkv[0].shape[0]
    n_quads = GROUP_WIDTH // QUAD_WIDTH
    nside = DILATED_GROUPS[0][0] // (2 * DILATED_GROUPS[0][1])
    assert all(w // (2 * r) == nside for w, r in DILATED_GROUPS)
    args, specs = [], []
    for src, (_, r) in zip(group_qkv, DILATED_GROUPS):
        assert src.shape[:3] == (b, r, seq // r)
        for part in range(3):
            args.append(src)
            specs.append(pl.BlockSpec((1, r, seq // r, QUAD_WIDTH),
                                      lambda i, hq, t=part * n_quads: (i, 0, 0, t + hq)))
    args.append(_band_bias(nside))
    specs.append(pl.BlockSpec((3, 2 * nside, 4 * nside), lambda i, hq: (0, 0, 0)))
    return pl.pallas_call(
        functools.partial(_attention_kernel, seq=seq, chunk=COMBINE_ROWS),
        grid=(b, n_quads),
        in_specs=specs,
        out_specs=pl.BlockSpec((1, seq, QUAD_WIDTH), lambda i, hq: (i, 0, hq)),
        out_shape=jax.ShapeDtypeStruct((b, seq, GROUP_WIDTH), BF16),
        scratch_shapes=[pltpu.VMEM((N_GROUPS * HEADS_PER_STEP // 2, seq, V7X_LANES), F32)] * 3,
        compiler_params=_params("parallel", "parallel"),
        name="attention",
    )(*args)


def _filter_kernel(feats_ref, t_ref, w1_ref, b1_ref, freq_ref, w2_ref, b2_ref, w3f_ref, w3b_ref, delta_ref, sgn_ref,
                   cos_hi_ref, cos_lo_ref, sin_hi_ref, sin_lo_ref, kr_ref, ki_ref, sum_ref, dif_ref, nyq_ref, hid_ref):
    fb = pl.program_id(2)
    tc = kr_ref.shape[2]

    dot = functools.partial(jnp.dot, precision=HIGHEST, preferred_element_type=F32)

    @pl.when(jnp.logical_and(fb == 0, jnp.logical_and(pl.program_id(0) == 0, pl.program_id(1) == 0)))
    def _():
        h = jnp.sin(freq_ref[0:1, :] * (dot(feats_ref[...], w1_ref[...]) + b1_ref[...]))
        hid_ref[...] = jnp.sin(freq_ref[1:2, :] * (dot(h, w2_ref[...]) + b2_ref[...]))

    @pl.when(fb == 0)
    def _():
        h = hid_ref[...]
        decay = jnp.exp(-t_ref[...] * jnp.abs(delta_ref[...]))
        fwd = dot(h, w3f_ref[...]) * decay
        bwd = dot(h, w3b_ref[...]) * decay
        row = lax.broadcasted_iota(jnp.int32, (fwd.shape[0], 1), 0)
        bwd = jnp.where(row == 0, 0.0, bwd)
        norm = jnp.sum(jnp.abs(fwd), axis=0, keepdims=True) + jnp.sum(jnp.abs(bwd), axis=0, keepdims=True)
        fwd = fwd / norm
        bwd = bwd / norm
        nyq_ref[...] = jnp.sum(sgn_ref[...] * (fwd + bwd), axis=0, keepdims=True)
        for ref, val in ((sum_ref, fwd + bwd), (dif_ref, fwd - bwd)):
            hi = val.astype(BF16)
            ref[:, :tc] = hi
            ref[:, tc:] = (val - hi.astype(F32)).astype(BF16)

    def dft(m_hi_ref, m_lo_ref, v_ref):
        hi_terms = jnp.dot(m_hi_ref[...], v_ref[...], preferred_element_type=F32)
        return hi_terms[:, :tc] + hi_terms[:, tc:] + jnp.dot(m_lo_ref[...], v_ref[:, :tc], preferred_element_type=F32)

    kr = dft(cos_hi_ref, cos_lo_ref, sum_ref)
    ki = dft(sin_hi_ref, sin_lo_ref, dif_ref)
    row0 = jnp.logical_and(lax.broadcasted_iota(jnp.int32, (kr.shape[0], 1), 0) == 0, fb == 0)
    kr_ref[0] = jnp.where(row0, 0.5 * kr, kr)
    ki_ref[0] = jnp.where(row0, nyq_ref[...], ki)


def _filter_spectrum(feats, t, w1, b1, freq, w2, b2, w3, deltas, sgn, dft_hi_lo, *, tc=HYENA_CHANS, tf=FILTER_FREQ_ROWS):
    length = feats.shape[0]
    hid = w2.shape[0]
    chans = deltas.shape[1]
    ncb = chans // tc
    const = lambda o, cb, fb: (0, 0)
    out = jax.ShapeDtypeStruct((HYENA_ORDER, length, chans), F32)
    return pl.pallas_call(
        _filter_kernel,
        grid=(HYENA_ORDER, ncb, length // tf),
        in_specs=[
            pl.BlockSpec(feats.shape, const),
            pl.BlockSpec((length, 1), const),
            pl.BlockSpec(w1.shape, const),
            pl.BlockSpec((1, hid), const),
            pl.BlockSpec((2, hid), const),
            pl.BlockSpec((hid, hid), const),
            pl.BlockSpec((1, hid), const),
            pl.BlockSpec((hid, tc), lambda o, cb, fb: (0, (2 * o) * ncb + cb)),
            pl.BlockSpec((hid, tc), lambda o, cb, fb: (0, (2 * o + 1) * ncb + cb)),
            pl.BlockSpec((1, tc), lambda o, cb, fb: (0, cb)),
            pl.BlockSpec((length, 1), const),
        ] + [pl.BlockSpec((tf, length), lambda o, cb, fb: (fb, 0))] * 4,
        out_specs=[pl.BlockSpec((1, tf, tc), lambda o, cb, fb: (o, fb, cb))] * 2,
        out_shape=[out, out],
        scratch_shapes=[pltpu.VMEM((length, 2 * tc), BF16)] * 2 + [pltpu.VMEM((1, tc), F32), pltpu.VMEM((length, hid), F32)],
        compiler_params=_params("arbitrary", "arbitrary", "arbitrary"),
        name="hyena_filter",
    )(feats, t, w1, b1.reshape(1, hid), freq, w2, b2.reshape(1, hid), w3, w3, deltas, sgn, *dft_hi_lo)


DFT_ROWS = 512


def _reverse_rows(flip_ref, x):
    h = x.shape[0]
    return jnp.concatenate([jnp.dot(flip_ref[c * DFT_ROWS:(c + 1) * DFT_ROWS, :], x, preferred_element_type=F32)
                            for c in range(h // DFT_ROWS)], axis=0)


def _fold(u_ref, flip_ref):
    h = u_ref.shape[1] // 2
    return u_ref[0, 0:h, :].astype(F32), pltpu.roll(_reverse_rows(flip_ref, u_ref[0, h:, :]), 1, 0)


def _dwconv3_folded(a, b, w_ref, b_ref):
    h = a.shape[0]
    row = lax.broadcasted_iota(jnp.int32, (h, 1), 0)
    a_prev = jnp.where(row == 0, 0.0, pltpu.roll(a, 1, 0))
    a_next = jnp.where(row == h - 1, b[0:1, :], pltpu.roll(a, h - 1, 0))
    b_prev = jnp.where(row == 0, a[h - 1:h, :], pltpu.roll(b, h - 1, 0))
    b_next = jnp.where(row == 1, 0.0, pltpu.roll(b, 1, 0))
    w0, w1, w2, bias = w_ref[0:1, :], w_ref[1:2, :], w_ref[2:3, :], b_ref[...]
    return a_prev * w0 + a * w1 + a_next * w2 + bias, b_prev * w0 + b * w1 + b_next * w2 + bias


def _long_conv_kernel(*refs, z_from_u):
    if z_from_u:
        (uz_ref, cwz_ref, cbz_ref, ug_ref, cwg_ref, cbg_ref, kr_ref, ki_ref, skip_ref, sgn_ref, flip_ref,
         c1_ref, s1_ref, c2_ref, s2_ref, c2t_ref, s2t_ref, o_ref, yre_ref, yie_ref, yro_ref, yio_ref) = refs
        za, zb = _dwconv3_folded(*_fold(uz_ref, flip_ref), cwz_ref, cbz_ref)
    else:
        (z_ref, ug_ref, cwg_ref, cbg_ref, kr_ref, ki_ref, skip_ref, sgn_ref, flip_ref,
         c1_ref, s1_ref, c2_ref, s2_ref, c2t_ref, s2t_ref, o_ref, yre_ref, yie_ref, yro_ref, yio_ref, ob_ref) = refs
        za, zb = z_ref[0, 0], z_ref[0, 1]
    ga, gb = _dwconv3_folded(*_fold(ug_ref, flip_ref), cwg_ref, cbg_ref)
    h = za.shape[0]
    first = lax.broadcasted_iota(jnp.int32, (h, 1), 0) == 0
    sgn = sgn_ref[...]
    p = jnp.where(first, za, za + zb).astype(BF16)
    d = jnp.where(first, za, za - zb).astype(BF16)
    b0 = zb[0:1, :]
    xnyq = jnp.sum(sgn * (za + zb), axis=0, keepdims=True)
    ynyq_half = 0.5 * (xnyq * ki_ref[0, 0:1, :])
    skip = skip_ref[0]
    mid = jnp.zeros_like(b0)
    for c in range(h // DFT_ROWS):
        rows = slice(c * DFT_ROWS, (c + 1) * DFT_ROWS)
        odd_rows = slice(h + c * DFT_ROWS, h + (c + 1) * DFT_ROWS)
        edge = sgn[rows, :] * b0
        xr_e = jnp.dot(c1_ref[rows, :], p, preferred_element_type=F32) + edge
        xi_e = jnp.dot(s1_ref[rows, :], d, preferred_element_type=F32)
        xr_o = jnp.dot(c2_ref[rows, :], d, preferred_element_type=F32)
        xi_o = jnp.dot(s2_ref[rows, :], p, preferred_element_type=F32) - edge
        kr_e, ki_e, kr_o, ki_o = kr_ref[0, rows, :], ki_ref[0, rows, :], kr_ref[0, odd_rows, :], ki_ref[0, odd_rows, :]
        yr_e = xr_e * kr_e - xi_e * ki_e
        yi_o = xr_o * ki_o + xi_o * kr_o
        mid = mid + jnp.sum(sgn[rows, :] * (yr_e - yi_o), axis=0, keepdims=True)
        yre_ref[rows, :] = yr_e.astype(BF16)
        yie_ref[rows, :] = (xr_e * ki_e + xi_e * kr_e).astype(BF16)
        yro_ref[rows, :] = (xr_o * kr_o - xi_o * ki_o).astype(BF16)
        yio_ref[rows, :] = yi_o.astype(BF16)
    inv_n = 1.0 / (2 * h)
    for c in range(h // DFT_ROWS):
        rows = slice(c * DFT_ROWS, (c + 1) * DFT_ROWS)
        pp = (jnp.dot(c1_ref[rows, :], yre_ref[...], preferred_element_type=F32)
              + jnp.dot(s2t_ref[rows, :], yio_ref[...], preferred_element_type=F32)) + sgn[rows, :] * ynyq_half
        qq = (jnp.dot(s1_ref[rows, :], yie_ref[...], preferred_element_type=F32)
              + jnp.dot(c2t_ref[rows, :], yro_ref[...], preferred_element_type=F32))
        ya = (pp + qq) * inv_n
        yb = jnp.where(first[rows, :], mid + ynyq_half, pp - qq) * inv_n
        out_a = ga[rows, :] * (ya + za[rows, :] * skip)
        out_b = gb[rows, :] * (yb + zb[rows, :] * skip)
        if z_from_u:
            o_ref[0, 0, rows, :] = out_a
            o_ref[0, 1, rows, :] = out_b
        else:
            o_ref[0, rows, :] = out_a.astype(o_ref.dtype)
            ob_ref[rows, :] = out_b
    if not z_from_u:
        o_ref[0, h:, :] = _reverse_rows(flip_ref, pltpu.roll(ob_ref[...], h - 1, 0).astype(BF16)).astype(o_ref.dtype)


def _long_conv(z_src, u3d, conv_w, conv_b, kr, ki, skip, sgn, dft, *, order, z_part, gate_part, u_col0, tc=HYENA_CHANS):
    b, seq, _ = u3d.shape
    half = seq // 2
    chans = kr.shape[2]
    ncb = chans // tc
    u_cb0 = u_col0 // tc
    first_order = z_src is None

    def u_specs(part):
        return [
            pl.BlockSpec((1, seq, tc), lambda cb, i: (i, 0, u_cb0 + part * ncb + cb)),
            pl.BlockSpec((3, tc), lambda cb, i: (0, part * ncb + cb)),
            pl.BlockSpec((1, tc), lambda cb, i: (0, part * ncb + cb)),
        ]

    def dft_block(row_block):
        return pl.BlockSpec((half, half), lambda cb, i: (row_block, 0), pipeline_mode=pl.Buffered(1))

    folded_spec = pl.BlockSpec((1, 2, half, tc), lambda cb, i: (i, 0, 0, cb))
    cosm, sinm, cos_odd_t, sin_odd_t, flip = dft
    conv_b2 = conv_b.reshape(1, -1)
    if first_order:
        args, specs = [u3d, conv_w, conv_b2], u_specs(z_part)
    else:
        args, specs = [z_src], [folded_spec]
    args += [u3d, conv_w, conv_b2, kr, ki, skip.reshape(HYENA_ORDER, 1, chans), sgn, flip,
             cosm, sinm, cosm, sinm, cos_odd_t, sin_odd_t]
    specs += u_specs(gate_part) + [
        pl.BlockSpec((1, seq, tc), lambda cb, i: (order, 0, cb)),
        pl.BlockSpec((1, seq, tc), lambda cb, i: (order, 0, cb)),
        pl.BlockSpec((1, 1, tc), lambda cb, i: (order, 0, cb)),
        pl.BlockSpec((half, 1), lambda cb, i: (0, 0)),
        dft_block(0), dft_block(0), dft_block(0), dft_block(1), dft_block(1), dft_block(0), dft_block(0),
    ]
    return pl.pallas_call(
        functools.partial(_long_conv_kernel, z_from_u=first_order),
        grid=(ncb, b),
        in_specs=specs,
        out_specs=folded_spec if first_order else pl.BlockSpec((1, seq, tc), lambda cb, i: (i, 0, cb)),
        out_shape=jax.ShapeDtypeStruct((b, 2, half, chans), F32) if first_order
        else jax.ShapeDtypeStruct((b, seq, chans), BF16),
        scratch_shapes=[pltpu.VMEM((half, tc), BF16)] * 4 + ([] if first_order else [pltpu.VMEM((half, tc), F32)]),
        compiler_params=_params("parallel", "parallel"),
        name=f"long_conv{order}",
    )(*args)


def _merge_kernel(x_ref, g_ref, oa_ref, oh_ref, wga_ref, wgh_ref, bga_ref, bgh_ref, wpa_ref, wph_ref, wo_ref, o_ref, h_ref):
    c = pl.program_id(1)

    @pl.when(c == 0)
    def _():
        x = x_ref[...]
        h_ref[...] = _rmsnorm(x, g_ref[...]).astype(BF16)
        o_ref[...] = x

    h = h_ref[...]
    gate_a = 1.0 / (1.0 + jnp.exp(-(jnp.dot(h, wga_ref[...], preferred_element_type=F32) + bga_ref[...])))
    gate_h = 1.0 / (1.0 + jnp.exp(-(jnp.dot(h, wgh_ref[...], preferred_element_type=F32) + bgh_ref[...])))
    pa = jnp.dot(oa_ref[...], wpa_ref[...], preferred_element_type=F32)
    ph = jnp.dot(oh_ref[...], wph_ref[...], preferred_element_type=F32)
    mixed = (gate_a * pa + gate_h * ph).astype(BF16)
    o_ref[...] += jnp.dot(mixed, wo_ref[...], preferred_element_type=F32)


def _merge(x2d, gain, o_attn, o_hy, w_gate, b_gate, w_pa, w_ph, w_out, layer, *, tm=MERGE_ROWS, tc=MERGE_COLS):
    rows, d = x2d.shape
    ncb = d // tc
    b_gate2 = b_gate.reshape(1, -1)
    return pl.pallas_call(
        _merge_kernel,
        grid=(rows // tm, ncb),
        in_specs=[
            pl.BlockSpec((tm, d), lambda i, c: (i, 0)),
            pl.BlockSpec((1, d), lambda i, c: (0, 0)),
            pl.BlockSpec((tm, o_attn.shape[1]), lambda i, c: (i, 0)),
            pl.BlockSpec((tm, o_hy.shape[1]), lambda i, c: (i, 0)),
            pl.BlockSpec((None, d, tc), lambda i, c: (layer, 0, c)),
            pl.BlockSpec((None, d, tc), lambda i, c: (layer, 0, ncb + c)),
            pl.BlockSpec((1, tc), lambda i, c: (0, c)),
            pl.BlockSpec((1, tc), lambda i, c: (0, ncb + c)),
            pl.BlockSpec((None, w_pa.shape[1], tc), lambda i, c: (layer, 0, c)),
            pl.BlockSpec((None, w_ph.shape[1], tc), lambda i, c: (layer, 0, c)),
            pl.BlockSpec((None, tc, d), lambda i, c: (layer, c, 0)),
        ],
        out_specs=pl.BlockSpec((tm, d), lambda i, c: (i, 0)),
        out_shape=jax.ShapeDtypeStruct((rows, d), F32),
        scratch_shapes=[pltpu.VMEM((tm, d), BF16)],
        compiler_params=_params("parallel", "arbitrary"),
        name="merge",
    )(x2d, gain.reshape(1, d), o_attn, o_hy, w_gate, w_gate, b_gate2, b_gate2, w_pa, w_ph, w_out)


HALO = 16


def _conv_ffn_kernel(*refs, blocks_per_seq, final_norm):
    if final_norm:
        x_ref, xp_ref, xn_ref, g_ref, wa_ref, wb_ref, cw_ref, cb_ref, wd_ref, gf_ref, o_ref, h_ref = refs
    else:
        x_ref, xp_ref, xn_ref, g_ref, wa_ref, wb_ref, cw_ref, cb_ref, wd_ref, o_ref, h_ref = refs
    i, f = pl.program_id(0), pl.program_id(1)
    tm = x_ref.shape[0]

    @pl.when(f == 0)
    def _():
        x = x_ref[...]
        g = g_ref[...]
        h_ref[0:HALO, :] = _rmsnorm(xp_ref[...], g).astype(BF16)
        h_ref[HALO:HALO + tm, :] = _rmsnorm(x, g).astype(BF16)
        h_ref[HALO + tm:, :] = _rmsnorm(xn_ref[...], g).astype(BF16)
        o_ref[...] = x

    up_a = jnp.dot(h_ref[...], wa_ref[...], preferred_element_type=F32)
    up_b = jnp.dot(h_ref[HALO:HALO + tm, :], wb_ref[...], preferred_element_type=F32)
    n_ext = up_a.shape[0]
    row = lax.broadcasted_iota(jnp.int32, (tm, 1), 0)
    seq_pos = i % blocks_per_seq
    at_start = jnp.logical_and(row == 0, seq_pos == 0)
    at_end = jnp.logical_and(row == tm - 1, seq_pos == blocks_per_seq - 1)
    a_prev = jnp.where(at_start, 0.0, pltpu.roll(up_a, 1, 0)[HALO:HALO + tm])
    a_next = jnp.where(at_end, 0.0, pltpu.roll(up_a, n_ext - 1, 0)[HALO:HALO + tm])
    a = a_prev * cw_ref[0:1, :] + up_a[HALO:HALO + tm] * cw_ref[1:2, :] + a_next * cw_ref[2:3, :] + cb_ref[...]
    gelu = 0.5 * a * (1.0 + jnp.tanh(math.sqrt(2.0 / math.pi) * (a + 0.044715 * (a * a * a))))
    o_ref[...] += jnp.dot((gelu * up_b).astype(BF16), wd_ref[...], preferred_element_type=F32)

    if final_norm:
        @pl.when(f == pl.num_programs(1) - 1)
        def _():
            o_ref[...] = _rmsnorm(o_ref[...], gf_ref[...])


def _conv_ffn(x2d, gain, w_up, conv_w, conv_b, w_down, layer, final_gain, *, seq, tm=FFN_ROWS, tf=FFN_COLS):
    rows, d = x2d.shape
    d_ff = w_down.shape[1]
    nfb = d_ff // tf
    halo_per_block = tm // HALO
    n_halo_blocks = rows // HALO
    args = [x2d, x2d, x2d, gain.reshape(1, d), w_up, w_up, conv_w, conv_b.reshape(1, d_ff), w_down]
    specs = [
        pl.BlockSpec((tm, d), lambda i, f: (i, 0)),
        pl.BlockSpec((HALO, d), lambda i, f: (jnp.maximum(i * halo_per_block - 1, 0), 0)),
        pl.BlockSpec((HALO, d), lambda i, f: (jnp.minimum((i + 1) * halo_per_block, n_halo_blocks - 1), 0)),
        pl.BlockSpec((1, d), lambda i, f: (0, 0)),
        pl.BlockSpec((None, d, tf), lambda i, f: (layer, 0, f)),
        pl.BlockSpec((None, d, tf), lambda i, f: (layer, 0, nfb + f)),
        pl.BlockSpec((3, tf), lambda i, f: (0, f)),
        pl.BlockSpec((1, tf), lambda i, f: (0, f)),
        pl.BlockSpec((None, tf, d), lambda i, f: (layer, f, 0)),
    ]
    if final_gain is not None:
        args.append(final_gain.reshape(1, d))
        specs.append(pl.BlockSpec((1, d), lambda i, f: (0, 0)))
    return pl.pallas_call(
        functools.partial(_conv_ffn_kernel, blocks_per_seq=seq // tm, final_norm=final_gain is not None),
        grid=(rows // tm, nfb),
        in_specs=specs,
        out_specs=pl.BlockSpec((tm, d), lambda i, f: (i, 0)),
        out_shape=jax.ShapeDtypeStruct((rows, d), F32),
        scratch_shapes=[pltpu.VMEM((tm + 2 * HALO, d), BF16)],
        compiler_params=_params("parallel", "arbitrary"),
        name="conv_ffn",
    )(*args)


def _rope_tables(seq):
    pos = jnp.arange(seq, dtype=F32)
    inv = 1.0 / (ROPE_THETA ** (jnp.arange(0, HEAD_DIM, 2, dtype=F32) / HEAD_DIM))
    ang = pos[:, None] * inv[None, :]
    ang = jnp.concatenate([ang] * (V7X_LANES // HALF_DIM), axis=-1)
    return jnp.cos(ang), jnp.sin(ang)


def _arrange_w_in(w_in):
    layers, d, _ = w_in.shape
    quads = HEADS_PER_GROUP // HEADS_PER_STEP
    qk = w_in[:, :, :2 * ATTN_WIDTH].reshape(layers, d, 2, N_GROUPS, quads, HEADS_PER_STEP, 2, HALF_DIM)
    qk = qk.transpose(0, 1, 2, 3, 4, 6, 5, 7).reshape(layers, d, 2, N_GROUPS, GROUP_WIDTH)
    v = w_in[:, :, 2 * ATTN_WIDTH:3 * ATTN_WIDTH].reshape(layers, d, N_GROUPS, GROUP_WIDTH)
    groups = [jnp.concatenate([qk[:, :, 0, g], qk[:, :, 1, g], v[:, :, g]], axis=-1) for g in range(N_GROUPS)]
    return jnp.concatenate([groups[0], w_in[:, :, 3 * ATTN_WIDTH:]] + groups[1:], axis=-1)


DFT_COARSE = 64


def _hyena_tables(length, chans):
    t = jnp.linspace(0.0, 1.0, length, dtype=F32)[:, None]
    bands = (HYENA_EMB_DIM - 1) // 2
    w = 2.0 * math.pi * jnp.arange(length, dtype=F32)[:, None] / length
    f = jnp.linspace(1e-4, bands - 1, bands, dtype=F32)[None, :]
    feats = jnp.concatenate([t, jnp.cos(f * w), -jnp.sin(f * w)], axis=-1)
    feats = jnp.pad(feats, ((0, 0), (0, V7X_LANES - HYENA_EMB_DIM)))
    max_decay = math.log(HYENA_DECAY_TARGET) / HYENA_FAST_DECAY
    min_decay = math.log(HYENA_DECAY_TARGET) / HYENA_SLOW_DECAY
    deltas = jnp.linspace(min_decay, max_decay, chans, dtype=F32)[None, :]
    idx = jnp.arange(length, dtype=jnp.int32)

    def trig(k):
        a = (k % (2 * length)).astype(F32) * (math.pi / length)
        return jnp.cos(a), jnp.sin(a)

    fine = jnp.arange(DFT_COARSE, dtype=jnp.int32)
    ca, sa = trig((DFT_COARSE * jnp.arange(length // DFT_COARSE, dtype=jnp.int32))[None, :, None, None] * idx)
    cb, sb = trig(jnp.stack([fine[0::2], fine[1::2]])[:, None, :, None] * idx)
    cosm = (ca * cb - sa * sb).reshape(length, length)
    sinm = (sa * cb + ca * sb).reshape(length, length)
    sgn = (1 - 2 * (idx % 2)).astype(F32)[:, None]
    return feats, t, deltas, sgn, cosm, -sinm


def kernel(x, attn_norm, w_in, hy_conv_w, hy_conv_b, f_w1, f_b1, f_freq, f_w2, f_b2, f_w3, hy_skip, w_proj_attn, w_proj_hyena, w_gate, b_gate, w_out, ffn_norm, w_up, ffn_conv_w, ffn_conv_b, w_down, final_norm):
    b, seq, d = x.shape
    depth = w_in.shape[0]
    chans = hy_skip.shape[2]
    cos, sin = _rope_tables(seq)
    feats, t, deltas, sgn, cosm, sinm = _hyena_tables(seq, chans)
    cosm16, sinm16 = cosm.astype(BF16), sinm.astype(BF16)
    dft_hi_lo = (cosm16, (cosm - cosm16.astype(F32)).astype(BF16), sinm16, (sinm - sinm16.astype(F32)).astype(BF16))
    half = seq // 2
    flip = (jnp.arange(half)[:, None] + jnp.arange(half)[None, :] == half - 1).astype(BF16)
    dft16 = (cosm16, sinm16, cosm16[half:, :half].T, sinm16[half:, :half].T, flip)
    w1_pad = jnp.pad(f_w1, ((0, 0), (0, V7X_LANES - HYENA_EMB_DIM), (0, 0)))
    w_in16 = _arrange_w_in(w_in).astype(BF16)
    w_gate16, w_pa16, w_ph16, w_out16, w_up16, w_down16 = (
        w.astype(BF16) for w in (w_gate, w_proj_attn, w_proj_hyena, w_out, w_up, w_down))
    n_nat_blocks = (w_in.shape[2] - 2 * QKV_WIDTH) // QKV_WIDTH

    x2d = x.reshape(b * seq, d)
    for l in range(depth):
        nat, h2d = _norm_proj(x2d, attn_norm[l], w_in16, l, n_nat_blocks, cos, sin, seq=seq)
        nat3d = nat.reshape(b, seq, -1)
        group_qkv = [nat3d.reshape(b, 1, seq, -1)] + [
            _dilated_proj(h2d, w_in16, l, n_nat_blocks + g - 1, r, cos, sin, batch=b, seq=seq)
            for g, (_, r) in enumerate(DILATED_GROUPS) if g > 0]
        o_attn = _attention(group_qkv, seq)
        kr, ki = _filter_spectrum(feats, t, w1_pad[l], f_b1[l], f_freq[l], f_w2[l], f_b2[l], f_w3[l], deltas, sgn, dft_hi_lo)
        conv = functools.partial(_long_conv, u3d=nat3d, conv_w=hy_conv_w[l], conv_b=hy_conv_b[l], kr=kr, ki=ki,
                                 skip=hy_skip[l], sgn=sgn[:half], dft=dft16, u_col0=QKV_WIDTH)
        z1 = conv(None, order=0, z_part=0, gate_part=1)
        o_hy = conv(z1, order=1, z_part=None, gate_part=2)
        x2d = _merge(x2d, attn_norm[l], o_attn.reshape(b * seq, -1), o_hy.reshape(b * seq, -1),
                     w_gate16, b_gate[l], w_pa16, w_ph16, w_out16, l)
        x2d = _conv_ffn(x2d, ffn_norm[l], w_up16, ffn_conv_w[l], ffn_conv_b[l], w_down16, l,
                        final_norm if l == depth - 1 else None, seq=seq)
    return x2d.reshape(b, seq, d)
```
